```python
import math
import jax, jax.numpy as jnp
from jax import lax
import numpy as np

D_MODEL = 1024
BATCH = 8
SEQ = 2048
DEPTH = 2
DEC_BATCH = 32
DEC_SEQ = 8
PAST_LEN = 8192
PAGE_SIZE = 128

HEAD_DIM = 64
A_HEADS = 8
A_KV_HEADS = 2
IDX_HEADS = 8
IDX_DIM = 64
TOPK_MAX = 256
B_HEADS = 8
C_HEADS = 8
C_KV_HEADS = 4
D_FF = 4 * D_MODEL
ROPE_THETA = 10000.0
NORM_EPS = 1e-6
SUBLN_EPS = 1e-5
Q_BLOCK = 128

A_Q_W = A_HEADS * HEAD_DIM
A_KV_W = A_KV_HEADS * HEAD_DIM
IDX_Q_W = IDX_HEADS * IDX_DIM
B_W = B_HEADS * HEAD_DIM
AB_SECTIONS = (A_Q_W, A_KV_W, A_KV_W, IDX_Q_W, IDX_DIM, IDX_HEADS, B_W, B_W, B_W)
AB_IN_W = A_Q_W + 2 * A_KV_W + IDX_Q_W + IDX_DIM + IDX_HEADS + 3 * B_W
AB_OUT_W = A_Q_W + B_W
C_Q_W = C_HEADS * 2 * HEAD_DIM
C_KV_W = C_KV_HEADS * 2 * HEAD_DIM
C_IN_W = C_Q_W + 2 * C_KV_W

kernel_name = 'hybrid_dsa_stickbreak_diffattn_step'


def rms_norm(x, g, eps=NORM_EPS):
    xf = x.astype(jnp.float32)
    y = xf * lax.rsqrt(jnp.mean(xf * xf, axis=-1, keepdims=True) + eps)
    return (y * g.astype(jnp.float32)).astype(x.dtype)


def rope(x, pos):
    half = x.shape[-1] // 2
    inv_freq = ROPE_THETA ** (-jnp.arange(half, dtype=jnp.float32) / half)
    ang = pos.astype(jnp.float32)[:, None] * inv_freq[None, :]
    cos = jnp.cos(ang)[None, :, None, :]
    sin = jnp.sin(ang)[None, :, None, :]
    xf = x.astype(jnp.float32)
    x1, x2 = xf[..., :half], xf[..., half:]
    return jnp.concatenate([x1 * cos - x2 * sin, x2 * cos + x1 * sin], axis=-1).astype(x.dtype)


def split_cols(x, sizes):
    offsets = [int(o) for o in np.cumsum(sizes)[:-1]]
    return jnp.split(x, offsets, axis=-1)


def gather_pages(cache, page_table):
    g = cache[page_table]
    return g.reshape(g.shape[0], g.shape[1] * g.shape[2], *g.shape[3:])


def sweep_query_blocks(fn, qs, qpos):
    t = qpos.shape[0]
    if t % Q_BLOCK != 0:
        return fn(qs, qpos)
    nb = t // Q_BLOCK

    def to_blocks(a):
        return jnp.moveaxis(a.reshape(a.shape[0], nb, Q_BLOCK, *a.shape[2:]), 1, 0)

    out = lax.map(lambda args: fn(args[0], args[1]),
                  (tuple(to_blocks(a) for a in qs), qpos.reshape(nb, Q_BLOCK)))
    out = jnp.moveaxis(out, 0, 1)
    return out.reshape(out.shape[0], t, out.shape[-1])


def dsa_attend(q, q_idx, w_idx, qpos, k, v, k_idx, kpos, n_sel):
    f32 = jnp.float32
    causal = kpos[None, :] <= qpos[:, None]
    logits = jnp.einsum('bthd,bsd->bths', q_idx.astype(f32), k_idx.astype(f32)) * IDX_DIM ** -0.5
    score = jnp.einsum('bth,bths->bts', w_idx.astype(f32), jax.nn.relu(logits))
    score = jnp.where(causal[None], score, -jnp.inf)
    _, sel = lax.top_k(score, n_sel)
    valid = kpos[sel] <= qpos[None, :, None]
    take = jax.vmap(lambda rows, idx: rows[idx])
    k_sel = take(k, sel).astype(f32)
    v_sel = take(v, sel).astype(f32)
    bsz, tq, nh, dh = q.shape
    g = k.shape[2]
    qg = q.reshape(bsz, tq, g, nh // g, dh).astype(f32)
    s = jnp.einsum('btgrd,btkgd->btgrk', qg, k_sel) * dh ** -0.5
    s = jnp.where(valid[:, :, None, None, :], s, -jnp.inf)
    p = jax.nn.softmax(s, axis=-1)
    o = jnp.einsum('btgrk,btkgd->btgrd', p, v_sel)
    return o.reshape(bsz, tq, nh, dh).astype(q.dtype)


def stick_breaking_attend(q, qpos, k, v, kpos):
    f32 = jnp.float32
    dh = q.shape[-1]
    z = jnp.einsum('bthd,bshd->bhts', q.astype(f32), k.astype(f32)) * dh ** -0.5
    strict = (kpos[None, :] < qpos[:, None])[None, None]
    log_stay = jnp.where(strict, jax.nn.log_sigmoid(-z), 0.0)
    shifted = jnp.concatenate([log_stay[..., 1:], jnp.zeros_like(log_stay[..., :1])], axis=-1)
    log_pass = lax.cumsum(shifted, axis=shifted.ndim - 1, reverse=True)
    weight = jnp.where(strict, jnp.exp(jax.nn.log_sigmoid(z) + log_pass), 0.0)
    o = jnp.einsum('bhts,bshd->bthd', weight, v.astype(f32))
    return o.astype(q.dtype)


def diff_attend(q, qpos, k, v, kpos, lam):
    f32 = jnp.float32
    bsz, tq, nh, _ = q.shape
    g = k.shape[2]
    qg = q.reshape(bsz, tq, g, nh // g, 2, HEAD_DIM).astype(f32)
    kg = k.reshape(bsz, k.shape[1], g, 2, HEAD_DIM).astype(f32)
    s = jnp.einsum('btgrcd,bsgcd->bcgrts', qg, kg) * HEAD_DIM ** -0.5
    causal = kpos[None, :] <= qpos[:, None]
    s = jnp.where(causal, s, -jnp.inf)
    p = jax.nn.softmax(s, axis=-1)
    p = p[:, 0] - lam * p[:, 1]
    o = jnp.einsum('bgrts,bsge->btgre', p, v.astype(f32))
    return o.reshape(bsz, tq, nh, 2 * HEAD_DIM)


def mixer_ab(h, pos, past, w_in, w_out):
    bsz, t, _ = h.shape
    a_q, a_k, a_v, i_q, i_k, i_w, b_q, b_k, b_v = split_cols(h @ w_in, AB_SECTIONS)
    a_q = rope(a_q.reshape(bsz, t, A_HEADS, HEAD_DIM), pos)
    a_k = rope(a_k.reshape(bsz, t, A_KV_HEADS, HEAD_DIM), pos)
    a_v = a_v.reshape(bsz, t, A_KV_HEADS, HEAD_DIM)
    i_q = rope(i_q.reshape(bsz, t, IDX_HEADS, IDX_DIM), pos)
    i_k = rope(i_k.reshape(bsz, t, 1, IDX_DIM), pos).reshape(bsz, t, IDX_DIM)
    i_w = i_w * IDX_HEADS ** -0.5
    b_q = b_q.reshape(bsz, t, B_HEADS, HEAD_DIM)
    b_k = b_k.reshape(bsz, t, B_HEADS, HEAD_DIM)
    b_v = b_v.reshape(bsz, t, B_HEADS, HEAD_DIM)
    new_rows = (a_k, a_v, i_k, b_k, b_v)
    if past is None:
        keys = new_rows
    else:
        keys = tuple(jnp.concatenate([pr, nr.astype(pr.dtype)], axis=1) for pr, nr in zip(past, new_rows))
    ka, va, kidx, kb, vb = keys
    n_keys = ka.shape[1]
    kpos = jnp.arange(n_keys, dtype=jnp.int32)
    n_sel = min(TOPK_MAX, n_keys // 4)

    def mix(qs, qpos):
        qa, qi, wi, qb = qs
        oa = dsa_attend(qa, qi, wi, qpos, ka, va, kidx, kpos, n_sel)
        ob = stick_breaking_attend(qb, qpos, kb, vb, kpos)
        nbat, tq = qa.shape[:2]
        return jnp.concatenate([oa.reshape(nbat, tq, A_Q_W), ob.reshape(nbat, tq, B_W)], axis=-1).astype(h.dtype)

    o = sweep_query_blocks(mix, (a_q, i_q, i_w, b_q), pos)
    return o @ w_out, new_rows


def mixer_c(h, pos, past, w_in, w_out, lq1, lk1, lq2, lk2, g_sub, lambda_init):
    f32 = jnp.float32
    bsz, t, _ = h.shape
    q, k, v = split_cols(h @ w_in, (C_Q_W, C_KV_W, C_KV_W))
    q = rope(q.reshape(bsz, t, 2 * C_HEADS, HEAD_DIM), pos).reshape(bsz, t, C_HEADS, 2 * HEAD_DIM)
    k = rope(k.reshape(bsz, t, 2 * C_KV_HEADS, HEAD_DIM), pos).reshape(bsz, t, C_KV_HEADS, 2 * HEAD_DIM)
    v = v.reshape(bsz, t, C_KV_HEADS, 2 * HEAD_DIM)
    new_rows = (k, v)
    if past is None:
        k_all, v_all = new_rows
    else:
        k_all = jnp.concatenate([past[0], k.astype(past[0].dtype)], axis=1)
        v_all = jnp.concatenate([past[1], v.astype(past[1].dtype)], axis=1)
    kpos = jnp.arange(k_all.shape[1], dtype=jnp.int32)
    lam = (jnp.exp(jnp.sum(lq1.astype(f32) * lk1.astype(f32)))
           - jnp.exp(jnp.sum(lq2.astype(f32) * lk2.astype(f32))) + lambda_init)

    def mix(qs, qpos):
        o = diff_attend(qs[0], qpos, k_all, v_all, kpos, lam)
        o = rms_norm(o, g_sub, SUBLN_EPS) * (1.0 - lambda_init)
        return o.reshape(o.shape[0], o.shape[1], C_Q_W).astype(h.dtype)

    o = sweep_query_blocks(mix, (q,), pos)
    return o @ w_out, new_rows


def sq_relu_mlp(h, w_up, w_down):
    u = jax.nn.relu(h @ w_up)
    return (u * u) @ w_down


def run_trunk(x, pos, caches_ab, caches_c, page_table, p):
    rows_ab, rows_c = [], []
    for layer in range(DEPTH):
        j = layer // 2
        h = rms_norm(x, p['norm_mix_pre'][layer])
        if layer % 2 == 0:
            past = None if page_table is None else tuple(gather_pages(c[j], page_table) for c in caches_ab)
            mix, rows = mixer_ab(h, pos, past, p['w_in_ab'][j], p['w_out_ab'][j])
            rows_ab.append(rows)
        else:
            past = None if page_table is None else tuple(gather_pages(c[j], page_table) for c in caches_c)
            lambda_init = 0.8 - 0.6 * math.exp(-0.3 * layer)
            mix, rows = mixer_c(h, pos, past, p['w_in_c'][j], p['w_out_c'][j],
                                p['c_lambda_q1'][j], p['c_lambda_k1'][j], p['c_lambda_q2'][j], p['c_lambda_k2'][j],
                                p['c_subln'][j], lambda_init)
            rows_c.append(rows)
        x = x + rms_norm(mix, p['norm_mix_post'][layer])
        f = sq_relu_mlp(rms_norm(x, p['norm_ffn_pre'][layer]), p['w_up'][layer], p['w_down'][layer])
        x = x + rms_norm(f, p['norm_ffn_post'][layer])
    return x, rows_ab, rows_c


def stack_rows(rows, i):
    return jnp.stack([r[i] for r in rows])


def setup_inputs(seed: int = 0) -> dict:
    key = jax.random.key(seed)
    ks = iter(jax.random.split(key, 32))
    n_even = (DEPTH + 1) // 2
    n_odd = DEPTH // 2
    n_pages = PAST_LEN // PAGE_SIZE
    n_used = DEC_BATCH * n_pages
    n_phys = n_used + max(1, n_used // 4)

    def nrm(shape, scale=1.0):
        return jax.random.normal(next(ks), shape, jnp.float32) * scale

    x_prompt = nrm((BATCH, SEQ, D_MODEL))
    x_sample = nrm((DEC_BATCH, DEC_SEQ, D_MODEL))
    cache_a_k = nrm((n_even, n_phys, PAGE_SIZE, A_KV_HEADS, HEAD_DIM))
    cache_a_v = nrm((n_even, n_phys, PAGE_SIZE, A_KV_HEADS, HEAD_DIM))
    cache_a_kidx = nrm((n_even, n_phys, PAGE_SIZE, IDX_DIM))
    cache_b_k = nrm((n_even, n_phys, PAGE_SIZE, B_HEADS, HEAD_DIM))
    cache_b_v = nrm((n_even, n_phys, PAGE_SIZE, B_HEADS, HEAD_DIM))
    cache_c_k = nrm((n_odd, n_phys, PAGE_SIZE, C_KV_HEADS, 2 * HEAD_DIM))
    cache_c_v = nrm((n_odd, n_phys, PAGE_SIZE, C_KV_HEADS, 2 * HEAD_DIM))
    page_table = jax.random.permutation(next(ks), n_phys)[:n_used].reshape(DEC_BATCH, n_pages).astype(jnp.int32)
    norm_mix_pre = 1.0 + nrm((DEPTH, D_MODEL), 0.05)
    norm_mix_post = 1.0 + nrm((DEPTH, D_MODEL), 0.05)
    norm_ffn_pre = 1.0 + nrm((DEPTH, D_MODEL), 0.05)
    norm_ffn_post = 1.0 + nrm((DEPTH, D_MODEL), 0.05)
    w_in_ab = nrm((n_even, D_MODEL, AB_IN_W), D_MODEL ** -0.5)
    w_out_ab = nrm((n_even, AB_OUT_W, D_MODEL), AB_OUT_W ** -0.5)
    w_in_c = nrm((n_odd, D_MODEL, C_IN_W), D_MODEL ** -0.5)
    w_out_c = nrm((n_odd, C_Q_W, D_MODEL), C_Q_W ** -0.5)
    c_lambda_q1 = nrm((n_odd, HEAD_DIM), 0.1)
    c_lambda_k1 = nrm((n_odd, HEAD_DIM), 0.1)
    c_lambda_q2 = nrm((n_odd, HEAD_DIM), 0.1)
    c_lambda_k2 = nrm((n_odd, HEAD_DIM), 0.1)
    c_subln = 1.0 + nrm((n_odd, 2 * HEAD_DIM), 0.05)
    w_up = nrm((DEPTH, D_MODEL, D_FF), D_MODEL ** -0.5)
    w_down = nrm((DEPTH, D_FF, D_MODEL), D_FF ** -0.5)
    return {'x_prompt': x_prompt, 'x_sample': x_sample,
            'cache_a_k': cache_a_k, 'cache_a_v': cache_a_v, 'cache_a_kidx': cache_a_kidx,
            'cache_b_k': cache_b_k, 'cache_b_v': cache_b_v,
            'cache_c_k': cache_c_k, 'cache_c_v': cache_c_v,
            'page_table': page_table,
            'norm_mix_pre': norm_mix_pre, 'norm_mix_post': norm_mix_post,
            'norm_ffn_pre': norm_ffn_pre, 'norm_ffn_post': norm_ffn_post,
            'w_in_ab': w_in_ab, 'w_out_ab': w_out_ab, 'w_in_c': w_in_c, 'w_out_c': w_out_c,
            'c_lambda_q1': c_lambda_q1, 'c_lambda_k1': c_lambda_k1,
            'c_lambda_q2': c_lambda_q2, 'c_lambda_k2': c_lambda_k2, 'c_subln': c_subln,
            'w_up': w_up, 'w_down': w_down}


def reference(x_prompt, x_sample, cache_a_k, cache_a_v, cache_a_kidx, cache_b_k, cache_b_v,
              cache_c_k, cache_c_v, page_table, norm_mix_pre, norm_mix_post, norm_ffn_pre, norm_ffn_post,
              w_in_ab, w_out_ab, w_in_c, w_out_c, c_lambda_q1, c_lambda_k1, c_lambda_q2, c_lambda_k2,
              c_subln, w_up, w_down):
    params = {'norm_mix_pre': norm_mix_pre, 'norm_mix_post': norm_mix_post,
              'norm_ffn_pre': norm_ffn_pre, 'norm_ffn_post': norm_ffn_post,
              'w_in_ab': w_in_ab, 'w_out_ab': w_out_ab, 'w_in_c': w_in_c, 'w_out_c': w_out_c,
              'c_lambda_q1': c_lambda_q1, 'c_lambda_k1': c_lambda_k1,
              'c_lambda_q2': c_lambda_q2, 'c_lambda_k2': c_lambda_k2, 'c_subln': c_subln,
              'w_up': w_up, 'w_down': w_down}
    seq = x_prompt.shape[1]
    dec_seq = x_sample.shape[1]
    past_len = page_table.shape[1] * cache_a_k.shape[2]
    pos_p = jnp.arange(seq, dtype=jnp.int32)
    pos_s = jnp.arange(past_len, past_len + dec_seq, dtype=jnp.int32)

    y_prompt, rows_ab_p, rows_c_p = run_trunk(x_prompt, pos_p, None, None, None, params)
    y_sample, rows_ab_s, rows_c_s = run_trunk(
        x_sample, pos_s, (cache_a_k, cache_a_v, cache_a_kidx, cache_b_k, cache_b_v),
        (cache_c_k, cache_c_v), page_table, params)

    new_a_k_p = stack_rows(rows_ab_p, 0)
    new_a_v_p = stack_rows(rows_ab_p, 1)
    new_a_kidx_p = stack_rows(rows_ab_p, 2)
    new_b_k_p = stack_rows(rows_ab_p, 3)
    new_b_v_p = stack_rows(rows_ab_p, 4)
    new_c_k_p = stack_rows(rows_c_p, 0)
    new_c_v_p = stack_rows(rows_c_p, 1)
    new_a_k_s = stack_rows(rows_ab_s, 0)
    new_a_v_s = stack_rows(rows_ab_s, 1)
    new_a_kidx_s = stack_rows(rows_ab_s, 2)
    new_b_k_s = stack_rows(rows_ab_s, 3)
    new_b_v_s = stack_rows(rows_ab_s, 4)
    new_c_k_s = stack_rows(rows_c_s, 0)
    new_c_v_s = stack_rows(rows_c_s, 1)
    return (y_prompt, y_sample,
            new_a_k_p, new_a_v_p, new_a_kidx_p, new_b_k_p, new_b_v_p, new_c_k_p, new_c_v_p,
            new_a_k_s, new_a_v_s, new_a_kidx_s, new_b_k_s, new_b_v_s, new_c_k_s, new_c_v_s)
```

```python
import functools
import math

import jax
import jax.numpy as jnp
from jax import lax
from jax.experimental import pallas as pl
from jax.experimental.pallas import tpu as pltpu

F32 = jnp.float32
BF16 = jnp.bfloat16
I32 = jnp.int32

HEAD_DIM = 64
A_HEADS = 8
A_KV_HEADS = 2
IDX_HEADS = 8
IDX_DIM = 64
TOPK_MAX = 256
B_HEADS = 8
C_HEADS = 8
C_KV_HEADS = 4
ROPE_THETA = 10000.0
NORM_EPS = 1e-6
SUBLN_EPS = 1e-5

LANES = 128
QB = 128
ROW_TILE = 256
MASKED = -1e30
QK_SCALE = HEAD_DIM ** -0.5
VMEM_LIMIT = 56 * 1024 * 1024


def _cparams(*sem):
    return pltpu.CompilerParams(dimension_semantics=sem, vmem_limit_bytes=VMEM_LIMIT)


def _rms(x, g, eps):
    return x * lax.rsqrt(jnp.mean(x * x, axis=-1, keepdims=True) + eps) * g


def _dot(a, b):
    return jnp.dot(a, b, preferred_element_type=F32)


def _dot_nt(a, b):
    return lax.dot_general(a, b, (((1,), (1,)), ((), ())), preferred_element_type=F32)


def _rope_chunk(seg, cos, sin_signed, first_half):
    partner = jnp.where(first_half, pltpu.roll(seg, LANES - HEAD_DIM // 2, 1), pltpu.roll(seg, HEAD_DIM // 2, 1))
    return seg * cos + partner * sin_signed


def _first_half_mask(shape):
    lane = lax.broadcasted_iota(I32, shape, 1)
    return (lane & (HEAD_DIM - 1)) < HEAD_DIM // 2


_AB_AQ, _AB_IQ, _AB_AK, _AB_AV, _AB_BQ, _AB_BK, _AB_BV, _AB_TAIL, _AB_WIDTH = 0, 512, 1024, 1152, 1280, 1792, 2304, 2816, 2944


def _prep_w_ab(w):
    d = w.shape[0]
    sizes = (A_HEADS * HEAD_DIM, A_KV_HEADS * HEAD_DIM, A_KV_HEADS * HEAD_DIM, IDX_HEADS * IDX_DIM, IDX_DIM,
             IDX_HEADS, B_HEADS * HEAD_DIM, B_HEADS * HEAD_DIM, B_HEADS * HEAD_DIM)
    offs = [0]
    for s in sizes:
        offs.append(offs[-1] + s)
    a_q, a_k, a_v, i_q, i_k, i_w, b_q, b_k, b_v = (w[:, offs[i]:offs[i + 1]] for i in range(9))
    pad = jnp.zeros((d, _AB_WIDTH - _AB_TAIL - IDX_DIM - IDX_HEADS), w.dtype)
    return jnp.concatenate([a_q, i_q, a_k, a_v, b_q, b_k, b_v, i_k, i_w, pad], axis=1).astype(BF16)


def _in_ab_body(x_ref, g_ref, w_ref, cos_ref, sin_ref,
                aq_o, iq_o, bq_o, ak_o, akh_o, av_o, avh_o, ik_o, ikh_o, iw_o, bk_o, bkh_o, bv_o, bvh_o):
    xn = _rms(x_ref[...], g_ref[...], NORM_EPS).astype(BF16)
    cos = cos_ref[...]
    sin = sin_ref[...]
    first = _first_half_mask(cos.shape)

    def proj(lo, width):
        return _dot(xn, w_ref[:, lo:lo + width])

    def split_heads(y, out_ref, scale=None):
        for h in range(y.shape[1] // HEAD_DIM):
            v = y[:, h * HEAD_DIM:(h + 1) * HEAD_DIM]
            out_ref[h] = (v if scale is None else v * scale).astype(out_ref.dtype)

    def rope_wide(y):
        return jnp.concatenate(
            [_rope_chunk(y[:, c * LANES:(c + 1) * LANES], cos, sin, first) for c in range(y.shape[1] // LANES)], axis=1)

    split_heads(rope_wide(proj(_AB_AQ, 512)), aq_o, QK_SCALE)
    split_heads(rope_wide(proj(_AB_IQ, 512)), iq_o, QK_SCALE)
    split_heads(proj(_AB_BQ, 512), bq_o, QK_SCALE)

    ak = _rope_chunk(proj(_AB_AK, 128), cos, sin, first)
    ak_o[...] = ak
    split_heads(ak, akh_o)
    av = proj(_AB_AV, 128)
    av_o[...] = av
    split_heads(av, avh_o)

    bk = proj(_AB_BK, 512)
    bk_o[...] = bk
    split_heads(bk, bkh_o)
    bv = proj(_AB_BV, 512)
    bv_o[...] = bv
    split_heads(bv, bvh_o)

    tail = proj(_AB_TAIL, 128)
    ik = _rope_chunk(tail, cos, sin, first)[:, :IDX_DIM]
    ik_o[...] = ik
    ikh_o[...] = ik.astype(BF16)
    iw_o[...] = tail[:, IDX_DIM:IDX_DIM + IDX_HEADS] * (IDX_HEADS ** -0.5)


def _in_proj_ab(x2d, g, w_perm, cos, sin):
    m, d = x2d.shape
    tm = min(ROW_TILE, m)
    n_pos_blocks = cos.shape[0] // tm
    row = lambda i: (i, 0)
    hrow = lambda i: (0, i, 0)
    pos = lambda i: (i % n_pos_blocks, 0)
    const = lambda i: (0, 0)
    hm = lambda heads: jax.ShapeDtypeStruct((heads, m, HEAD_DIM), BF16)
    out_shape = (
        hm(A_HEADS), hm(IDX_HEADS), hm(B_HEADS),
        jax.ShapeDtypeStruct((m, 128), F32), hm(A_KV_HEADS),
        jax.ShapeDtypeStruct((m, 128), F32), hm(A_KV_HEADS),
        jax.ShapeDtypeStruct((m, IDX_DIM), F32), jax.ShapeDtypeStruct((m, IDX_DIM), BF16),
        jax.ShapeDtypeStruct((m, IDX_HEADS), F32),
        jax.ShapeDtypeStruct((m, 512), F32), hm(B_HEADS),
        jax.ShapeDtypeStruct((m, 512), F32), hm(B_HEADS),
    )
    hspec = lambda heads: pl.BlockSpec((heads, tm, HEAD_DIM), hrow)
    out_specs = (
        hspec(A_HEADS), hspec(IDX_HEADS), hspec(B_HEADS),
        pl.BlockSpec((tm, 128), row), hspec(A_KV_HEADS),
        pl.BlockSpec((tm, 128), row), hspec(A_KV_HEADS),
        pl.BlockSpec((tm, IDX_DIM), row), pl.BlockSpec((tm, IDX_DIM), row),
        pl.BlockSpec((tm, IDX_HEADS), row),
        pl.BlockSpec((tm, 512), row), hspec(B_HEADS),
        pl.BlockSpec((tm, 512), row), hspec(B_HEADS),
    )
    return pl.pallas_call(
        _in_ab_body,
        grid=(m // tm,),
        in_specs=[pl.BlockSpec((tm, d), row), pl.BlockSpec((1, d), const), pl.BlockSpec((d, _AB_WIDTH), const),
                  pl.BlockSpec((tm, LANES), pos), pl.BlockSpec((tm, LANES), pos)],
        out_specs=out_specs,
        out_shape=out_shape,
        compiler_params=_cparams("parallel"),
        name="in_proj_ab",
    )(x2d, g.reshape(1, d), w_perm, cos, sin)


_C_Q_W = C_HEADS * 2 * HEAD_DIM
_C_KV_W = C_KV_HEADS * 2 * HEAD_DIM


def _in_c_body(x_ref, g_ref, w_ref, cos_ref, sin_ref, cq_o, ck_o, ckh_o, cv_o, cvh_o):
    xn = _rms(x_ref[...], g_ref[...], NORM_EPS).astype(BF16)
    cos = cos_ref[...]
    sin = sin_ref[...]
    first = _first_half_mask(cos.shape)

    def proj(lo, width):
        return _dot(xn, w_ref[:, lo:lo + width])

    for half in range(2):
        q = proj(half * 512, 512)
        for c in range(4):
            r = _rope_chunk(q[:, c * LANES:(c + 1) * LANES], cos, sin, first) * QK_SCALE
            cq_o[half * 8 + 2 * c] = r[:, :HEAD_DIM].astype(BF16)
            cq_o[half * 8 + 2 * c + 1] = r[:, HEAD_DIM:].astype(BF16)
    k = proj(_C_Q_W, _C_KV_W)
    k = jnp.concatenate([_rope_chunk(k[:, c * LANES:(c + 1) * LANES], cos, sin, first) for c in range(4)], axis=1)
    ck_o[...] = k
    for h in range(2 * C_KV_HEADS):
        ckh_o[h] = k[:, h * HEAD_DIM:(h + 1) * HEAD_DIM].astype(BF16)
    v = proj(_C_Q_W + _C_KV_W, _C_KV_W)
    cv_o[...] = v
    for g_ in range(C_KV_HEADS):
        cvh_o[g_] = v[:, g_ * LANES:(g_ + 1) * LANES].astype(BF16)


def _in_proj_c(x2d, g, w_bf16, cos, sin):
    m, d = x2d.shape
    tm = min(ROW_TILE, m)
    n_pos_blocks = cos.shape[0] // tm
    row = lambda i: (i, 0)
    hrow = lambda i: (0, i, 0)
    pos = lambda i: (i % n_pos_blocks, 0)
    const = lambda i: (0, 0)
    width = _C_Q_W + 2 * _C_KV_W
    out_shape = (
        jax.ShapeDtypeStruct((2 * C_HEADS, m, HEAD_DIM), BF16),
        jax.ShapeDtypeStruct((m, _C_KV_W), F32), jax.ShapeDtypeStruct((2 * C_KV_HEADS, m, HEAD_DIM), BF16),
        jax.ShapeDtypeStruct((m, _C_KV_W), F32), jax.ShapeDtypeStruct((C_KV_HEADS, m, LANES), BF16),
    )
    out_specs = (
        pl.BlockSpec((2 * C_HEADS, tm, HEAD_DIM), hrow),
        pl.BlockSpec((tm, _C_KV_W), row), pl.BlockSpec((2 * C_KV_HEADS, tm, HEAD_DIM), hrow),
        pl.BlockSpec((tm, _C_KV_W), row), pl.BlockSpec((C_KV_HEADS, tm, LANES), hrow),
    )
    return pl.pallas_call(
        _in_c_body,
        grid=(m // tm,),
        in_specs=[pl.BlockSpec((tm, d), row), pl.BlockSpec((1, d), const), pl.BlockSpec((d, width), const),
                  pl.BlockSpec((tm, LANES), pos), pl.BlockSpec((tm, LANES), pos)],
        out_specs=out_specs,
        out_shape=out_shape,
        compiler_params=_cparams("parallel"),
        name="in_proj_c",
    )(x2d, g.reshape(1, d), w_bf16, cos, sin)


def _post_body(o_ref, x_ref, wout_ref, gpost_ref, gpre_ref, wup_ref, wdown_ref, gfpost_ref, y_ref):
    mix = _dot(o_ref[...], wout_ref[...])
    x1 = x_ref[...] + _rms(mix, gpost_ref[...], NORM_EPS)
    hn = _rms(x1, gpre_ref[...], NORM_EPS).astype(BF16)
    d_ff = wup_ref.shape[1]
    step = 1024
    f = jnp.zeros(x1.shape, F32)
    for c in range(d_ff // step):
        u = jnp.maximum(_dot(hn, wup_ref[:, c * step:(c + 1) * step]), 0.0)
        f = f + _dot((u * u).astype(BF16), wdown_ref[c * step:(c + 1) * step, :])
    y_ref[...] = x1 + _rms(f, gfpost_ref[...], NORM_EPS)


def _post_block(o2d, x2d, w_out, g_post, g_pre, w_up, w_down, g_fpost):
    m, d = x2d.shape
    tm = min(ROW_TILE, m)
    d_ff = w_up.shape[1]
    row = lambda i: (i, 0)
    const = lambda i: (0, 0)
    return pl.pallas_call(
        _post_body,
        grid=(m // tm,),
        in_specs=[pl.BlockSpec((tm, o2d.shape[1]), row), pl.BlockSpec((tm, d), row),
                  pl.BlockSpec(w_out.shape, const), pl.BlockSpec((1, d), const), pl.BlockSpec((1, d), const),
                  pl.BlockSpec((d, d_ff), const), pl.BlockSpec((d_ff, d), const), pl.BlockSpec((1, d), const)],
        out_specs=pl.BlockSpec((tm, d), row),
        out_shape=jax.ShapeDtypeStruct((m, d), F32),
        compiler_params=_cparams("parallel"),
        name="post_block",
    )(o2d, x2d, w_out, g_post.reshape(1, d), g_pre.reshape(1, d), w_up, w_down, g_fpost.reshape(1, d))


def _order_key(s):
    s = jnp.where(s == 0.0, 0.0, s)
    b = pltpu.bitcast(s, I32)
    return b ^ ((b >> 31) & I32(0x7FFFFFFF))


def _prefix_matrices():
    r = lax.broadcasted_iota(I32, (LANES, LANES), 0)
    c = lax.broadcasted_iota(I32, (LANES, LANES), 1)
    upper = jnp.where(r < c, 1.0, 0.0).astype(BF16)
    ones = jnp.ones((LANES, LANES), BF16)
    return upper, ones


def _select_bias(sc_ref, key_ref, nch, n_sel, rows):
    def fill_keys(j, c):
        key_ref[j] = _order_key(sc_ref[j])
        return c

    lax.fori_loop(0, nch, fill_keys, 0)

    def count(pred):
        def body(j, c):
            return c + jnp.where(pred(key_ref[j]), 1.0, 0.0)

        c = lax.fori_loop(0, nch, body, jnp.zeros((rows, LANES), F32))
        return jnp.sum(c, axis=1, keepdims=True)

    k_sel = float(n_sel)
    int_min = I32(-2 ** 31)
    thr = jnp.where(count(lambda k: k >= 0) >= k_sel, I32(0), int_min)

    def bit_body(i, t):
        cand = t + lax.shift_left(I32(1), I32(30) - i)
        cand_b = jnp.broadcast_to(cand, (rows, LANES))
        return jnp.where(count(lambda k: k >= cand_b) >= k_sel, cand, t)

    thr = lax.fori_loop(0, 31, bit_body, thr)
    thr_b = jnp.broadcast_to(thr, (rows, LANES))
    need_b = jnp.broadcast_to(k_sel - count(lambda k: k > thr_b), (rows, LANES))
    upper, ones = _prefix_matrices()

    def tie_body(j, run):
        k = key_ref[j]
        eq = jnp.where(k == thr_b, 1.0, 0.0)
        eq16 = eq.astype(BF16)
        rank = run + _dot(eq16, upper)
        bias = jnp.where(k > thr_b, 0.0, jnp.where(eq * jnp.where(rank < need_b, 1.0, 0.0) > 0.5, 0.0, MASKED))
        sc_ref[j] = jnp.where(sc_ref[j] == -jnp.inf, MASKED, bias)
        return run + _dot(eq16, ones)

    lax.fori_loop(0, nch, tie_body, jnp.zeros((rows, LANES), F32))


def _weighted_values(p16, v16, v_transposed):
    return _dot_nt(p16, v16) if v_transposed else _dot(p16, v16)


def _softmax_step(s, v16, m, l, acc, v_transposed=False):
    m_new = jnp.maximum(m, jnp.max(s, axis=1, keepdims=True))
    alpha = jnp.exp(m - m_new)
    p = jnp.exp(s - m_new)
    l = alpha * l + jnp.sum(p, axis=1, keepdims=True)
    acc = alpha * acc + _weighted_values(p.astype(BF16), v16, v_transposed)
    return m_new, l, acc


def _stick_matrices():
    r = lax.broadcasted_iota(I32, (2 * LANES, 2 * LANES), 0) & (LANES - 1)
    c = lax.broadcasted_iota(I32, (2 * LANES, 2 * LANES), 1)
    return jnp.where((c >= LANES) | (r > c), 1.0, 0.0).astype(BF16)


def _stick_step(z, valid, v16, run, acc, mats, v_transposed=False):
    softplus = jnp.maximum(z, 0.0) + jnp.log(1.0 + jnp.exp(-jnp.abs(z)))
    log_stay = jnp.where(valid, -softplus, 0.0)
    hi = log_stay.astype(BF16)
    lo = (log_stay - hi.astype(F32)).astype(BF16)
    sums = _dot(jnp.concatenate([hi, lo], axis=1), mats)
    log_w = (z - softplus) + sums[:, :LANES] + run
    w = jnp.where(valid, jnp.exp(log_w), 0.0)
    acc = acc + _weighted_values(w.astype(BF16), v16, v_transposed)
    return run + sums[:, LANES:], acc


def _att0_prompt_body(n_sel, aq_ref, iq_ref, iw_ref, bq_ref, akh_ref, avh_ref, ikh_ref, bkh_ref, bvh_ref,
                      o_ref, sc_ref, key_ref, wb_ref):
    qi = pl.program_id(1)
    nch = qi + 1
    row = lax.broadcasted_iota(I32, (QB, LANES), 0)
    col = lax.broadcasted_iota(I32, (QB, LANES), 1)
    qpos = qi * QB + row

    def rows_of(j):
        return pl.ds(pl.multiple_of(j * LANES, LANES), LANES)

    for h in range(IDX_HEADS):
        wb_ref[h] = jnp.broadcast_to(iw_ref[:, h:h + 1], (QB, LANES))

    def score_chunk(j, c):
        k16 = ikh_ref[rows_of(j), :]
        acc = jnp.zeros((QB, LANES), F32)
        for h in range(IDX_HEADS):
            acc = acc + wb_ref[h] * jnp.maximum(_dot_nt(iq_ref[h], k16), 0.0)
        sc_ref[j] = jnp.where(j * LANES + col <= qpos, acc, -jnp.inf)
        return c

    lax.fori_loop(0, nch, score_chunk, 0)
    _select_bias(sc_ref, key_ref, nch, n_sel, QB)

    rep = A_HEADS // A_KV_HEADS
    for h in range(A_HEADS):
        g = h // rep
        q16 = aq_ref[h]

        def dsa_chunk(j, carry, q16=q16, g=g):
            s = _dot_nt(q16, akh_ref[g, rows_of(j), :]) + sc_ref[j]
            return _softmax_step(s, avh_ref[g, rows_of(j), :], *carry)

        m, l, acc = lax.fori_loop(0, nch, dsa_chunk, (jnp.full((QB, 1), MASKED, F32), jnp.zeros((QB, 1), F32),
                                                      jnp.zeros((QB, HEAD_DIM), F32)))
        o_ref[:, h * HEAD_DIM:(h + 1) * HEAD_DIM] = (acc / l).astype(o_ref.dtype)

    mats = _stick_matrices()
    base = A_HEADS * HEAD_DIM
    for h in range(B_HEADS):
        q16 = bq_ref[h]

        def sb_chunk(i, carry, q16=q16, h=h):
            j = qi - i
            z = _dot_nt(q16, bkh_ref[h, rows_of(j), :])
            valid = j * LANES + col < qpos
            return _stick_step(z, valid, bvh_ref[h, rows_of(j), :], *carry, mats)

        _, acc = lax.fori_loop(0, nch, sb_chunk, (jnp.zeros((QB, LANES), F32), jnp.zeros((QB, HEAD_DIM), F32)))
        o_ref[:, base + h * HEAD_DIM:base + (h + 1) * HEAD_DIM] = acc.astype(o_ref.dtype)


def _att0_prompt(aq, iq, iw, bq, akh, avh, ikh, bkh, bvh, bsz, t):
    nq = t // QB
    n_sel = min(TOPK_MAX, t // 4)
    qh = lambda b, q: (0, b * nq + q, 0)
    qrow = lambda b, q: (b * nq + q, 0)
    kh = lambda b, q: (0, b, 0)
    krow = lambda b, q: (b, 0)
    width = (A_HEADS + B_HEADS) * HEAD_DIM
    return pl.pallas_call(
        functools.partial(_att0_prompt_body, n_sel),
        grid=(bsz, nq),
        in_specs=[pl.BlockSpec((A_HEADS, QB, HEAD_DIM), qh), pl.BlockSpec((IDX_HEADS, QB, IDX_DIM), qh),
                  pl.BlockSpec((QB, IDX_HEADS), qrow), pl.BlockSpec((B_HEADS, QB, HEAD_DIM), qh),
                  pl.BlockSpec((A_KV_HEADS, t, HEAD_DIM), kh), pl.BlockSpec((A_KV_HEADS, t, HEAD_DIM), kh),
                  pl.BlockSpec((t, IDX_DIM), krow),
                  pl.BlockSpec((B_HEADS, t, HEAD_DIM), kh), pl.BlockSpec((B_HEADS, t, HEAD_DIM), kh)],
        out_specs=pl.BlockSpec((QB, width), qrow),
        out_shape=jax.ShapeDtypeStruct((bsz * t, width), BF16),
        scratch_shapes=[pltpu.VMEM((nq, QB, LANES), F32), pltpu.VMEM((nq, QB, LANES), I32),
                        pltpu.VMEM((IDX_HEADS, QB, LANES), F32)],
        compiler_params=_cparams("parallel", "arbitrary"),
        name="att0_prompt",
    )(aq, iq, iw, bq, akh, avh, ikh, bkh, bvh)


def _diff_lambda(lq1_ref, lk1_ref, lq2_ref, lk2_ref, lambda_init):
    s1 = jnp.sum(lq1_ref[...] * lk1_ref[...], axis=1, keepdims=True)
    s2 = jnp.sum(lq2_ref[...] * lk2_ref[...], axis=1, keepdims=True)
    return jnp.exp(s1) - jnp.exp(s2) + lambda_init


def _diff_finish(a1, l1, a2, l2, lam, gsub, lambda_init):
    o = a1 / l1 - lam * (a2 / l2)
    return _rms(o, gsub, SUBLN_EPS) * (1.0 - lambda_init)


def _att1_prompt_body(lambda_init, cq_ref, ckh_ref, cvh_ref, lq1_ref, lk1_ref, lq2_ref, lk2_ref, gsub_ref, o_ref):
    qi = pl.program_id(1)
    nch = qi + 1
    row = lax.broadcasted_iota(I32, (QB, LANES), 0)
    col = lax.broadcasted_iota(I32, (QB, LANES), 1)
    qpos = qi * QB + row
    lam = _diff_lambda(lq1_ref, lk1_ref, lq2_ref, lk2_ref, lambda_init)
    gsub = gsub_ref[...]
    rep = C_HEADS // C_KV_HEADS

    def rows_of(j):
        return pl.ds(pl.multiple_of(j * LANES, LANES), LANES)

    for h in range(C_HEADS):
        g = h // rep
        q1 = cq_ref[2 * h]
        q2 = cq_ref[2 * h + 1]

        def chunk(j, carry, q1=q1, q2=q2, g=g):
            bias = jnp.where(j * LANES + col <= qpos, 0.0, MASKED)
            v16 = cvh_ref[g, rows_of(j), :]
            s1 = _dot_nt(q1, ckh_ref[2 * g, rows_of(j), :]) + bias
            s2 = _dot_nt(q2, ckh_ref[2 * g + 1, rows_of(j), :]) + bias
            return _softmax_step(s1, v16, *carry[:3]) + _softmax_step(s2, v16, *carry[3:])

        init = (jnp.full((QB, 1), MASKED, F32), jnp.zeros((QB, 1), F32), jnp.zeros((QB, LANES), F32))
        m1, l1, a1, m2, l2, a2 = lax.fori_loop(0, nch, chunk, init + init)
        o_ref[:, h * LANES:(h + 1) * LANES] = _diff_finish(a1, l1, a2, l2, lam, gsub, lambda_init).astype(o_ref.dtype)


def _att1_prompt(cq, ckh, cvh, lq1, lk1, lq2, lk2, gsub, lambda_init, bsz, t):
    nq = t // QB
    qh = lambda b, q: (0, b * nq + q, 0)
    qrow = lambda b, q: (b * nq + q, 0)
    kh = lambda b, q: (0, b, 0)
    const = lambda b, q: (0, 0)
    vec = lambda a: a.reshape(1, -1)
    return pl.pallas_call(
        functools.partial(_att1_prompt_body, lambda_init),
        grid=(bsz, nq),
        in_specs=[pl.BlockSpec((2 * C_HEADS, QB, HEAD_DIM), qh),
                  pl.BlockSpec((2 * C_KV_HEADS, t, HEAD_DIM), kh), pl.BlockSpec((C_KV_HEADS, t, LANES), kh),
                  pl.BlockSpec((1, HEAD_DIM), const), pl.BlockSpec((1, HEAD_DIM), const),
                  pl.BlockSpec((1, HEAD_DIM), const), pl.BlockSpec((1, HEAD_DIM), const),
                  pl.BlockSpec((1, LANES), const)],
        out_specs=pl.BlockSpec((QB, _C_Q_W), qrow),
        out_shape=jax.ShapeDtypeStruct((bsz * t, _C_Q_W), BF16),
        compiler_params=_cparams("parallel", "arbitrary"),
        name="att1_prompt",
    )(cq, ckh, cvh, vec(lq1), vec(lk1), vec(lq2), vec(lk2), vec(gsub))


def _page_index(n_pages, newest_first, trailing):
    zeros = (0,) * trailing
    if newest_first:
        return lambda b, s, pt: (0, pt[b, jnp.minimum(n_pages - s, n_pages - 1)]) + zeros
    return lambda b, s, pt: (0, pt[b, jnp.minimum(s, n_pages - 1)]) + zeros


def _keys_on_lanes(cache):
    nd = cache.ndim
    return jnp.transpose(cache, (0, 1) + tuple(range(3, nd)) + (2,))


def _att0_sample_walk_body(dec, pt_ref, iq_ref, iw_ref, bq_ref, ikn_ref, bkn_ref, bvn_ref,
                           ikc_ref, bkc_ref, bvc_ref, sc_o, ob_o, run_ref, acc_ref):
    del pt_ref
    s = pl.program_id(1)
    rows = B_HEADS * dec
    bw = B_HEADS * HEAD_DIM
    row = lax.broadcasted_iota(I32, (rows, LANES), 0)
    col = lax.broadcasted_iota(I32, (rows, LANES), 1)
    t_of_row = row & (dec - 1)
    mats = _stick_matrices()

    def scores(kt16):
        lg = jnp.maximum(_dot(iq_ref[...], kt16), 0.0) * iw_ref[...]
        out = lg[0:dec]
        for h in range(1, IDX_HEADS):
            out = out + lg[h * dec:(h + 1) * dec]
        return out

    def stick(kt16, vt16, valid):
        z = _dot(bq_ref[...], kt16)
        run, acc = _stick_step(z, valid, vt16, run_ref[...], acc_ref[...], mats, v_transposed=True)
        run_ref[...] = run
        acc_ref[...] = acc

    @pl.when(s == 0)
    def _():
        run_ref[...] = jnp.zeros_like(run_ref)
        acc_ref[...] = jnp.zeros_like(acc_ref)
        sc = scores(ikn_ref[...])
        causal = (col[:dec] < dec) & (col[:dec] <= row[:dec])
        sc_o[...] = jnp.where(causal, sc, -jnp.inf)
        stick(bkn_ref[...], bvn_ref[...], (col < dec) & (col < t_of_row))

    @pl.when(s > 0)
    def _():
        sc_o[...] = scores(ikc_ref[...].astype(BF16))
        stick(bkc_ref[...].reshape(bw, LANES).astype(BF16), bvc_ref[...].reshape(bw, LANES).astype(BF16), col >= 0)

    @pl.when(s == pl.num_programs(1) - 1)
    def _():
        acc = acc_ref[...]
        for h in range(B_HEADS):
            ob_o[:, h * HEAD_DIM:(h + 1) * HEAD_DIM] = acc[h * dec:(h + 1) * dec,
                                                           h * HEAD_DIM:(h + 1) * HEAD_DIM].astype(ob_o.dtype)


def _att0_sample_walk(page_table, iq_st, iw_st, bq_bd, ik_new, bk_new, bv_new, cache_kidx, cache_bk, cache_bv):
    db, n_pages = page_table.shape
    dec = iq_st.shape[1] // IDX_HEADS
    rows = B_HEADS * dec
    bw = B_HEADS * HEAD_DIM
    per_b = lambda b, s, pt: (b, 0, 0)
    chunk_of_step = lambda b, s, pt: (jnp.where(s == 0, n_pages, n_pages - s), b, 0, 0)
    grid_spec = pltpu.PrefetchScalarGridSpec(
        num_scalar_prefetch=1,
        grid=(db, n_pages + 1),
        in_specs=[pl.BlockSpec((None, rows, IDX_DIM), per_b), pl.BlockSpec((None, rows, 1), per_b),
                  pl.BlockSpec((None, rows, bw), per_b),
                  pl.BlockSpec((None, IDX_DIM, LANES), per_b), pl.BlockSpec((None, bw, LANES), per_b),
                  pl.BlockSpec((None, bw, LANES), per_b),
                  pl.BlockSpec((None, None, IDX_DIM, LANES), _page_index(n_pages, True, 2)),
                  pl.BlockSpec((None, None, B_HEADS, HEAD_DIM, LANES), _page_index(n_pages, True, 3)),
                  pl.BlockSpec((None, None, B_HEADS, HEAD_DIM, LANES), _page_index(n_pages, True, 3))],
        out_specs=[pl.BlockSpec((None, None, dec, LANES), chunk_of_step), pl.BlockSpec((None, dec, bw), per_b)],
        scratch_shapes=[pltpu.VMEM((rows, LANES), F32), pltpu.VMEM((rows, bw), F32)],
    )
    return pl.pallas_call(
        functools.partial(_att0_sample_walk_body, dec),
        grid_spec=grid_spec,
        out_shape=(jax.ShapeDtypeStruct((n_pages + 1, db, dec, LANES), F32), jax.ShapeDtypeStruct((db, dec, bw), BF16)),
        compiler_params=_cparams("parallel", "arbitrary"),
        name="att0_sample_walk",
    )(page_table, iq_st, iw_st, bq_bd, ik_new, bk_new, bv_new, cache_kidx, cache_bk, cache_bv)


def _select_sample_body(n_sel, sc_ref, bias_o, key_ref):
    nch, rows, _ = sc_ref.shape
    bias_o[...] = sc_ref[...]
    _select_bias(bias_o, key_ref, nch, n_sel, rows)


def _select_sample(scores, n_sel):
    nch, m, _ = scores.shape
    rows = min(m, 64)
    spec = pl.BlockSpec((nch, rows, LANES), lambda i: (0, i, 0))
    return pl.pallas_call(
        functools.partial(_select_sample_body, n_sel),
        grid=(m // rows,),
        in_specs=[spec],
        out_specs=spec,
        out_shape=jax.ShapeDtypeStruct(scores.shape, F32),
        scratch_shapes=[pltpu.VMEM((nch, rows, LANES), I32)],
        compiler_params=_cparams("parallel"),
        name="select_sample",
    )(scores)


def _att0_sample_dsa_body(n_pages, dec, pt_ref, aq_ref, bias_ref, akn_ref, avn_ref, akc_ref, avc_ref,
                          oa_o, m_ref, l_ref, acc_ref):
    del pt_ref
    s = pl.program_id(1)
    kw = A_KV_HEADS * HEAD_DIM

    @pl.when(s == 0)
    def _():
        m_ref[...] = jnp.full_like(m_ref, MASKED)
        l_ref[...] = jnp.zeros_like(l_ref)
        acc_ref[...] = jnp.zeros_like(acc_ref)

    def step(kt16, vt16):
        bias = jnp.concatenate([bias_ref[...]] * A_HEADS, axis=0)
        sc = _dot(aq_ref[...], kt16) + bias
        m, l, acc = _softmax_step(sc, vt16, m_ref[...], l_ref[...], acc_ref[...], v_transposed=True)
        m_ref[...] = m
        l_ref[...] = l
        acc_ref[...] = acc

    @pl.when(s < n_pages)
    def _():
        step(akc_ref[...].reshape(kw, LANES).astype(BF16), avc_ref[...].reshape(kw, LANES).astype(BF16))

    @pl.when(s == n_pages)
    def _():
        step(akn_ref[...], avn_ref[...])
        o = acc_ref[...] / l_ref[...]
        rep = A_HEADS // A_KV_HEADS
        for h in range(A_HEADS):
            g = h // rep
            oa_o[:, h * HEAD_DIM:(h + 1) * HEAD_DIM] = o[h * dec:(h + 1) * dec,
                                                         g * HEAD_DIM:(g + 1) * HEAD_DIM].astype(oa_o.dtype)


def _att0_sample_dsa(page_table, aq_bd, bias, ak_new, av_new, cache_ak, cache_av):
    db, n_pages = page_table.shape
    rows = aq_bd.shape[1]
    dec = rows // A_HEADS
    kw = A_KV_HEADS * HEAD_DIM
    per_b = lambda b, s, pt: (b, 0, 0)
    page = _page_index(n_pages, False, 3)
    grid_spec = pltpu.PrefetchScalarGridSpec(
        num_scalar_prefetch=1,
        grid=(db, n_pages + 1),
        in_specs=[pl.BlockSpec((None, rows, kw), per_b),
                  pl.BlockSpec((None, None, dec, LANES), lambda b, s, pt: (s, b, 0, 0)),
                  pl.BlockSpec((None, kw, LANES), per_b), pl.BlockSpec((None, kw, LANES), per_b),
                  pl.BlockSpec((None, None, A_KV_HEADS, HEAD_DIM, LANES), page),
                  pl.BlockSpec((None, None, A_KV_HEADS, HEAD_DIM, LANES), page)],
        out_specs=pl.BlockSpec((None, dec, A_HEADS * HEAD_DIM), per_b),
        scratch_shapes=[pltpu.VMEM((rows, 1), F32), pltpu.VMEM((rows, 1), F32), pltpu.VMEM((rows, kw), F32)],
    )
    return pl.pallas_call(
        functools.partial(_att0_sample_dsa_body, n_pages, dec),
        grid_spec=grid_spec,
        out_shape=jax.ShapeDtypeStruct((db, dec, A_HEADS * HEAD_DIM), BF16),
        compiler_params=_cparams("parallel", "arbitrary"),
        name="att0_sample_dsa",
    )(page_table, aq_bd, bias, ak_new, av_new, cache_ak, cache_av)


def _att1_sample_body(n_pages, dec, lambda_init, pt_ref, cq_ref, ckn_ref, cvn_ref, ckc_ref, cvc_ref,
                      lq1_ref, lk1_ref, lq2_ref, lk2_ref, gsub_ref, o_o, m_ref, l_ref, acc_ref):
    del pt_ref
    s = pl.program_id(1)
    rows_g = 2 * (C_HEADS // C_KV_HEADS) * dec

    @pl.when(s == 0)
    def _():
        m_ref[...] = jnp.full_like(m_ref, MASKED)
        l_ref[...] = jnp.zeros_like(l_ref)
        acc_ref[...] = jnp.zeros_like(acc_ref)

    def step(g, k16, v16, bias):
        r = slice(g * rows_g, (g + 1) * rows_g)
        sc = _dot_nt(cq_ref[r, :], k16) + bias
        m, l, acc = _softmax_step(sc, v16, m_ref[r, :], l_ref[r, :], acc_ref[r, :])
        m_ref[r, :] = m
        l_ref[r, :] = l
        acc_ref[r, :] = acc

    @pl.when(s < n_pages)
    def _():
        for g in range(C_KV_HEADS):
            head_rows = pl.ds(g, LANES, stride=C_KV_HEADS)
            step(g, ckc_ref[head_rows, :].astype(BF16), cvc_ref[head_rows, :].astype(BF16), 0.0)

    @pl.when(s == n_pages)
    def _():
        row = lax.broadcasted_iota(I32, (rows_g, LANES), 0)
        col = lax.broadcasted_iota(I32, (rows_g, LANES), 1)
        bias = jnp.where(col <= (row & (dec - 1)), 0.0, MASKED)
        for g in range(C_KV_HEADS):
            cols = slice(g * LANES, (g + 1) * LANES)
            step(g, ckn_ref[:, cols], cvn_ref[:, cols], bias)
        lam = _diff_lambda(lq1_ref, lk1_ref, lq2_ref, lk2_ref, lambda_init)
        acc = acc_ref[...]
        l = l_ref[...]
        for h in range(C_HEADS):
            r1 = (2 * h) * dec
            r2 = (2 * h + 1) * dec
            o = _diff_finish(acc[r1:r1 + dec], l[r1:r1 + dec], acc[r2:r2 + dec], l[r2:r2 + dec],
                             lam, gsub_ref[...], lambda_init)
            o_o[:, h * LANES:(h + 1) * LANES] = o.astype(o_o.dtype)


def _att1_sample(page_table, cq_bd, ck_new, cv_new, cache_ck, cache_cv, lq1, lk1, lq2, lk2, gsub, lambda_init):
    db, n_pages = page_table.shape
    rows = cq_bd.shape[1]
    dec = rows // (2 * C_HEADS)
    per_b = lambda b, s, pt: (b, 0, 0)
    const = lambda b, s, pt: (0, 0)
    page = _page_index(n_pages, False, 2)
    vec = lambda a: a.reshape(1, -1)
    page_rows = LANES * C_KV_HEADS
    grid_spec = pltpu.PrefetchScalarGridSpec(
        num_scalar_prefetch=1,
        grid=(db, n_pages + 1),
        in_specs=[pl.BlockSpec((None, rows, LANES), per_b),
                  pl.BlockSpec((None, LANES, _C_KV_W), per_b), pl.BlockSpec((None, LANES, _C_KV_W), per_b),
                  pl.BlockSpec((None, None, page_rows, LANES), page), pl.BlockSpec((None, None, page_rows, LANES), page),
                  pl.BlockSpec((1, HEAD_DIM), const), pl.BlockSpec((1, HEAD_DIM), const),
                  pl.BlockSpec((1, HEAD_DIM), const), pl.BlockSpec((1, HEAD_DIM), const),
                  pl.BlockSpec((1, LANES), const)],
        out_specs=pl.BlockSpec((None, dec, _C_Q_W), per_b),
        scratch_shapes=[pltpu.VMEM((rows, 1), F32), pltpu.VMEM((rows, 1), F32), pltpu.VMEM((rows, LANES), F32)],
    )
    return pl.pallas_call(
        functools.partial(_att1_sample_body, n_pages, dec, lambda_init),
        grid_spec=grid_spec,
        out_shape=jax.ShapeDtypeStruct((db, dec, _C_Q_W), BF16),
        compiler_params=_cparams("parallel", "arbitrary"),
        name="att1_sample",
    )(page_table, cq_bd, ck_new, cv_new, cache_ck, cache_cv, vec(lq1), vec(lk1), vec(lq2), vec(lk2), vec(gsub))


def _rope_tables(pos):
    half = HEAD_DIM // 2
    inv_freq = ROPE_THETA ** (-jnp.arange(half, dtype=F32) / half)
    ang = pos.astype(F32)[:, None] * inv_freq[None, :]
    cos, sin = jnp.cos(ang), jnp.sin(ang)
    return jnp.concatenate([cos] * 4, axis=1), jnp.concatenate([-sin, sin, -sin, sin], axis=1)


def _pad_rows(a, rows):
    return jnp.pad(a, ((0, 0), (0, rows - a.shape[1]), (0, 0)))


def _block_diag_rows(q_hm, db, dec, slot_of_head, n_slots):
    heads, _, w = q_hm.shape
    q = q_hm.reshape(heads, db, dec, w).transpose(1, 0, 2, 3)
    onehot = (jnp.asarray(slot_of_head)[:, None] == jnp.arange(n_slots)[None, :]).astype(q.dtype)
    return (q[:, :, :, None, :] * onehot[None, :, None, :, None]).reshape(db, heads * dec, n_slots * w)


def kernel(x_prompt, x_sample, cache_a_k, cache_a_v, cache_a_kidx, cache_b_k, cache_b_v, cache_c_k, cache_c_v,
           page_table, norm_mix_pre, norm_mix_post, norm_ffn_pre, norm_ffn_post, w_in_ab, w_out_ab, w_in_c, w_out_c,
           c_lambda_q1, c_lambda_k1, c_lambda_q2, c_lambda_k2, c_subln, w_up, w_down):
    bsz, t, d = x_prompt.shape
    db, dec, _ = x_sample.shape
    n_phys, page = cache_a_k.shape[1], cache_a_k.shape[2]
    n_pages = page_table.shape[1]
    past = n_pages * page
    assert page == LANES and t % ROW_TILE == 0 and dec & (dec - 1) == 0 and dec <= 8
    mp, ms = bsz * t, db * dec

    cos_p, sin_p = _rope_tables(jnp.arange(t, dtype=jnp.int32))
    cos_s, sin_s = _rope_tables(jnp.arange(past, past + dec, dtype=jnp.int32))
    cos_s, sin_s = jnp.tile(cos_s, (db, 1)), jnp.tile(sin_s, (db, 1))

    xp = x_prompt.reshape(mp, d)
    xs = x_sample.reshape(ms, d)
    w_ab = _prep_w_ab(w_in_ab[0])
    w_out_ab16 = w_out_ab[0].astype(BF16)
    w_c16 = w_in_c[0].astype(BF16)
    w_out_c16 = w_out_c[0].astype(BF16)
    w_up16 = w_up.astype(BF16)
    w_down16 = w_down.astype(BF16)
    lambda_init = 0.8 - 0.6 * math.exp(-0.3 * 1)

    (aq_p, iq_p, bq_p, ak_p, akh_p, av_p, avh_p, ik_p, ikh_p, iw_p, bk_p, bkh_p, bv_p, bvh_p) = _in_proj_ab(
        xp, norm_mix_pre[0], w_ab, cos_p, sin_p)
    o_p = _att0_prompt(aq_p, iq_p, iw_p, bq_p, akh_p, avh_p, ikh_p, bkh_p, bvh_p, bsz, t)
    xp = _post_block(o_p, xp, w_out_ab16, norm_mix_post[0], norm_ffn_pre[0], w_up16[0], w_down16[0], norm_ffn_post[0])

    (aq_s, iq_s, bq_s, ak_s, _, av_s, _, ik_s, _, iw_s, bk_s, _, bv_s, _) = _in_proj_ab(
        xs, norm_mix_pre[0], w_ab, cos_s, sin_s)
    iq_st = iq_s.reshape(IDX_HEADS, db, dec, IDX_DIM).transpose(1, 0, 2, 3).reshape(db, IDX_HEADS * dec, IDX_DIM)
    iw_st = iw_s.reshape(db, dec, IDX_HEADS).transpose(0, 2, 1).reshape(db, IDX_HEADS * dec, 1)
    bq_bd = _block_diag_rows(bq_s, db, dec, list(range(B_HEADS)), B_HEADS)
    aq_bd = _block_diag_rows(aq_s, db, dec, [h // (A_HEADS // A_KV_HEADS) for h in range(A_HEADS)], A_KV_HEADS)
    new16 = lambda a: _pad_rows(a.reshape(db, dec, -1).astype(BF16), LANES)
    new16t = lambda a: jnp.pad(a.reshape(db, dec, -1).astype(BF16).transpose(0, 2, 1), ((0, 0), (0, 0), (0, LANES - dec)))
    scores, ob_s = _att0_sample_walk(
        page_table, iq_st, iw_st, bq_bd, new16t(ik_s), new16t(bk_s), new16t(bv_s),
        _keys_on_lanes(cache_a_kidx), _keys_on_lanes(cache_b_k), _keys_on_lanes(cache_b_v))
    n_sel_s = min(TOPK_MAX, (past + dec) // 4)
    bias = _select_sample(scores.reshape(n_pages + 1, ms, LANES), n_sel_s).reshape(n_pages + 1, db, dec, LANES)
    oa_s = _att0_sample_dsa(page_table, aq_bd, bias, new16t(ak_s), new16t(av_s),
                            _keys_on_lanes(cache_a_k), _keys_on_lanes(cache_a_v))
    o_s = jnp.concatenate([oa_s, ob_s], axis=-1).reshape(ms, -1)
    xs = _post_block(o_s, xs, w_out_ab16, norm_mix_post[0], norm_ffn_pre[0], w_up16[0], w_down16[0], norm_ffn_post[0])

    cq_p, ck_p, ckh_p, cv_p, cvh_p = _in_proj_c(xp, norm_mix_pre[1], w_c16, cos_p, sin_p)
    o_p = _att1_prompt(cq_p, ckh_p, cvh_p, c_lambda_q1[0], c_lambda_k1[0], c_lambda_q2[0], c_lambda_k2[0], c_subln[0],
                       lambda_init, bsz, t)
    yp = _post_block(o_p, xp, w_out_c16, norm_mix_post[1], norm_ffn_pre[1], w_up16[1], w_down16[1], norm_ffn_post[1])

    cq_s, ck_s, _, cv_s, _ = _in_proj_c(xs, norm_mix_pre[1], w_c16, cos_s, sin_s)
    rep = C_HEADS // C_KV_HEADS
    cq_bd = _block_diag_rows(cq_s, db, dec, [hc % 2 for hc in range(2 * C_HEADS)], 2)
    o_s = _att1_sample(page_table, cq_bd, new16(ck_s), new16(cv_s),
                       cache_c_k.reshape(-1, n_phys, page * C_KV_HEADS, LANES),
                       cache_c_v.reshape(-1, n_phys, page * C_KV_HEADS, LANES),
                       c_lambda_q1[0], c_lambda_k1[0], c_lambda_q2[0], c_lambda_k2[0], c_subln[0], lambda_init)
    ys = _post_block(o_s.reshape(ms, -1), xs, w_out_c16, norm_mix_post[1], norm_ffn_pre[1], w_up16[1], w_down16[1],
                     norm_ffn_post[1])

    def rows(a, b_, t_, heads, width):
        shape = (1, b_, t_, heads, width) if heads else (1, b_, t_, width)
        return a.reshape(shape)

    return (yp.reshape(bsz, t, d), ys.reshape(db, dec, d),
            rows(ak_p, bsz, t, A_KV_HEADS, HEAD_DIM), rows(av_p, bsz, t, A_KV_HEADS, HEAD_DIM),
            rows(ik_p, bsz, t, 0, IDX_DIM),
            rows(bk_p, bsz, t, B_HEADS, HEAD_DIM), rows(bv_p, bsz, t, B_HEADS, HEAD_DIM),
            rows(ck_p, bsz, t, C_KV_HEADS, 2 * HEAD_DIM), rows(cv_p, bsz, t, C_KV_HEADS, 2 * HEAD_DIM),
            rows(ak_s, db, dec, A_KV_HEADS, HEAD_DIM), rows(av_s, db, dec, A_KV_HEADS, HEAD_DIM),
            rows(ik_s, db, dec, 0, IDX_DIM),
            rows(bk_s, db, dec, B_HEADS, HEAD_DIM), rows(bv_s, db, dec, B_HEADS, HEAD_DIM),
            rows(ck_s, db, dec, C_KV_HEADS, 2 * HEAD_DIM), rows(cv_s, db, dec, C_KV_HEADS, 2 * HEAD_DIM))
```

```python
import functools
import math

import jax
import jax.numpy as jnp
from jax import lax
from jax.experimental import pallas as pl
from jax.experimental.pallas import tpu as pltpu

F32 = jnp.float32
BF16 = jnp.bfloat16
I32 = jnp.int32

HEAD_DIM = 64
A_HEADS = 8
A_KV_HEADS = 2
IDX_HEADS = 8
IDX_DIM = 64
TOPK_MAX = 256
B_HEADS = 8
C_HEADS = 8
C_KV_HEADS = 4
ROPE_THETA = 10000.0
NORM_EPS = 1e-6
SUBLN_EPS = 1e-5

LANES = 128
QB = 128
QB_DIFF = 256
KC = 2 * LANES
PAGE_GROUP = 8
ROW_TILE = 256
MASKED = -1e30
QK_SCALE = HEAD_DIM ** -0.5
VMEM_LIMIT = 56 * 1024 * 1024


def _cparams(*sem):
    return pltpu.CompilerParams(dimension_semantics=sem, vmem_limit_bytes=VMEM_LIMIT)


def _rms(x, g, eps):
    return x * lax.rsqrt(jnp.mean(x * x, axis=-1, keepdims=True) + eps) * g


def _dot(a, b):
    return jnp.dot(a, b, preferred_element_type=F32)


def _dot_nt(a, b):
    return lax.dot_general(a, b, (((1,), (1,)), ((), ())), preferred_element_type=F32)


def _rope_chunk(seg, cos, sin_signed, first_half):
    partner = jnp.where(first_half, pltpu.roll(seg, LANES - HEAD_DIM // 2, 1), pltpu.roll(seg, HEAD_DIM // 2, 1))
    return seg * cos + partner * sin_signed


def _rope_wide(y, cos, sin_signed, first_half):
    return jnp.concatenate([_rope_chunk(y[:, c * LANES:(c + 1) * LANES], cos, sin_signed, first_half)
                            for c in range(y.shape[1] // LANES)], axis=1)


def _first_half_mask(shape):
    lane = lax.broadcasted_iota(I32, shape, 1)
    return (lane & (HEAD_DIM - 1)) < HEAD_DIM // 2


def _split_heads(y, out_ref, scale=None):
    for h in range(y.shape[1] // HEAD_DIM):
        v = y[:, h * HEAD_DIM:(h + 1) * HEAD_DIM]
        out_ref[h] = (v if scale is None else v * scale).astype(out_ref.dtype)


def _split_heads_t(y, out_ref):
    yt = y.T
    for h in range(yt.shape[0] // HEAD_DIM):
        out_ref[h] = yt[h * HEAD_DIM:(h + 1) * HEAD_DIM, :].astype(out_ref.dtype)


_AB_AQ, _AB_IQ, _AB_AK, _AB_AV, _AB_BQ, _AB_BK, _AB_BV, _AB_TAIL, _AB_WIDTH = 0, 512, 1024, 1152, 1280, 1792, 2304, 2816, 2944
_A_KV_W = A_KV_HEADS * HEAD_DIM
_B_W = B_HEADS * HEAD_DIM


def _prep_w_ab(w):
    d = w.shape[0]
    sizes = (A_HEADS * HEAD_DIM, _A_KV_W, _A_KV_W, IDX_HEADS * IDX_DIM, IDX_DIM, IDX_HEADS, _B_W, _B_W, _B_W)
    offs = [0]
    for s in sizes:
        offs.append(offs[-1] + s)
    a_q, a_k, a_v, i_q, i_k, i_w, b_q, b_k, b_v = (w[:, offs[i]:offs[i + 1]] for i in range(9))
    pad = jnp.zeros((d, _AB_WIDTH - _AB_TAIL - IDX_DIM - IDX_HEADS), w.dtype)
    return jnp.concatenate([a_q, i_q, a_k, a_v, b_q, b_k, b_v, i_k, i_w, pad], axis=1).astype(BF16)


def _in_ab_body(x_ref, g_ref, w_ref, cos_ref, sin_ref,
                aq_o, iq_o, bq_o, ak_o, akt_o, av_o, av16_o, ik_o, ikt_o, iw_o, bk_o, bkt_o, bv_o, bv16_o):
    xn = _rms(x_ref[...], g_ref[...], NORM_EPS).astype(BF16)
    cos = cos_ref[...]
    sin = sin_ref[...]
    first = _first_half_mask(cos.shape)

    def proj(lo, width):
        return _dot(xn, w_ref[:, lo:lo + width])

    _split_heads(_rope_wide(proj(_AB_AQ, 512), cos, sin, first), aq_o, QK_SCALE)
    _split_heads(_rope_wide(proj(_AB_IQ, 512), cos, sin, first), iq_o, QK_SCALE)
    _split_heads(proj(_AB_BQ, 512), bq_o, QK_SCALE)

    ak = _rope_chunk(proj(_AB_AK, _A_KV_W), cos, sin, first)
    ak_o[...] = ak
    _split_heads_t(ak, akt_o)
    av = proj(_AB_AV, _A_KV_W)
    av_o[...] = av
    av16_o[...] = av.astype(BF16)

    bk = proj(_AB_BK, _B_W)
    bk_o[...] = bk
    _split_heads_t(bk, bkt_o)
    bv = proj(_AB_BV, _B_W)
    bv_o[...] = bv
    bv16_o[...] = bv.astype(BF16)

    tail = proj(_AB_TAIL, LANES)
    ik = _rope_chunk(tail, cos, sin, first)
    ik_o[...] = ik[:, :IDX_DIM]
    ikt_o[...] = ik.T[:IDX_DIM, :].astype(BF16)
    iw_o[...] = tail[:, IDX_DIM:IDX_DIM + IDX_HEADS] * (IDX_HEADS ** -0.5)


def _in_proj_ab(x2d, g, w_perm, cos, sin):
    m, d = x2d.shape
    tm = min(ROW_TILE, m)
    n_pos_blocks = cos.shape[0] // tm
    row = lambda i: (i, 0)
    hrow = lambda i: (0, i, 0)
    hcol = lambda i: (0, 0, i)
    pos = lambda i: (i % n_pos_blocks, 0)
    const = lambda i: (0, 0)
    sds = jax.ShapeDtypeStruct
    hm = lambda heads: sds((heads, m, HEAD_DIM), BF16)
    hmt = lambda heads: sds((heads, HEAD_DIM, m), BF16)
    out_shape = (
        hm(A_HEADS), hm(IDX_HEADS), hm(B_HEADS),
        sds((m, _A_KV_W), F32), hmt(A_KV_HEADS), sds((m, _A_KV_W), F32), sds((m, _A_KV_W), BF16),
        sds((m, IDX_DIM), F32), sds((IDX_DIM, m), BF16), sds((m, IDX_HEADS), F32),
        sds((m, _B_W), F32), hmt(B_HEADS), sds((m, _B_W), F32), sds((m, _B_W), BF16),
    )
    hspec = lambda heads: pl.BlockSpec((heads, tm, HEAD_DIM), hrow)
    htspec = lambda heads: pl.BlockSpec((heads, HEAD_DIM, tm), hcol)
    rspec = lambda width: pl.BlockSpec((tm, width), row)
    out_specs = (
        hspec(A_HEADS), hspec(IDX_HEADS), hspec(B_HEADS),
        rspec(_A_KV_W), htspec(A_KV_HEADS), rspec(_A_KV_W), rspec(_A_KV_W),
        rspec(IDX_DIM), pl.BlockSpec((IDX_DIM, tm), lambda i: (0, i)), rspec(IDX_HEADS),
        rspec(_B_W), htspec(B_HEADS), rspec(_B_W), rspec(_B_W),
    )
    return pl.pallas_call(
        _in_ab_body,
        grid=(m // tm,),
        in_specs=[pl.BlockSpec((tm, d), row), pl.BlockSpec((1, d), const), pl.BlockSpec((d, _AB_WIDTH), const),
                  pl.BlockSpec((tm, LANES), pos), pl.BlockSpec((tm, LANES), pos)],
        out_specs=out_specs,
        out_shape=out_shape,
        compiler_params=_cparams("parallel"),
        name="in_proj_ab",
    )(x2d, g.reshape(1, d), w_perm, cos, sin)


_C_Q_W = C_HEADS * 2 * HEAD_DIM
_C_KV_W = C_KV_HEADS * 2 * HEAD_DIM


def _in_c_body(x_ref, g_ref, w_ref, cos_ref, sin_ref, cq_o, ck_o, ckt_o, cv_o, cv16_o):
    xn = _rms(x_ref[...], g_ref[...], NORM_EPS).astype(BF16)
    cos = cos_ref[...]
    sin = sin_ref[...]
    first = _first_half_mask(cos.shape)

    def proj(lo, width):
        return _dot(xn, w_ref[:, lo:lo + width])

    rep = C_HEADS // C_KV_HEADS
    for block in range(2):
        q = _rope_wide(proj(block * 512, 512), cos, sin, first)
        for i in range(8):
            head, half = divmod(block * 8 + i, 2)
            slot = ((head // rep) * 2 + half) * rep + head % rep
            cq_o[slot] = (q[:, i * HEAD_DIM:(i + 1) * HEAD_DIM] * QK_SCALE).astype(BF16)
    k = _rope_wide(proj(_C_Q_W, _C_KV_W), cos, sin, first)
    ck_o[...] = k
    _split_heads_t(k, ckt_o)
    v = proj(_C_Q_W + _C_KV_W, _C_KV_W)
    cv_o[...] = v
    cv16_o[...] = v.astype(BF16)


def _in_proj_c(x2d, g, w_bf16, cos, sin):
    m, d = x2d.shape
    tm = min(ROW_TILE, m)
    n_pos_blocks = cos.shape[0] // tm
    row = lambda i: (i, 0)
    hrow = lambda i: (0, i, 0)
    hcol = lambda i: (0, 0, i)
    pos = lambda i: (i % n_pos_blocks, 0)
    const = lambda i: (0, 0)
    width = _C_Q_W + 2 * _C_KV_W
    sds = jax.ShapeDtypeStruct
    out_shape = (
        sds((2 * C_HEADS, m, HEAD_DIM), BF16),
        sds((m, _C_KV_W), F32), sds((2 * C_KV_HEADS, HEAD_DIM, m), BF16),
        sds((m, _C_KV_W), F32), sds((m, _C_KV_W), BF16),
    )
    out_specs = (
        pl.BlockSpec((2 * C_HEADS, tm, HEAD_DIM), hrow),
        pl.BlockSpec((tm, _C_KV_W), row), pl.BlockSpec((2 * C_KV_HEADS, HEAD_DIM, tm), hcol),
        pl.BlockSpec((tm, _C_KV_W), row), pl.BlockSpec((tm, _C_KV_W), row),
    )
    return pl.pallas_call(
        _in_c_body,
        grid=(m // tm,),
        in_specs=[pl.BlockSpec((tm, d), row), pl.BlockSpec((1, d), const), pl.BlockSpec((d, width), const),
                  pl.BlockSpec((tm, LANES), pos), pl.BlockSpec((tm, LANES), pos)],
        out_specs=out_specs,
        out_shape=out_shape,
        compiler_params=_cparams("parallel"),
        name="in_proj_c",
    )(x2d, g.reshape(1, d), w_bf16, cos, sin)


def _post_body(o_ref, x_ref, wout_ref, gpost_ref, gpre_ref, wup_ref, wdown_ref, gfpost_ref, y_ref):
    mix = _dot(o_ref[...], wout_ref[...])
    x1 = x_ref[...] + _rms(mix, gpost_ref[...], NORM_EPS)
    hn = _rms(x1, gpre_ref[...], NORM_EPS).astype(BF16)
    d_ff = wup_ref.shape[1]
    step = 1024
    f = jnp.zeros(x1.shape, F32)
    for c in range(d_ff // step):
        u = jnp.maximum(_dot(hn, wup_ref[:, c * step:(c + 1) * step]), 0.0)
        f = f + _dot((u * u).astype(BF16), wdown_ref[c * step:(c + 1) * step, :])
    y_ref[...] = x1 + _rms(f, gfpost_ref[...], NORM_EPS)


def _post_block(o2d, x2d, w_out, g_post, g_pre, w_up, w_down, g_fpost):
    m, d = x2d.shape
    tm = min(ROW_TILE, m)
    d_ff = w_up.shape[1]
    row = lambda i: (i, 0)
    const = lambda i: (0, 0)
    return pl.pallas_call(
        _post_body,
        grid=(m // tm,),
        in_specs=[pl.BlockSpec((tm, o2d.shape[1]), row), pl.BlockSpec((tm, d), row),
                  pl.BlockSpec(w_out.shape, const), pl.BlockSpec((1, d), const), pl.BlockSpec((1, d), const),
                  pl.BlockSpec((d, d_ff), const), pl.BlockSpec((d_ff, d), const), pl.BlockSpec((1, d), const)],
        out_specs=pl.BlockSpec((tm, d), row),
        out_shape=jax.ShapeDtypeStruct((m, d), F32),
        compiler_params=_cparams("parallel"),
        name="post_block",
    )(o2d, x2d, w_out, g_post.reshape(1, d), g_pre.reshape(1, d), w_up, w_down, g_fpost.reshape(1, d))


def _order_key(s):
    s = jnp.where(s == 0.0, 0.0, s)
    b = pltpu.bitcast(s, I32)
    return b ^ ((b >> 31) & I32(0x7FFFFFFF))


def _select_bias(sc_ref, key_ref, nch, n_sel, rows, width):
    def fill_keys(j, c):
        key_ref[j] = _order_key(sc_ref[j])
        return c

    lax.fori_loop(0, nch, fill_keys, 0)

    def count(pred):
        def body(j, c):
            return c + jnp.where(pred(key_ref[j]), 1.0, 0.0)

        c = lax.fori_loop(0, nch, body, jnp.zeros((rows, width), F32))
        return jnp.sum(c, axis=1, keepdims=True)

    k_sel = float(n_sel)
    int_min = I32(-2 ** 31)
    thr = jnp.where(count(lambda k: k >= 0) >= k_sel, I32(0), int_min)

    def bit_body(i, t):
        cand = t + lax.shift_left(I32(1), I32(30) - i)
        cand_b = jnp.broadcast_to(cand, (rows, width))
        return jnp.where(count(lambda k: k >= cand_b) >= k_sel, cand, t)

    thr = lax.fori_loop(0, 31, bit_body, thr)
    thr_b = jnp.broadcast_to(thr, (rows, width))
    need_b = jnp.broadcast_to(k_sel - count(lambda k: k > thr_b), (rows, width))
    r = lax.broadcasted_iota(I32, (width, width), 0)
    c = lax.broadcasted_iota(I32, (width, width), 1)
    upper = jnp.where(r < c, 1.0, 0.0).astype(BF16)
    ones = jnp.ones((width, width), BF16)

    def tie_body(j, run):
        k = key_ref[j]
        eq = jnp.where(k == thr_b, 1.0, 0.0)
        eq16 = eq.astype(BF16)
        rank = run + _dot(eq16, upper)
        bias = jnp.where(k > thr_b, 0.0, jnp.where(eq * jnp.where(rank < need_b, 1.0, 0.0) > 0.5, 0.0, MASKED))
        sc_ref[j] = jnp.where(sc_ref[j] == -jnp.inf, MASKED, bias)
        return run + _dot(eq16, ones)

    lax.fori_loop(0, nch, tie_body, jnp.zeros((rows, width), F32))


def _weighted_values(p16, v16, v_transposed):
    return _dot_nt(p16, v16) if v_transposed else _dot(p16, v16)


def _softmax_step(s, values, m, l, acc, v_transposed=False):
    m_new = jnp.maximum(m, jnp.max(s, axis=1, keepdims=True))
    alpha = jnp.exp(m - m_new)
    p = jnp.exp(s - m_new)
    l = alpha * l + jnp.sum(p, axis=1, keepdims=True)
    p16 = p.astype(BF16)
    if not isinstance(values, (list, tuple)):
        pv = _weighted_values(p16, values, v_transposed)
    else:
        pv = sum(_weighted_values(p16[:, i * LANES:(i + 1) * LANES], v, v_transposed) for i, v in enumerate(values))
    return m_new, l, alpha * acc + pv


def _stick_matrices():
    r = lax.broadcasted_iota(I32, (2 * LANES, 2 * LANES), 0) & (LANES - 1)
    c = lax.broadcasted_iota(I32, (2 * LANES, 2 * LANES), 1)
    return jnp.where((c >= LANES) | (r > c), 1.0, 0.0).astype(BF16)


def _softmax_ones_step(s, v_ext16, m, la):
    tiles = s.shape[1] // LANES
    m_new = jnp.maximum(m, jnp.max(s.astype(BF16), axis=1, keepdims=True).astype(F32))
    alpha = jnp.exp(m - m_new)
    p16 = jnp.exp(s - jnp.concatenate([m_new] * tiles, axis=1)).astype(BF16)
    return m_new, jnp.concatenate([alpha] * (la.shape[1] // LANES), axis=1) * la + _dot(p16, v_ext16)


def _stick_weights(zs, valid, run, mats):
    rows = zs[0].shape[0]
    softplus = [jnp.maximum(z, 0.0) + jnp.log(1.0 + jnp.exp(-jnp.abs(z))) for z in zs]
    split = []
    for sp in softplus:
        log_stay = -sp if valid is None else jnp.where(valid, -sp, 0.0)
        hi = log_stay.astype(BF16)
        lo = (log_stay - hi.astype(F32)).astype(BF16)
        split.append(jnp.concatenate([hi, lo], axis=1))
    sums = _dot(split[0] if len(split) == 1 else jnp.concatenate(split, axis=0), mats)
    ws = []
    for n, (z, sp) in enumerate(zip(zs, softplus)):
        part = sums[n * rows:(n + 1) * rows]
        w = jnp.exp((z - sp) + part[:, :LANES] + run)
        if valid is not None:
            w = jnp.where(valid, w, 0.0)
        ws.append(w.astype(BF16))
        run = run + part[:, LANES:]
    return run, ws


def _chunk_count(qi, qb=QB):
    return lax.shift_right_logical((qi + 1) * qb - 1, KC.bit_length() - 1) + 1


def _chunk_lanes(j):
    return pl.ds(pl.multiple_of(j * KC, KC), KC)


def _att0_prompt_body(n_sel, aq_ref, iq_ref, iw_ref, bq_ref, akt_ref, av_ref, ikt_ref, bkt_ref, bv_ref,
                      o_ref, sc_ref, key_ref, m_ref, la_ref, run_ref, acc_ref):
    qi = pl.program_id(1)
    n_kc = _chunk_count(qi)
    row = lax.broadcasted_iota(I32, (QB, KC), 0)
    col = lax.broadcasted_iota(I32, (QB, KC), 1)
    qpos = qi * QB + row

    def score_chunk(j, c):
        kt = ikt_ref[:, _chunk_lanes(j)]
        acc = jnp.zeros((QB, KC), F32)
        for h in range(IDX_HEADS):
            acc = acc + jnp.maximum(_dot(iq_ref[h], kt), 0.0) * iw_ref[:, h:h + 1]
        sc_ref[j] = jnp.where(j * KC + col <= qpos, acc, -jnp.inf)
        return c

    lax.fori_loop(0, n_kc, score_chunk, 0)
    _select_bias(sc_ref, key_ref, n_kc, n_sel, QB, KC)

    m_ref[...] = jnp.full_like(m_ref, MASKED)
    la_ref[...] = jnp.zeros_like(la_ref)
    rep = A_HEADS // A_KV_HEADS
    ones = jnp.ones((KC, LANES), BF16)

    def dsa_chunk(j, c):
        bias = jnp.concatenate([sc_ref[j]] * rep, axis=0)
        v_ext = jnp.concatenate([av_ref[_chunk_lanes(j), :], ones], axis=1)
        for g in range(A_KV_HEADS):
            q = aq_ref[rep * g:rep * (g + 1)].reshape(rep * QB, HEAD_DIM)
            s = _dot(q, akt_ref[g, :, _chunk_lanes(j)]) + bias
            m_ref[g], la_ref[g] = _softmax_ones_step(s, v_ext, m_ref[g], la_ref[g])
        return c

    lax.fori_loop(0, n_kc, dsa_chunk, 0)
    outs = []
    for h in range(A_HEADS):
        g, r = h // rep, h % rep
        la = la_ref[g][r * QB:(r + 1) * QB]
        outs.append(la[:, g * HEAD_DIM:(g + 1) * HEAD_DIM] / la[:, LANES:LANES + HEAD_DIM])
    o_ref[:, :A_HEADS * HEAD_DIM] = jnp.concatenate(outs, axis=1).astype(o_ref.dtype)

    run_ref[...] = jnp.zeros_like(run_ref)
    acc_ref[...] = jnp.zeros_like(acc_ref)
    mats = _stick_matrices()
    rows_all = B_HEADS * QB
    qpos_c = qi * QB + (lax.broadcasted_iota(I32, (rows_all, LANES), 0) & (QB - 1))
    col_c = lax.broadcasted_iota(I32, (rows_all, LANES), 1)

    def sb_keys(starts, on_diagonal):
        keys = [pl.ds(pl.multiple_of(start, LANES), LANES) for start in starts]
        zs = [jnp.concatenate([_dot(bq_ref[h], bkt_ref[h, :, k]) for h in range(B_HEADS)], axis=0) for k in keys]
        valid = (starts[0] + col_c < qpos_c) if on_diagonal else None
        run_ref[...], ws = _stick_weights(zs, valid, run_ref[...], mats)
        for pair in range(B_HEADS // 2):
            rows = slice(2 * pair * QB, 2 * (pair + 1) * QB)
            acc_ref[pair] += sum(_dot(w16[rows], bv_ref[k, pair * LANES:(pair + 1) * LANES]) for w16, k in zip(ws, keys))

    n_full = lax.shift_right_logical(qi * QB, KC.bit_length() - 1)
    sb_keys([qi * QB], True)
    pl.when(n_full * KC < qi * QB)(functools.partial(sb_keys, [n_full * KC], False))

    def sb_chunk(i, c):
        j = n_full - 1 - i
        sb_keys([j * KC + sub * LANES for sub in reversed(range(KC // LANES))], False)
        return c

    lax.fori_loop(0, n_full, sb_chunk, 0)
    outs = [acc_ref[h // 2][(h % 2) * QB:(h % 2 + 1) * QB, (h % 2) * HEAD_DIM:(h % 2 + 1) * HEAD_DIM]
            for h in range(B_HEADS)]
    o_ref[:, A_HEADS * HEAD_DIM:] = jnp.concatenate(outs, axis=1).astype(o_ref.dtype)


def _att0_prompt(aq, iq, iw, bq, akt, av16, ikt, bkt, bv16, bsz, t):
    nq = t // QB
    n_sel = min(TOPK_MAX, t // 4)
    qh = lambda b, q: (0, b * nq + q, 0)
    qrow = lambda b, q: (b * nq + q, 0)
    kt = lambda b, q: (0, 0, b)
    krow = lambda b, q: (b, 0)
    width = (A_HEADS + B_HEADS) * HEAD_DIM
    rows_a = (A_HEADS // A_KV_HEADS) * QB
    return pl.pallas_call(
        functools.partial(_att0_prompt_body, n_sel),
        grid=(bsz, nq),
        in_specs=[pl.BlockSpec((A_HEADS, QB, HEAD_DIM), qh), pl.BlockSpec((IDX_HEADS, QB, IDX_DIM), qh),
                  pl.BlockSpec((QB, IDX_HEADS), qrow), pl.BlockSpec((B_HEADS, QB, HEAD_DIM), qh),
                  pl.BlockSpec((A_KV_HEADS, HEAD_DIM, t), kt), pl.BlockSpec((t, _A_KV_W), krow),
                  pl.BlockSpec((IDX_DIM, t), lambda b, q: (0, b)),
                  pl.BlockSpec((B_HEADS, HEAD_DIM, t), kt), pl.BlockSpec((t, _B_W), krow)],
        out_specs=pl.BlockSpec((QB, width), qrow),
        out_shape=jax.ShapeDtypeStruct((bsz * t, width), BF16),
        scratch_shapes=[pltpu.VMEM((t // KC, QB, KC), F32), pltpu.VMEM((t // KC, QB, KC), I32),
                        pltpu.VMEM((A_KV_HEADS, rows_a, LANES), F32), pltpu.VMEM((A_KV_HEADS, rows_a, 2 * LANES), F32),
                        pltpu.VMEM((B_HEADS * QB, LANES), F32), pltpu.VMEM((B_HEADS // 2, 2 * QB, LANES), F32)],
        compiler_params=_cparams("parallel", "arbitrary"),
        name="att0_prompt",
    )(aq, iq, iw, bq, akt, av16, ikt, bkt, bv16)


def _diff_lambda(lq1_ref, lk1_ref, lq2_ref, lk2_ref, lambda_init):
    s1 = jnp.sum(lq1_ref[...] * lk1_ref[...], axis=1, keepdims=True)
    s2 = jnp.sum(lq2_ref[...] * lk2_ref[...], axis=1, keepdims=True)
    return jnp.exp(s1) - jnp.exp(s2) + lambda_init


def _diff_finish(a1, l1, a2, l2, lam, gsub, lambda_init):
    o = a1 / l1 - lam * (a2 / l2)
    return _rms(o, gsub, SUBLN_EPS) * (1.0 - lambda_init)


def _att1_prompt_body(lambda_init, cq_ref, ckt_ref, cv_ref, lq1_ref, lk1_ref, lq2_ref, lk2_ref, gsub_ref,
                      o_ref, m_ref, la_ref):
    qi = pl.program_id(1)
    qb = o_ref.shape[0]
    rep = C_HEADS // C_KV_HEADS
    rows = rep * qb
    n_kc = _chunk_count(qi, qb)
    row = lax.broadcasted_iota(I32, (rows, KC), 0)
    col = lax.broadcasted_iota(I32, (rows, KC), 1)
    qpos = qi * qb + (row & (qb - 1))
    m_ref[...] = jnp.full_like(m_ref, MASKED)
    la_ref[...] = jnp.zeros_like(la_ref)
    ones = jnp.ones((KC, LANES), BF16)

    def chunk(j, on_diagonal):
        bias = jnp.where(j * KC + col <= qpos, 0.0, MASKED) if on_diagonal else None
        for g in range(C_KV_HEADS):
            v_ext = jnp.concatenate([cv_ref[_chunk_lanes(j), g * LANES:(g + 1) * LANES], ones], axis=1)
            for half in range(2):
                u = 2 * g + half
                q = cq_ref[rep * u:rep * (u + 1)].reshape(rows, HEAD_DIM)
                s = _dot(q, ckt_ref[u, :, _chunk_lanes(j)])
                if on_diagonal:
                    s = s + bias
                m_ref[u], la_ref[u] = _softmax_ones_step(s, v_ext, m_ref[u], la_ref[u])

    def full_chunk(j, c):
        chunk(j, False)
        return c

    lax.fori_loop(0, n_kc - 1, full_chunk, 0)
    chunk(n_kc - 1, True)

    lam = _diff_lambda(lq1_ref, lk1_ref, lq2_ref, lk2_ref, lambda_init)
    gsub = gsub_ref[...]
    for h in range(C_HEADS):
        g, r = h // rep, h % rep
        la1 = la_ref[2 * g][r * qb:(r + 1) * qb]
        la2 = la_ref[2 * g + 1][r * qb:(r + 1) * qb]
        o = _diff_finish(la1[:, :LANES], la1[:, LANES:], la2[:, :LANES], la2[:, LANES:], lam, gsub, lambda_init)
        o_ref[:, h * LANES:(h + 1) * LANES] = o.astype(o_ref.dtype)


def _att1_prompt(cq, ckt, cv16, lq1, lk1, lq2, lk2, gsub, lambda_init, bsz, t):
    qb = QB_DIFF
    nq = t // qb
    qh = lambda b, q: (0, b * nq + q, 0)
    qrow = lambda b, q: (b * nq + q, 0)
    const = lambda b, q: (0, 0)
    vec = lambda a: a.reshape(1, -1)
    rows = (C_HEADS // C_KV_HEADS) * qb
    return pl.pallas_call(
        functools.partial(_att1_prompt_body, lambda_init),
        grid=(bsz, nq),
        in_specs=[pl.BlockSpec((2 * C_HEADS, qb, HEAD_DIM), qh),
                  pl.BlockSpec((2 * C_KV_HEADS, HEAD_DIM, t), lambda b, q: (0, 0, b)),
                  pl.BlockSpec((t, _C_KV_W), lambda b, q: (b, 0)),
                  pl.BlockSpec((1, HEAD_DIM), const), pl.BlockSpec((1, HEAD_DIM), const),
                  pl.BlockSpec((1, HEAD_DIM), const), pl.BlockSpec((1, HEAD_DIM), const),
                  pl.BlockSpec((1, LANES), const)],
        out_specs=pl.BlockSpec((qb, _C_Q_W), qrow),
        out_shape=jax.ShapeDtypeStruct((bsz * t, _C_Q_W), BF16),
        scratch_shapes=[pltpu.VMEM((2 * C_KV_HEADS, rows, LANES), F32),
                        pltpu.VMEM((2 * C_KV_HEADS, rows, 2 * LANES), F32)],
        compiler_params=_cparams("parallel", "arbitrary"),
        name="att1_prompt",
    )(cq, ckt, cv16, vec(lq1), vec(lk1), vec(lq2), vec(lk2), vec(gsub))


def _page_specs(block, n_groups, group, newest_first):
    zeros = (0,) * (len(block) - 2)

    def index(i):
        if newest_first:
            return lambda b, s, pt: (0, pt[b, jnp.minimum(n_groups - s, n_groups - 1) * group + i]) + zeros
        return lambda b, s, pt: (0, pt[b, jnp.minimum(s, n_groups - 1) * group + i]) + zeros

    return [pl.BlockSpec(block, index(i)) for i in range(group)]


def _keys_on_lanes(cache):
    nd = cache.ndim
    return jnp.transpose(cache, (0, 1) + tuple(range(3, nd)) + (2,))


def _att0_sample_walk_body(dec, group, pt_ref, iq_ref, iw_ref, bq_ref, ikn_ref, bkn_ref, bvn_ref, *refs):
    del pt_ref
    ikc_refs, bkc_refs, bvc_refs = refs[:group], refs[group:2 * group], refs[2 * group:3 * group]
    sc_o, ob_o, run_ref, acc_ref = refs[3 * group:]
    s = pl.program_id(1)
    rows = B_HEADS * dec
    row = lax.broadcasted_iota(I32, (rows, LANES), 0)
    col = lax.broadcasted_iota(I32, (rows, LANES), 1)
    t_of_row = row & (dec - 1)
    mats = _stick_matrices()

    def scores(kt16):
        lg = jnp.maximum(_dot(iq_ref[...], kt16), 0.0) * iw_ref[...]
        out = lg[0:dec]
        for h in range(1, IDX_HEADS):
            out = out + lg[h * dec:(h + 1) * dec]
        return out

    def stick(kts, vts, valid):
        run_ref[...], ws = _stick_weights([_dot(bq_ref[...], kt) for kt in kts], valid, run_ref[...], mats)
        acc_ref[...] += sum(_dot_nt(w16, vt) for w16, vt in zip(ws, vts))

    @pl.when(s == 0)
    def _():
        run_ref[...] = jnp.zeros_like(run_ref)
        acc_ref[...] = jnp.zeros_like(acc_ref)
        causal = (col[:dec] < dec) & (col[:dec] <= row[:dec])
        sc_o[0] = jnp.where(causal, scores(ikn_ref[...]), -jnp.inf)
        for i in range(1, group):
            sc_o[i] = jnp.full((dec, LANES), -jnp.inf, F32)
        stick([bkn_ref[...]], [bvn_ref[...]], (col < dec) & (col < t_of_row))

    @pl.when(s > 0)
    def _():
        for i in range(group):
            sc_o[i] = scores(ikc_refs[i][...].astype(BF16))
        stick([bkc_refs[i][...].reshape(_B_W, LANES).astype(BF16) for i in reversed(range(group))],
              [bvc_refs[i][...].reshape(_B_W, LANES).astype(BF16) for i in reversed(range(group))], None)

    @pl.when(s == pl.num_programs(1) - 1)
    def _():
        acc = acc_ref[...]
        ob_o[...] = jnp.concatenate([acc[h * dec:(h + 1) * dec, h * HEAD_DIM:(h + 1) * HEAD_DIM]
                                     for h in range(B_HEADS)], axis=1).astype(ob_o.dtype)


def _att0_sample_walk(page_table, iq_st, iw_st, bq_bd, ikt_new, bkt_new, bvt_new, cache_kidx, cache_bk, cache_bv):
    db, n_pages = page_table.shape
    group = math.gcd(PAGE_GROUP, n_pages)
    n_groups = n_pages // group
    dec = iq_st.shape[1] // IDX_HEADS
    rows = B_HEADS * dec
    per_b = lambda b, s, pt: (b, 0, 0)
    chunks_of_step = lambda b, s, pt: (jnp.where(s == 0, n_groups, n_groups - s), b, 0, 0)
    b_page = (None, None, B_HEADS, HEAD_DIM, LANES)
    grid_spec = pltpu.PrefetchScalarGridSpec(
        num_scalar_prefetch=1,
        grid=(db, n_groups + 1),
        in_specs=[pl.BlockSpec((None, rows, IDX_DIM), per_b), pl.BlockSpec((None, rows, 1), per_b),
                  pl.BlockSpec((None, rows, _B_W), per_b),
                  pl.BlockSpec((None, IDX_DIM, LANES), per_b), pl.BlockSpec((None, _B_W, LANES), per_b),
                  pl.BlockSpec((None, _B_W, LANES), per_b)]
        + _page_specs((None, None, IDX_DIM, LANES), n_groups, group, True)
        + _page_specs(b_page, n_groups, group, True) + _page_specs(b_page, n_groups, group, True),
        out_specs=[pl.BlockSpec((group, None, dec, LANES), chunks_of_step), pl.BlockSpec((None, dec, _B_W), per_b)],
        scratch_shapes=[pltpu.VMEM((rows, LANES), F32), pltpu.VMEM((rows, _B_W), F32)],
    )
    return pl.pallas_call(
        functools.partial(_att0_sample_walk_body, dec, group),
        grid_spec=grid_spec,
        out_shape=(jax.ShapeDtypeStruct(((n_groups + 1) * group, db, dec, LANES), F32),
                   jax.ShapeDtypeStruct((db, dec, _B_W), BF16)),
        compiler_params=_cparams("parallel", "arbitrary"),
        name="att0_sample_walk",
    )(page_table, iq_st, iw_st, bq_bd, ikt_new, bkt_new, bvt_new,
      *([cache_kidx] * group + [cache_bk] * group + [cache_bv] * group))


def _select_sample_body(n_sel, nch, sc_ref, bias_o, key_ref):
    rows = sc_ref.shape[1]
    bias_o[...] = sc_ref[...]
    _select_bias(bias_o, key_ref, nch, n_sel, rows, LANES)


def _select_sample(scores, nch, n_sel):
    n_alloc, m, _ = scores.shape
    rows = min(m, 64)
    spec = pl.BlockSpec((n_alloc, rows, LANES), lambda i: (0, i, 0))
    return pl.pallas_call(
        functools.partial(_select_sample_body, n_sel, nch),
        grid=(m // rows,),
        in_specs=[spec],
        out_specs=spec,
        out_shape=jax.ShapeDtypeStruct(scores.shape, F32),
        scratch_shapes=[pltpu.VMEM((n_alloc, rows, LANES), I32)],
        compiler_params=_cparams("parallel"),
        name="select_sample",
    )(scores)


def _att0_sample_dsa_body(n_groups, dec, group, pt_ref, aq_ref, bias_ref, akn_ref, avn_ref, *refs):
    del pt_ref
    akc_refs, avc_refs = refs[:group], refs[group:2 * group]
    oa_o, m_ref, l_ref, acc_ref = refs[2 * group:]
    s = pl.program_id(1)

    @pl.when(s == 0)
    def _():
        m_ref[...] = jnp.full_like(m_ref, MASKED)
        l_ref[...] = jnp.zeros_like(l_ref)
        acc_ref[...] = jnp.zeros_like(acc_ref)

    def step(kts, vts):
        n = len(kts)
        bias = jnp.concatenate([jnp.concatenate([bias_ref[i]] * A_HEADS, axis=0) for i in range(n)], axis=1)
        sc = jnp.concatenate([_dot(aq_ref[...], kt) for kt in kts], axis=1) + bias
        m_ref[...], l_ref[...], acc_ref[...] = _softmax_step(sc, vts, m_ref[...], l_ref[...], acc_ref[...],
                                                             v_transposed=True)

    @pl.when(s < n_groups)
    def _():
        step([r[...].reshape(_A_KV_W, LANES).astype(BF16) for r in akc_refs],
             [r[...].reshape(_A_KV_W, LANES).astype(BF16) for r in avc_refs])

    @pl.when(s == n_groups)
    def _():
        step([akn_ref[...]], [avn_ref[...]])
        o = acc_ref[...] / l_ref[...]
        rep = A_HEADS // A_KV_HEADS
        oa_o[...] = jnp.concatenate(
            [o[h * dec:(h + 1) * dec, (h // rep) * HEAD_DIM:(h // rep + 1) * HEAD_DIM] for h in range(A_HEADS)],
            axis=1).astype(oa_o.dtype)


def _att0_sample_dsa(page_table, aq_bd, bias, akt_new, avt_new, cache_akt, cache_avt):
    db, n_pages = page_table.shape
    group = math.gcd(PAGE_GROUP, n_pages)
    n_groups = n_pages // group
    rows = aq_bd.shape[1]
    dec = rows // A_HEADS
    per_b = lambda b, s, pt: (b, 0, 0)
    a_page = (None, None, A_KV_HEADS, HEAD_DIM, LANES)
    grid_spec = pltpu.PrefetchScalarGridSpec(
        num_scalar_prefetch=1,
        grid=(db, n_groups + 1),
        in_specs=[pl.BlockSpec((None, rows, _A_KV_W), per_b),
                  pl.BlockSpec((group, None, dec, LANES), lambda b, s, pt: (s, b, 0, 0)),
                  pl.BlockSpec((None, _A_KV_W, LANES), per_b), pl.BlockSpec((None, _A_KV_W, LANES), per_b)]
        + _page_specs(a_page, n_groups, group, False) + _page_specs(a_page, n_groups, group, False),
        out_specs=pl.BlockSpec((None, dec, A_HEADS * HEAD_DIM), per_b),
        scratch_shapes=[pltpu.VMEM((rows, 1), F32), pltpu.VMEM((rows, 1), F32), pltpu.VMEM((rows, _A_KV_W), F32)],
    )
    return pl.pallas_call(
        functools.partial(_att0_sample_dsa_body, n_groups, dec, group),
        grid_spec=grid_spec,
        out_shape=jax.ShapeDtypeStruct((db, dec, A_HEADS * HEAD_DIM), BF16),
        compiler_params=_cparams("parallel", "arbitrary"),
        name="att0_sample_dsa",
    )(page_table, aq_bd, bias, akt_new, avt_new, *([cache_akt] * group + [cache_avt] * group))


def _att1_sample_body(n_groups, dec, group, lambda_init, pt_ref, cq_ref, ckn_ref, cvn_ref, *refs):
    del pt_ref
    ckc_refs, cvc_refs = refs[:group], refs[group:2 * group]
    lq1_ref, lk1_ref, lq2_ref, lk2_ref, gsub_ref, o_o, m_ref, l_ref, acc_ref = refs[2 * group:]
    s = pl.program_id(1)
    rows_g = 2 * (C_HEADS // C_KV_HEADS) * dec

    @pl.when(s == 0)
    def _():
        m_ref[...] = jnp.full_like(m_ref, MASKED)
        l_ref[...] = jnp.zeros_like(l_ref)
        acc_ref[...] = jnp.zeros_like(acc_ref)

    def step(g, ks, vs, bias):
        r = slice(g * rows_g, (g + 1) * rows_g)
        sc = jnp.concatenate([_dot_nt(cq_ref[r, :], k) for k in ks], axis=1)
        if bias is not None:
            sc = sc + bias
        m_ref[r, :], l_ref[r, :], acc_ref[r, :] = _softmax_step(sc, vs, m_ref[r, :], l_ref[r, :], acc_ref[r, :])

    @pl.when(s < n_groups)
    def _():
        for g in range(C_KV_HEADS):
            head_rows = pl.ds(g, LANES, stride=C_KV_HEADS)
            step(g, [r[head_rows, :].astype(BF16) for r in ckc_refs],
                 [r[head_rows, :].astype(BF16) for r in cvc_refs], None)

    @pl.when(s == n_groups)
    def _():
        row = lax.broadcasted_iota(I32, (rows_g, LANES), 0)
        col = lax.broadcasted_iota(I32, (rows_g, LANES), 1)
        bias = jnp.where(col <= (row & (dec - 1)), 0.0, MASKED)
        for g in range(C_KV_HEADS):
            cols = slice(g * LANES, (g + 1) * LANES)
            step(g, [ckn_ref[:, cols]], [cvn_ref[:, cols]], bias)
        lam = _diff_lambda(lq1_ref, lk1_ref, lq2_ref, lk2_ref, lambda_init)
        acc = acc_ref[...]
        l = l_ref[...]
        outs = []
        rep = C_HEADS // C_KV_HEADS
        for h in range(C_HEADS):
            g, r = h // rep, h % rep
            r1 = ((2 * g) * rep + r) * dec
            r2 = ((2 * g + 1) * rep + r) * dec
            outs.append(_diff_finish(acc[r1:r1 + dec], l[r1:r1 + dec], acc[r2:r2 + dec], l[r2:r2 + dec],
                                     lam, gsub_ref[...], lambda_init))
        o_o[...] = jnp.concatenate(outs, axis=1).astype(o_o.dtype)


def _att1_sample(page_table, cq_bd, ck_new, cv_new, cache_ck, cache_cv, lq1, lk1, lq2, lk2, gsub, lambda_init):
    db, n_pages = page_table.shape
    group = math.gcd(PAGE_GROUP, n_pages)
    n_groups = n_pages // group
    rows = cq_bd.shape[1]
    dec = rows // (2 * C_HEADS)
    per_b = lambda b, s, pt: (b, 0, 0)
    const = lambda b, s, pt: (0, 0)
    vec = lambda a: a.reshape(1, -1)
    c_page = (None, None, LANES * C_KV_HEADS, LANES)
    grid_spec = pltpu.PrefetchScalarGridSpec(
        num_scalar_prefetch=1,
        grid=(db, n_groups + 1),
        in_specs=[pl.BlockSpec((None, rows, LANES), per_b),
                  pl.BlockSpec((None, LANES, _C_KV_W), per_b), pl.BlockSpec((None, LANES, _C_KV_W), per_b)]
        + _page_specs(c_page, n_groups, group, False) + _page_specs(c_page, n_groups, group, False)
        + [pl.BlockSpec((1, HEAD_DIM), const), pl.BlockSpec((1, HEAD_DIM), const),
           pl.BlockSpec((1, HEAD_DIM), const), pl.BlockSpec((1, HEAD_DIM), const), pl.BlockSpec((1, LANES), const)],
        out_specs=pl.BlockSpec((None, dec, _C_Q_W), per_b),
        scratch_shapes=[pltpu.VMEM((rows, 1), F32), pltpu.VMEM((rows, 1), F32), pltpu.VMEM((rows, LANES), F32)],
    )
    return pl.pallas_call(
        functools.partial(_att1_sample_body, n_groups, dec, group, lambda_init),
        grid_spec=grid_spec,
        out_shape=jax.ShapeDtypeStruct((db, dec, _C_Q_W), BF16),
        compiler_params=_cparams("parallel", "arbitrary"),
        name="att1_sample",
    )(page_table, cq_bd, ck_new, cv_new, *([cache_ck] * group + [cache_cv] * group),
      vec(lq1), vec(lk1), vec(lq2), vec(lk2), vec(gsub))


def _rope_tables(pos):
    half = HEAD_DIM // 2
    inv_freq = ROPE_THETA ** (-jnp.arange(half, dtype=F32) / half)
    ang = pos.astype(F32)[:, None] * inv_freq[None, :]
    cos, sin = jnp.cos(ang), jnp.sin(ang)
    return jnp.concatenate([cos] * 4, axis=1), jnp.concatenate([-sin, sin, -sin, sin], axis=1)


def _block_diag_rows(q_hm, db, dec, slot_of_head, n_slots):
    heads, _, w = q_hm.shape
    q = q_hm.reshape(heads, db, dec, w).transpose(1, 0, 2, 3)
    onehot = (jnp.asarray(slot_of_head)[:, None] == jnp.arange(n_slots)[None, :]).astype(q.dtype)
    return (q[:, :, :, None, :] * onehot[None, :, None, :, None]).reshape(db, heads * dec, n_slots * w)


def kernel(x_prompt, x_sample, cache_a_k, cache_a_v, cache_a_kidx, cache_b_k, cache_b_v, cache_c_k, cache_c_v,
           page_table, norm_mix_pre, norm_mix_post, norm_ffn_pre, norm_ffn_post, w_in_ab, w_out_ab, w_in_c, w_out_c,
           c_lambda_q1, c_lambda_k1, c_lambda_q2, c_lambda_k2, c_subln, w_up, w_down):
    bsz, t, d = x_prompt.shape
    db, dec, _ = x_sample.shape
    n_phys, page = cache_a_k.shape[1], cache_a_k.shape[2]
    n_pages = page_table.shape[1]
    past = n_pages * page
    assert page == LANES and t % KC == 0 and dec & (dec - 1) == 0 and dec <= 8
    mp, ms = bsz * t, db * dec

    cos_p, sin_p = _rope_tables(jnp.arange(t, dtype=jnp.int32))
    cos_s, sin_s = _rope_tables(jnp.arange(past, past + dec, dtype=jnp.int32))
    cos_s, sin_s = jnp.tile(cos_s, (db, 1)), jnp.tile(sin_s, (db, 1))

    xp = x_prompt.reshape(mp, d)
    xs = x_sample.reshape(ms, d)
    w_ab = _prep_w_ab(w_in_ab[0])
    w_out_ab16 = w_out_ab[0].astype(BF16)
    w_c16 = w_in_c[0].astype(BF16)
    w_out_c16 = w_out_c[0].astype(BF16)
    w_up16 = w_up.astype(BF16)
    w_down16 = w_down.astype(BF16)
    lambda_init = 0.8 - 0.6 * math.exp(-0.3 * 1)

    (aq_p, iq_p, bq_p, ak_p, akt_p, av_p, av16_p, ik_p, ikt_p, iw_p, bk_p, bkt_p, bv_p, bv16_p) = _in_proj_ab(
        xp, norm_mix_pre[0], w_ab, cos_p, sin_p)
    o_p = _att0_prompt(aq_p, iq_p, iw_p, bq_p, akt_p, av16_p, ikt_p, bkt_p, bv16_p, bsz, t)
    xp = _post_block(o_p, xp, w_out_ab16, norm_mix_post[0], norm_ffn_pre[0], w_up16[0], w_down16[0], norm_ffn_post[0])

    (aq_s, iq_s, bq_s, ak_s, _, av_s, _, ik_s, _, iw_s, bk_s, _, bv_s, _) = _in_proj_ab(
        xs, norm_mix_pre[0], w_ab, cos_s, sin_s)
    iq_st = iq_s.reshape(IDX_HEADS, db, dec, IDX_DIM).transpose(1, 0, 2, 3).reshape(db, IDX_HEADS * dec, IDX_DIM)
    iw_st = iw_s.reshape(db, dec, IDX_HEADS).transpose(0, 2, 1).reshape(db, IDX_HEADS * dec, 1)
    bq_bd = _block_diag_rows(bq_s, db, dec, list(range(B_HEADS)), B_HEADS)
    aq_bd = _block_diag_rows(aq_s, db, dec, [h // (A_HEADS // A_KV_HEADS) for h in range(A_HEADS)], A_KV_HEADS)
    new16 = lambda a: jnp.pad(a.reshape(db, dec, -1).astype(BF16), ((0, 0), (0, LANES - dec), (0, 0)))
    new16t = lambda a: jnp.pad(a.reshape(db, dec, -1).astype(BF16).transpose(0, 2, 1), ((0, 0), (0, 0), (0, LANES - dec)))
    scores, ob_s = _att0_sample_walk(
        page_table, iq_st, iw_st, bq_bd, new16t(ik_s), new16t(bk_s), new16t(bv_s),
        _keys_on_lanes(cache_a_kidx), _keys_on_lanes(cache_b_k), _keys_on_lanes(cache_b_v))
    n_sel_s = min(TOPK_MAX, (past + dec) // 4)
    n_alloc = scores.shape[0]
    bias = _select_sample(scores.reshape(n_alloc, ms, LANES), n_pages + 1, n_sel_s).reshape(n_alloc, db, dec, LANES)
    oa_s = _att0_sample_dsa(page_table, aq_bd, bias, new16t(ak_s), new16t(av_s),
                            _keys_on_lanes(cache_a_k), _keys_on_lanes(cache_a_v))
    o_s = jnp.concatenate([oa_s, ob_s], axis=-1).reshape(ms, -1)
    xs = _post_block(o_s, xs, w_out_ab16, norm_mix_post[0], norm_ffn_pre[0], w_up16[0], w_down16[0], norm_ffn_post[0])

    cq_p, ck_p, ckt_p, cv_p, cv16_p = _in_proj_c(xp, norm_mix_pre[1], w_c16, cos_p, sin_p)
    o_p = _att1_prompt(cq_p, ckt_p, cv16_p, c_lambda_q1[0], c_lambda_k1[0], c_lambda_q2[0], c_lambda_k2[0], c_subln[0],
                       lambda_init, bsz, t)
    yp = _post_block(o_p, xp, w_out_c16, norm_mix_post[1], norm_ffn_pre[1], w_up16[1], w_down16[1], norm_ffn_post[1])

    cq_s, ck_s, _, cv_s, _ = _in_proj_c(xs, norm_mix_pre[1], w_c16, cos_s, sin_s)
    rep = C_HEADS // C_KV_HEADS
    cq_bd = _block_diag_rows(cq_s, db, dec, [(slot // rep) % 2 for slot in range(2 * C_HEADS)], 2)
    o_s = _att1_sample(page_table, cq_bd, new16(ck_s), new16(cv_s),
                       cache_c_k.reshape(-1, n_phys, page * C_KV_HEADS, LANES),
                       cache_c_v.reshape(-1, n_phys, page * C_KV_HEADS, LANES),
                       c_lambda_q1[0], c_lambda_k1[0], c_lambda_q2[0], c_lambda_k2[0], c_subln[0], lambda_init)
    ys = _post_block(o_s.reshape(ms, -1), xs, w_out_c16, norm_mix_post[1], norm_ffn_pre[1], w_up16[1], w_down16[1],
                     norm_ffn_post[1])

    def rows(a, b_, t_, heads, width):
        shape = (1, b_, t_, heads, width) if heads else (1, b_, t_, width)
        return a.reshape(shape)

    return (yp.reshape(bsz, t, d), ys.reshape(db, dec, d),
            rows(ak_p, bsz, t, A_KV_HEADS, HEAD_DIM), rows(av_p, bsz, t, A_KV_HEADS, HEAD_DIM),
            rows(ik_p, bsz, t, 0, IDX_DIM),
            rows(bk_p, bsz, t, B_HEADS, HEAD_DIM), rows(bv_p, bsz, t, B_HEADS, HEAD_DIM),
            rows(ck_p, bsz, t, C_KV_HEADS, 2 * HEAD_DIM), rows(cv_p, bsz, t, C_KV_HEADS, 2 * HEAD_DIM),
            rows(ak_s, db, dec, A_KV_HEADS, HEAD_DIM), rows(av_s, db, dec, A_KV_HEADS, HEAD_DIM),
            rows(ik_s, db, dec, 0, IDX_DIM),
            rows(bk_s, db, dec, B_HEADS, HEAD_DIM), rows(bv_s, db, dec, B_HEADS, HEAD_DIM),
            rows(ck_s, db, dec, C_KV_HEADS, 2 * HEAD_DIM), rows(cv_s, db, dec, C_KV_HEADS, 2 * HEAD_DIM))
```

```python
import functools
import math

import jax
import jax.numpy as jnp
from jax import lax
from jax.experimental import pallas as pl
from jax.experimental.pallas import tpu as pltpu

F32 = jnp.float32
BF16 = jnp.bfloat16
I32 = jnp.int32

HEAD_DIM = 64
A_HEADS = 8
A_KV_HEADS = 2
IDX_HEADS = 8
IDX_DIM = 64
TOPK_MAX = 256
B_HEADS = 8
C_HEADS = 8
C_KV_HEADS = 4
ROPE_THETA = 10000.0
NORM_EPS = 1e-6
SUBLN_EPS = 1e-5

LANES = 128
KC = 2 * LANES
QB = KC
QB_DIFF = 256
PAGE_GROUP = 8
ROW_TILE = 256
MLP_ROW_TILE = 512
MASKED = -1e30
QK_SCALE = HEAD_DIM ** -0.5
VMEM_LIMIT = 56 * 1024 * 1024


def _cparams(*sem):
    return pltpu.CompilerParams(dimension_semantics=sem, vmem_limit_bytes=VMEM_LIMIT)


def _rms(x, g, eps):
    return x * lax.rsqrt(jnp.mean(x * x, axis=-1, keepdims=True) + eps) * g


def _dot(a, b):
    return jnp.dot(a, b, preferred_element_type=F32)


def _dot_nt(a, b):
    return lax.dot_general(a, b, (((1,), (1,)), ((), ())), preferred_element_type=F32)


def _rope_chunk(seg, cos, sin_signed, first_half):
    partner = jnp.where(first_half, pltpu.roll(seg, LANES - HEAD_DIM // 2, 1), pltpu.roll(seg, HEAD_DIM // 2, 1))
    return seg * cos + partner * sin_signed


def _rope_wide(y, cos, sin_signed, first_half):
    return jnp.concatenate([_rope_chunk(y[:, c * LANES:(c + 1) * LANES], cos, sin_signed, first_half)
                            for c in range(y.shape[1] // LANES)], axis=1)


def _first_half_mask(shape):
    lane = lax.broadcasted_iota(I32, shape, 1)
    return (lane & (HEAD_DIM - 1)) < HEAD_DIM // 2


def _split_heads(y, out_ref, scale=None):
    for h in range(y.shape[1] // HEAD_DIM):
        v = y[:, h * HEAD_DIM:(h + 1) * HEAD_DIM]
        out_ref[h] = (v if scale is None else v * scale).astype(out_ref.dtype)


def _split_heads_t(y, out_ref):
    yt = y.T
    for h in range(yt.shape[0] // HEAD_DIM):
        out_ref[h] = yt[h * HEAD_DIM:(h + 1) * HEAD_DIM, :].astype(out_ref.dtype)


_AB_AQ, _AB_IQ, _AB_AK, _AB_AV, _AB_BQ, _AB_BK, _AB_BV, _AB_TAIL, _AB_WIDTH = 0, 512, 1024, 1152, 1280, 1792, 2304, 2816, 2944
_A_KV_W = A_KV_HEADS * HEAD_DIM
_B_W = B_HEADS * HEAD_DIM


def _prep_w_ab(w):
    d = w.shape[0]
    sizes = (A_HEADS * HEAD_DIM, _A_KV_W, _A_KV_W, IDX_HEADS * IDX_DIM, IDX_DIM, IDX_HEADS, _B_W, _B_W, _B_W)
    offs = [0]
    for s in sizes:
        offs.append(offs[-1] + s)
    a_q, a_k, a_v, i_q, i_k, i_w, b_q, b_k, b_v = (w[:, offs[i]:offs[i + 1]] for i in range(9))
    pad = jnp.zeros((d, _AB_WIDTH - _AB_TAIL - IDX_DIM - IDX_HEADS), w.dtype)
    return jnp.concatenate([a_q, i_q, a_k, a_v, b_q, b_k, b_v, i_k, i_w, pad], axis=1).astype(BF16)


def _in_ab_body(x_ref, g_ref, w_ref, cos_ref, sin_ref,
                aq_o, iq_o, bq_o, ak_o, akt_o, av_o, av16_o, ik_o, ikt_o, iw_o, bk_o, bkt_o, bv_o, bv16_o):
    xn = _rms(x_ref[...], g_ref[...], NORM_EPS).astype(BF16)
    cos = cos_ref[...]
    sin = sin_ref[...]
    first = _first_half_mask(cos.shape)

    def proj(lo, width):
        return _dot(xn, w_ref[:, lo:lo + width])

    _split_heads(_rope_wide(proj(_AB_AQ, 512), cos, sin, first), aq_o, QK_SCALE)
    _split_heads(_rope_wide(proj(_AB_IQ, 512), cos, sin, first), iq_o, QK_SCALE)
    _split_heads(proj(_AB_BQ, 512), bq_o, QK_SCALE)

    ak = _rope_chunk(proj(_AB_AK, _A_KV_W), cos, sin, first)
    ak_o[...] = ak
    _split_heads_t(ak, akt_o)
    av = proj(_AB_AV, _A_KV_W)
    av_o[...] = av
    av16_o[...] = av.astype(BF16)

    bk = proj(_AB_BK, _B_W)
    bk_o[...] = bk
    _split_heads_t(bk, bkt_o)
    bv = proj(_AB_BV, _B_W)
    bv_o[...] = bv
    bv16_o[...] = bv.astype(BF16)

    tail = proj(_AB_TAIL, LANES)
    ik = _rope_chunk(tail, cos, sin, first)
    ik_o[...] = ik[:, :IDX_DIM]
    ikt_o[...] = ik.T[:IDX_DIM, :].astype(BF16)
    iw_o[...] = tail[:, IDX_DIM:IDX_DIM + IDX_HEADS] * (IDX_HEADS ** -0.5)


def _in_proj_ab(x2d, g, w_perm, cos, sin):
    m, d = x2d.shape
    tm = min(ROW_TILE, m)
    n_pos_blocks = cos.shape[0] // tm
    row = lambda i: (i, 0)
    hrow = lambda i: (0, i, 0)
    hcol = lambda i: (0, 0, i)
    pos = lambda i: (i % n_pos_blocks, 0)
    const = lambda i: (0, 0)
    sds = jax.ShapeDtypeStruct
    hm = lambda heads: sds((heads, m, HEAD_DIM), BF16)
    hmt = lambda heads: sds((heads, HEAD_DIM, m), BF16)
    out_shape = (
        hm(A_HEADS), hm(IDX_HEADS), hm(B_HEADS),
        sds((m, _A_KV_W), F32), hmt(A_KV_HEADS), sds((m, _A_KV_W), F32), sds((m, _A_KV_W), BF16),
        sds((m, IDX_DIM), F32), sds((IDX_DIM, m), BF16), sds((m, IDX_HEADS), F32),
        sds((m, _B_W), F32), hmt(B_HEADS), sds((m, _B_W), F32), sds((m, _B_W), BF16),
    )
    hspec = lambda heads: pl.BlockSpec((heads, tm, HEAD_DIM), hrow)
    htspec = lambda heads: pl.BlockSpec((heads, HEAD_DIM, tm), hcol)
    rspec = lambda width: pl.BlockSpec((tm, width), row)
    out_specs = (
        hspec(A_HEADS), hspec(IDX_HEADS), hspec(B_HEADS),
        rspec(_A_KV_W), htspec(A_KV_HEADS), rspec(_A_KV_W), rspec(_A_KV_W),
        rspec(IDX_DIM), pl.BlockSpec((IDX_DIM, tm), lambda i: (0, i)), rspec(IDX_HEADS),
        rspec(_B_W), htspec(B_HEADS), rspec(_B_W), rspec(_B_W),
    )
    return pl.pallas_call(
        _in_ab_body,
        grid=(m // tm,),
        in_specs=[pl.BlockSpec((tm, d), row), pl.BlockSpec((1, d), const), pl.BlockSpec((d, _AB_WIDTH), const),
                  pl.BlockSpec((tm, LANES), pos), pl.BlockSpec((tm, LANES), pos)],
        out_specs=out_specs,
        out_shape=out_shape,
        compiler_params=_cparams("parallel"),
        name="in_proj_ab",
    )(x2d, g.reshape(1, d), w_perm, cos, sin)


_C_Q_W = C_HEADS * 2 * HEAD_DIM
_C_KV_W = C_KV_HEADS * 2 * HEAD_DIM


def _in_c_body(x_ref, g_ref, w_ref, cos_ref, sin_ref, cq_o, ck_o, ckt_o, cv_o, cv16_o):
    xn = _rms(x_ref[...], g_ref[...], NORM_EPS).astype(BF16)
    cos = cos_ref[...]
    sin = sin_ref[...]
    first = _first_half_mask(cos.shape)

    def proj(lo, width):
        return _dot(xn, w_ref[:, lo:lo + width])

    rep = C_HEADS // C_KV_HEADS
    for block in range(2):
        q = _rope_wide(proj(block * 512, 512), cos, sin, first)
        for i in range(8):
            head, half = divmod(block * 8 + i, 2)
            slot = ((head // rep) * 2 + half) * rep + head % rep
            cq_o[slot] = (q[:, i * HEAD_DIM:(i + 1) * HEAD_DIM] * QK_SCALE).astype(BF16)
    k = _rope_wide(proj(_C_Q_W, _C_KV_W), cos, sin, first)
    ck_o[...] = k
    _split_heads_t(k, ckt_o)
    v = proj(_C_Q_W + _C_KV_W, _C_KV_W)
    cv_o[...] = v
    cv16_o[...] = v.astype(BF16)


def _in_proj_c(x2d, g, w_bf16, cos, sin):
    m, d = x2d.shape
    tm = min(ROW_TILE, m)
    n_pos_blocks = cos.shape[0] // tm
    row = lambda i: (i, 0)
    hrow = lambda i: (0, i, 0)
    hcol = lambda i: (0, 0, i)
    pos = lambda i: (i % n_pos_blocks, 0)
    const = lambda i: (0, 0)
    width = _C_Q_W + 2 * _C_KV_W
    sds = jax.ShapeDtypeStruct
    out_shape = (
        sds((2 * C_HEADS, m, HEAD_DIM), BF16),
        sds((m, _C_KV_W), F32), sds((2 * C_KV_HEADS, HEAD_DIM, m), BF16),
        sds((m, _C_KV_W), F32), sds((m, _C_KV_W), BF16),
    )
    out_specs = (
        pl.BlockSpec((2 * C_HEADS, tm, HEAD_DIM), hrow),
        pl.BlockSpec((tm, _C_KV_W), row), pl.BlockSpec((2 * C_KV_HEADS, HEAD_DIM, tm), hcol),
        pl.BlockSpec((tm, _C_KV_W), row), pl.BlockSpec((tm, _C_KV_W), row),
    )
    return pl.pallas_call(
        _in_c_body,
        grid=(m // tm,),
        in_specs=[pl.BlockSpec((tm, d), row), pl.BlockSpec((1, d), const), pl.BlockSpec((d, width), const),
                  pl.BlockSpec((tm, LANES), pos), pl.BlockSpec((tm, LANES), pos)],
        out_specs=out_specs,
        out_shape=out_shape,
        compiler_params=_cparams("parallel"),
        name="in_proj_c",
    )(x2d, g.reshape(1, d), w_bf16, cos, sin)


def _post_body(o_ref, x_ref, wout_ref, gpost_ref, gpre_ref, wup_ref, wdown_ref, gfpost_ref, y_ref):
    mix = _dot(o_ref[...], wout_ref[...])
    x1 = x_ref[...] + _rms(mix, gpost_ref[...], NORM_EPS)
    hn = _rms(x1, gpre_ref[...], NORM_EPS).astype(BF16)
    d_ff = wup_ref.shape[1]
    step = 1024
    f = jnp.zeros(x1.shape, F32)
    for c in range(d_ff // step):
        u = jnp.maximum(_dot(hn, wup_ref[:, c * step:(c + 1) * step]), 0.0)
        f = f + _dot((u * u).astype(BF16), wdown_ref[c * step:(c + 1) * step, :])
    y_ref[...] = x1 + _rms(f, gfpost_ref[...], NORM_EPS)


def _post_block(o2d, x2d, w_out, g_post, g_pre, w_up, w_down, g_fpost):
    m, d = x2d.shape
    tm = min(MLP_ROW_TILE if m % MLP_ROW_TILE == 0 else ROW_TILE, m)
    assert m % tm == 0
    d_ff = w_up.shape[1]
    row = lambda i: (i, 0)
    const = lambda i: (0, 0)
    resident = lambda shape: pl.BlockSpec(shape, const, pipeline_mode=pl.Buffered(1))
    return pl.pallas_call(
        _post_body,
        grid=(m // tm,),
        in_specs=[pl.BlockSpec((tm, o2d.shape[1]), row), pl.BlockSpec((tm, d), row),
                  resident(w_out.shape), pl.BlockSpec((1, d), const), pl.BlockSpec((1, d), const),
                  resident((d, d_ff)), resident((d_ff, d)), pl.BlockSpec((1, d), const)],
        out_specs=pl.BlockSpec((tm, d), row),
        out_shape=jax.ShapeDtypeStruct((m, d), F32),
        compiler_params=_cparams("parallel"),
        name="post_block",
    )(o2d, x2d, w_out, g_post.reshape(1, d), g_pre.reshape(1, d), w_up, w_down, g_fpost.reshape(1, d))


def _order_key(s):
    s = jnp.where(s == 0.0, 0.0, s)
    b = pltpu.bitcast(s, I32)
    return b ^ ((b >> 31) & I32(0x7FFFFFFF))


def _select_bias(sc_ref, key_ref, nch, n_sel, rows, width):
    def fill_keys(j, carry):
        s = sc_ref[j]
        key_ref[j] = _order_key(s)
        return jnp.maximum(carry[0], s), jnp.minimum(carry[1], s)

    lane_max, lane_min = lax.fori_loop(0, nch, fill_keys, (jnp.full((rows, width), -jnp.inf, F32),
                                                           jnp.full((rows, width), jnp.inf, F32)))

    def count(pred):
        def body(j, c):
            hit = jnp.where(pred(key_ref[j]), 1.0, 0.0)
            return c + sum(hit[:, i * LANES:(i + 1) * LANES] for i in range(width // LANES))

        c = lax.fori_loop(0, nch, body, jnp.zeros((rows, LANES), F32))
        return jnp.sum(c, axis=1, keepdims=True)

    def count_ge(cand):
        cand_b = jnp.broadcast_to(cand, (rows, width))
        return count(lambda k: k >= cand_b)

    k_sel = float(n_sel)
    low_score = jnp.min(lane_max if n_sel <= width else lane_min, axis=1, keepdims=True)
    lo = _order_key(low_score)
    hi = _order_key(jnp.max(lane_max, axis=1, keepdims=True))
    lo_b = jnp.broadcast_to(lo, (rows, width))
    hi = jnp.where(count(lambda k: k > lo_b) < k_sel, lo, hi)

    def open_rows(lo, hi):
        return jnp.max(jnp.where(lo < hi, 1.0, 0.0))

    def bisect(carry):
        lo, hi, _ = carry
        mid = (lo | hi) - ((lo ^ hi) >> 1)
        n_ge = count_ge(mid)
        lo = jnp.where(n_ge >= k_sel, mid, lo)
        hi = jnp.where(n_ge > k_sel, hi, jnp.where(n_ge == k_sel, mid, mid - 1))
        return lo, hi, open_rows(lo, hi)

    thr, _, _ = lax.while_loop(lambda carry: carry[2] > 0.5, bisect, (lo, hi, open_rows(lo, hi)))
    thr_b = jnp.broadcast_to(thr, (rows, width))
    need_b = jnp.broadcast_to(k_sel - count(lambda k: k > thr_b), (rows, width))
    r = lax.broadcasted_iota(I32, (width, width), 0)
    c = lax.broadcasted_iota(I32, (width, width), 1)
    upper = jnp.where(r < c, 1.0, 0.0).astype(BF16)
    ones = jnp.ones((width, width), BF16)

    def tie_body(j, run):
        k = key_ref[j]
        eq = jnp.where(k == thr_b, 1.0, 0.0)
        eq16 = eq.astype(BF16)
        rank = run + _dot(eq16, upper)
        bias = jnp.where(k > thr_b, 0.0, jnp.where(eq * jnp.where(rank < need_b, 1.0, 0.0) > 0.5, 0.0, MASKED))
        sc_ref[j] = jnp.where(sc_ref[j] == -jnp.inf, MASKED, bias)
        return run + _dot(eq16, ones)

    lax.fori_loop(0, nch, tie_body, jnp.zeros((rows, width), F32))


def _weighted_values(p16, v16, v_transposed):
    return _dot_nt(p16, v16) if v_transposed else _dot(p16, v16)


def _softmax_step(s, values, m, l, acc, v_transposed=False):
    m_new = jnp.maximum(m, jnp.max(s, axis=1, keepdims=True))
    alpha = jnp.exp(m - m_new)
    p = jnp.exp(s - m_new)
    l = alpha * l + jnp.sum(p, axis=1, keepdims=True)
    p16 = p.astype(BF16)
    if not isinstance(values, (list, tuple)):
        pv = _weighted_values(p16, values, v_transposed)
    else:
        pv = sum(_weighted_values(p16[:, i * LANES:(i + 1) * LANES], v, v_transposed) for i, v in enumerate(values))
    return m_new, l, alpha * acc + pv


def _stick_matrices():
    r = lax.broadcasted_iota(I32, (2 * LANES, 2 * LANES), 0) & (LANES - 1)
    c = lax.broadcasted_iota(I32, (2 * LANES, 2 * LANES), 1)
    return jnp.where((c >= LANES) | (r > c), 1.0, 0.0).astype(BF16)


def _softmax_ones_step(s, v_ext16, m, la):
    tiles = s.shape[1] // LANES
    m_new = jnp.maximum(m, jnp.max(s.astype(BF16), axis=1, keepdims=True).astype(F32))
    alpha = jnp.exp(m - m_new)
    p16 = jnp.exp(s - jnp.concatenate([m_new] * tiles, axis=1)).astype(BF16)
    return m_new, jnp.concatenate([alpha] * (la.shape[1] // LANES), axis=1) * la + _dot(p16, v_ext16)


def _stick_weights(zs, valid, run, mats):
    rows = zs[0].shape[0]
    softplus = [jnp.maximum(z, 0.0) + jnp.log(1.0 + jnp.exp(-jnp.abs(z))) for z in zs]
    split = []
    for sp in softplus:
        log_stay = -sp if valid is None else jnp.where(valid, -sp, 0.0)
        hi = log_stay.astype(BF16)
        lo = (log_stay - hi.astype(F32)).astype(BF16)
        split.append(jnp.concatenate([hi, lo], axis=1))
    sums = _dot(split[0] if len(split) == 1 else jnp.concatenate(split, axis=0), mats)
    ws = []
    for n, (z, sp) in enumerate(zip(zs, softplus)):
        part = sums[n * rows:(n + 1) * rows]
        w = jnp.exp((z - sp) + part[:, :LANES] + run)
        if valid is not None:
            w = jnp.where(valid, w, 0.0)
        ws.append(w.astype(BF16))
        run = run + part[:, LANES:]
    return run, ws


def _chunk_count(qi, qb=QB):
    return lax.shift_right_logical((qi + 1) * qb - 1, KC.bit_length() - 1) + 1


def _chunk_lanes(j):
    return pl.ds(pl.multiple_of(j * KC, KC), KC)


def _att0_prompt_body(n_sel, aq_ref, iq_ref, iw_ref, bq_ref, akt_ref, av_ref, ikt_ref, bkt_ref, bv_ref,
                      o_ref, sc_ref, key_ref, m_ref, la_ref, run_ref, acc_ref):
    qi = pl.program_id(1)
    n_kc = _chunk_count(qi)
    row = lax.broadcasted_iota(I32, (QB, KC), 0)
    col = lax.broadcasted_iota(I32, (QB, KC), 1)
    qpos = qi * QB + row

    def score_chunk(j, c):
        kt = ikt_ref[:, _chunk_lanes(j)]
        acc = jnp.zeros((QB, KC), F32)
        for h in range(IDX_HEADS):
            acc = acc + jnp.maximum(_dot(iq_ref[h], kt), 0.0) * iw_ref[:, h:h + 1]
        sc_ref[j] = jnp.where(j * KC + col <= qpos, acc, -jnp.inf)
        return c

    lax.fori_loop(0, n_kc, score_chunk, 0)
    _select_bias(sc_ref, key_ref, n_kc, n_sel, QB, KC)

    m_ref[...] = jnp.full_like(m_ref, MASKED)
    la_ref[...] = jnp.zeros_like(la_ref)
    rep = A_HEADS // A_KV_HEADS
    ones = jnp.ones((KC, LANES), BF16)

    def dsa_chunk(j, c):
        bias = jnp.concatenate([sc_ref[j]] * rep, axis=0)
        v_ext = jnp.concatenate([av_ref[_chunk_lanes(j), :], ones], axis=1)
        for g in range(A_KV_HEADS):
            q = aq_ref[rep * g:rep * (g + 1)].reshape(rep * QB, HEAD_DIM)
            s = _dot(q, akt_ref[g, :, _chunk_lanes(j)]) + bias
            m_ref[g], la_ref[g] = _softmax_ones_step(s, v_ext, m_ref[g], la_ref[g])
        return c

    lax.fori_loop(0, n_kc, dsa_chunk, 0)
    outs = []
    for h in range(A_HEADS):
        g, r = h // rep, h % rep
        la = la_ref[g][r * QB:(r + 1) * QB]
        outs.append(la[:, g * HEAD_DIM:(g + 1) * HEAD_DIM] / la[:, LANES:LANES + HEAD_DIM])
    o_ref[:, :A_HEADS * HEAD_DIM] = jnp.concatenate(outs, axis=1).astype(o_ref.dtype)

    run_ref[...] = jnp.zeros_like(run_ref)
    acc_ref[...] = jnp.zeros_like(acc_ref)
    mats = _stick_matrices()
    rows_all = B_HEADS * QB
    qpos_c = qi * QB + (lax.broadcasted_iota(I32, (rows_all, LANES), 0) & (QB - 1))
    col_c = lax.broadcasted_iota(I32, (rows_all, LANES), 1)

    def sb_keys(starts, on_diagonal):
        keys = [pl.ds(pl.multiple_of(start, LANES), LANES) for start in starts]
        zs = [jnp.concatenate([_dot(bq_ref[h], bkt_ref[h, :, k]) for h in range(B_HEADS)], axis=0) for k in keys]
        valid = (starts[0] + col_c < qpos_c) if on_diagonal else None
        run_ref[...], ws = _stick_weights(zs, valid, run_ref[...], mats)
        for pair in range(B_HEADS // 2):
            rows = slice(2 * pair * QB, 2 * (pair + 1) * QB)
            acc_ref[pair] += sum(_dot(w16[rows], bv_ref[k, pair * LANES:(pair + 1) * LANES]) for w16, k in zip(ws, keys))

    n_full = qi * (QB // KC)
    for sub in reversed(range(QB // LANES)):
        sb_keys([qi * QB + sub * LANES], True)

    def sb_chunk(i, c):
        j = n_full - 1 - i
        sb_keys([j * KC + sub * LANES for sub in reversed(range(KC // LANES))], False)
        return c

    lax.fori_loop(0, n_full, sb_chunk, 0)
    outs = [acc_ref[h // 2][(h % 2) * QB:(h % 2 + 1) * QB, (h % 2) * HEAD_DIM:(h % 2 + 1) * HEAD_DIM]
            for h in range(B_HEADS)]
    o_ref[:, A_HEADS * HEAD_DIM:] = jnp.concatenate(outs, axis=1).astype(o_ref.dtype)


def _att0_prompt(aq, iq, iw, bq, akt, av16, ikt, bkt, bv16, bsz, t):
    nq = t // QB
    n_sel = min(TOPK_MAX, t // 4)
    qh = lambda b, q: (0, b * nq + q, 0)
    qrow = lambda b, q: (b * nq + q, 0)
    kt = lambda b, q: (0, 0, b)
    krow = lambda b, q: (b, 0)
    width = (A_HEADS + B_HEADS) * HEAD_DIM
    rows_a = (A_HEADS // A_KV_HEADS) * QB
    return pl.pallas_call(
        functools.partial(_att0_prompt_body, n_sel),
        grid=(bsz, nq),
        in_specs=[pl.BlockSpec((A_HEADS, QB, HEAD_DIM), qh), pl.BlockSpec((IDX_HEADS, QB, IDX_DIM), qh),
                  pl.BlockSpec((QB, IDX_HEADS), qrow), pl.BlockSpec((B_HEADS, QB, HEAD_DIM), qh),
                  pl.BlockSpec((A_KV_HEADS, HEAD_DIM, t), kt), pl.BlockSpec((t, _A_KV_W), krow),
                  pl.BlockSpec((IDX_DIM, t), lambda b, q: (0, b)),
                  pl.BlockSpec((B_HEADS, HEAD_DIM, t), kt), pl.BlockSpec((t, _B_W), krow)],
        out_specs=pl.BlockSpec((QB, width), qrow),
        out_shape=jax.ShapeDtypeStruct((bsz * t, width), BF16),
        scratch_shapes=[pltpu.VMEM((t // KC, QB, KC), F32), pltpu.VMEM((t // KC, QB, KC), I32),
                        pltpu.VMEM((A_KV_HEADS, rows_a, LANES), F32), pltpu.VMEM((A_KV_HEADS, rows_a, 2 * LANES), F32),
                        pltpu.VMEM((B_HEADS * QB, LANES), F32), pltpu.VMEM((B_HEADS // 2, 2 * QB, LANES), F32)],
        compiler_params=_cparams("parallel", "arbitrary"),
        name="att0_prompt",
    )(aq, iq, iw, bq, akt, av16, ikt, bkt, bv16)


def _diff_lambda(lq1_ref, lk1_ref, lq2_ref, lk2_ref, lambda_init):
    s1 = jnp.sum(lq1_ref[...] * lk1_ref[...], axis=1, keepdims=True)
    s2 = jnp.sum(lq2_ref[...] * lk2_ref[...], axis=1, keepdims=True)
    return jnp.exp(s1) - jnp.exp(s2) + lambda_init


def _diff_finish(a1, l1, a2, l2, lam, gsub, lambda_init):
    o = a1 / l1 - lam * (a2 / l2)
    return _rms(o, gsub, SUBLN_EPS) * (1.0 - lambda_init)


def _att1_prompt_body(lambda_init, cq_ref, ckt_ref, cv_ref, lq1_ref, lk1_ref, lq2_ref, lk2_ref, gsub_ref,
                      o_ref, m_ref, la_ref):
    qi = pl.program_id(1)
    qb = o_ref.shape[0]
    rep = C_HEADS // C_KV_HEADS
    rows = rep * qb
    n_kc = _chunk_count(qi, qb)
    row = lax.broadcasted_iota(I32, (rows, KC), 0)
    col = lax.broadcasted_iota(I32, (rows, KC), 1)
    qpos = qi * qb + (row & (qb - 1))
    m_ref[...] = jnp.full_like(m_ref, MASKED)
    la_ref[...] = jnp.zeros_like(la_ref)
    ones = jnp.ones((KC, LANES), BF16)

    def chunk(j, on_diagonal):
        bias = jnp.where(j * KC + col <= qpos, 0.0, MASKED) if on_diagonal else None
        for g in range(C_KV_HEADS):
            v_ext = jnp.concatenate([cv_ref[_chunk_lanes(j), g * LANES:(g + 1) * LANES], ones], axis=1)
            for half in range(2):
                u = 2 * g + half
                q = cq_ref[rep * u:rep * (u + 1)].reshape(rows, HEAD_DIM)
                s = _dot(q, ckt_ref[u, :, _chunk_lanes(j)])
                if on_diagonal:
                    s = s + bias
                m_ref[u], la_ref[u] = _softmax_ones_step(s, v_ext, m_ref[u], la_ref[u])

    def full_chunk(j, c):
        chunk(j, False)
        return c

    lax.fori_loop(0, n_kc - 1, full_chunk, 0)
    chunk(n_kc - 1, True)

    lam = _diff_lambda(lq1_ref, lk1_ref, lq2_ref, lk2_ref, lambda_init)
    gsub = gsub_ref[...]
    for h in range(C_HEADS):
        g, r = h // rep, h % rep
        la1 = la_ref[2 * g][r * qb:(r + 1) * qb]
        la2 = la_ref[2 * g + 1][r * qb:(r + 1) * qb]
        o = _diff_finish(la1[:, :LANES], la1[:, LANES:], la2[:, :LANES], la2[:, LANES:], lam, gsub, lambda_init)
        o_ref[:, h * LANES:(h + 1) * LANES] = o.astype(o_ref.dtype)


def _att1_prompt(cq, ckt, cv16, lq1, lk1, lq2, lk2, gsub, lambda_init, bsz, t):
    qb = QB_DIFF
    nq = t // qb
    qh = lambda b, q: (0, b * nq + q, 0)
    qrow = lambda b, q: (b * nq + q, 0)
    const = lambda b, q: (0, 0)
    vec = lambda a: a.reshape(1, -1)
    rows = (C_HEADS // C_KV_HEADS) * qb
    return pl.pallas_call(
        functools.partial(_att1_prompt_body, lambda_init),
        grid=(bsz, nq),
        in_specs=[pl.BlockSpec((2 * C_HEADS, qb, HEAD_DIM), qh),
                  pl.BlockSpec((2 * C_KV_HEADS, HEAD_DIM, t), lambda b, q: (0, 0, b)),
                  pl.BlockSpec((t, _C_KV_W), lambda b, q: (b, 0)),
                  pl.BlockSpec((1, HEAD_DIM), const), pl.BlockSpec((1, HEAD_DIM), const),
                  pl.BlockSpec((1, HEAD_DIM), const), pl.BlockSpec((1, HEAD_DIM), const),
                  pl.BlockSpec((1, LANES), const)],
        out_specs=pl.BlockSpec((qb, _C_Q_W), qrow),
        out_shape=jax.ShapeDtypeStruct((bsz * t, _C_Q_W), BF16),
        scratch_shapes=[pltpu.VMEM((2 * C_KV_HEADS, rows, LANES), F32),
                        pltpu.VMEM((2 * C_KV_HEADS, rows, 2 * LANES), F32)],
        compiler_params=_cparams("parallel", "arbitrary"),
        name="att1_prompt",
    )(cq, ckt, cv16, vec(lq1), vec(lk1), vec(lq2), vec(lk2), vec(gsub))


def _page_specs(block, n_groups, group, newest_first):
    zeros = (0,) * (len(block) - 2)

    def index(i):
        if newest_first:
            return lambda b, s, pt: (0, pt[b, jnp.minimum(n_groups - s, n_groups - 1) * group + i]) + zeros
        return lambda b, s, pt: (0, pt[b, jnp.minimum(s, n_groups - 1) * group + i]) + zeros

    return [pl.BlockSpec(block, index(i)) for i in range(group)]


def _keys_on_lanes(cache):
    nd = cache.ndim
    return jnp.transpose(cache, (0, 1) + tuple(range(3, nd)) + (2,))


def _att0_sample_walk_body(dec, group, pt_ref, iq_ref, iw_ref, bq_ref, ikn_ref, bkn_ref, bvn_ref, *refs):
    del pt_ref
    ikc_refs, bkc_refs, bvc_refs = refs[:group], refs[group:2 * group], refs[2 * group:3 * group]
    sc_o, ob_o, run_ref, acc_ref = refs[3 * group:]
    s = pl.program_id(1)
    rows = B_HEADS * dec
    row = lax.broadcasted_iota(I32, (rows, LANES), 0)
    col = lax.broadcasted_iota(I32, (rows, LANES), 1)
    t_of_row = row & (dec - 1)
    mats = _stick_matrices()

    def scores(kt16):
        lg = jnp.maximum(_dot(iq_ref[...], kt16), 0.0) * iw_ref[...]
        out = lg[0:dec]
        for h in range(1, IDX_HEADS):
            out = out + lg[h * dec:(h + 1) * dec]
        return out

    def stick(kts, vts, valid):
        run_ref[...], ws = _stick_weights([_dot(bq_ref[...], kt) for kt in kts], valid, run_ref[...], mats)
        acc_ref[...] += sum(_dot_nt(w16, vt) for w16, vt in zip(ws, vts))

    @pl.when(s == 0)
    def _():
        run_ref[...] = jnp.zeros_like(run_ref)
        acc_ref[...] = jnp.zeros_like(acc_ref)
        causal = (col[:dec] < dec) & (col[:dec] <= row[:dec])
        sc_o[0] = jnp.where(causal, scores(ikn_ref[...]), -jnp.inf)
        for i in range(1, group):
            sc_o[i] = jnp.full((dec, LANES), -jnp.inf, F32)
        stick([bkn_ref[...]], [bvn_ref[...]], (col < dec) & (col < t_of_row))

    @pl.when(s > 0)
    def _():
        for i in range(group):
            sc_o[i] = scores(ikc_refs[i][...].astype(BF16))
        stick([bkc_refs[i][...].reshape(_B_W, LANES).astype(BF16) for i in reversed(range(group))],
              [bvc_refs[i][...].reshape(_B_W, LANES).astype(BF16) for i in reversed(range(group))], None)

    @pl.when(s == pl.num_programs(1) - 1)
    def _():
        acc = acc_ref[...]
        ob_o[...] = jnp.concatenate([acc[h * dec:(h + 1) * dec, h * HEAD_DIM:(h + 1) * HEAD_DIM]
                                     for h in range(B_HEADS)], axis=1).astype(ob_o.dtype)


def _att0_sample_walk(page_table, iq_st, iw_st, bq_bd, ikt_new, bkt_new, bvt_new, cache_kidx, cache_bk, cache_bv):
    db, n_pages = page_table.shape
    group = math.gcd(PAGE_GROUP, n_pages)
    n_groups = n_pages // group
    dec = iq_st.shape[1] // IDX_HEADS
    rows = B_HEADS * dec
    per_b = lambda b, s, pt: (b, 0, 0)
    chunks_of_step = lambda b, s, pt: (jnp.where(s == 0, n_groups, n_groups - s), b, 0, 0)
    b_page = (None, None, B_HEADS, HEAD_DIM, LANES)
    grid_spec = pltpu.PrefetchScalarGridSpec(
        num_scalar_prefetch=1,
        grid=(db, n_groups + 1),
        in_specs=[pl.BlockSpec((None, rows, IDX_DIM), per_b), pl.BlockSpec((None, rows, 1), per_b),
                  pl.BlockSpec((None, rows, _B_W), per_b),
                  pl.BlockSpec((None, IDX_DIM, LANES), per_b), pl.BlockSpec((None, _B_W, LANES), per_b),
                  pl.BlockSpec((None, _B_W, LANES), per_b)]
        + _page_specs((None, None, IDX_DIM, LANES), n_groups, group, True)
        + _page_specs(b_page, n_groups, group, True) + _page_specs(b_page, n_groups, group, True),
        out_specs=[pl.BlockSpec((group, None, dec, LANES), chunks_of_step), pl.BlockSpec((None, dec, _B_W), per_b)],
        scratch_shapes=[pltpu.VMEM((rows, LANES), F32), pltpu.VMEM((rows, _B_W), F32)],
    )
    return pl.pallas_call(
        functools.partial(_att0_sample_walk_body, dec, group),
        grid_spec=grid_spec,
        out_shape=(jax.ShapeDtypeStruct(((n_groups + 1) * group, db, dec, LANES), F32),
                   jax.ShapeDtypeStruct((db, dec, _B_W), BF16)),
        compiler_params=_cparams("parallel", "arbitrary"),
        name="att0_sample_walk",
    )(page_table, iq_st, iw_st, bq_bd, ikt_new, bkt_new, bvt_new,
      *([cache_kidx] * group + [cache_bk] * group + [cache_bv] * group))


def _select_sample_body(n_sel, nch, sc_ref, bias_o, key_ref):
    rows = sc_ref.shape[1]
    bias_o[...] = sc_ref[...]
    _select_bias(bias_o, key_ref, nch, n_sel, rows, LANES)


def _select_sample(scores, nch, n_sel):
    n_alloc, m, _ = scores.shape
    rows = min(m, 64)
    spec = pl.BlockSpec((n_alloc, rows, LANES), lambda i: (0, i, 0))
    return pl.pallas_call(
        functools.partial(_select_sample_body, n_sel, nch),
        grid=(m // rows,),
        in_specs=[spec],
        out_specs=spec,
        out_shape=jax.ShapeDtypeStruct(scores.shape, F32),
        scratch_shapes=[pltpu.VMEM((n_alloc, rows, LANES), I32)],
        compiler_params=_cparams("parallel"),
        name="select_sample",
    )(scores)


def _att0_sample_dsa_body(n_groups, dec, group, pt_ref, aq_ref, bias_ref, akn_ref, avn_ref, *refs):
    del pt_ref
    akc_refs, avc_refs = refs[:group], refs[group:2 * group]
    oa_o, m_ref, l_ref, acc_ref = refs[2 * group:]
    s = pl.program_id(1)

    @pl.when(s == 0)
    def _():
        m_ref[...] = jnp.full_like(m_ref, MASKED)
        l_ref[...] = jnp.zeros_like(l_ref)
        acc_ref[...] = jnp.zeros_like(acc_ref)

    def step(kts, vts):
        n = len(kts)
        bias = jnp.concatenate([jnp.concatenate([bias_ref[i]] * A_HEADS, axis=0) for i in range(n)], axis=1)
        sc = jnp.concatenate([_dot(aq_ref[...], kt) for kt in kts], axis=1) + bias
        m_ref[...], l_ref[...], acc_ref[...] = _softmax_step(sc, vts, m_ref[...], l_ref[...], acc_ref[...],
                                                             v_transposed=True)

    @pl.when(s < n_groups)
    def _():
        step([r[...].reshape(_A_KV_W, LANES).astype(BF16) for r in akc_refs],
             [r[...].reshape(_A_KV_W, LANES).astype(BF16) for r in avc_refs])

    @pl.when(s == n_groups)
    def _():
        step([akn_ref[...]], [avn_ref[...]])
        o = acc_ref[...] / l_ref[...]
        rep = A_HEADS // A_KV_HEADS
        oa_o[...] = jnp.concatenate(
            [o[h * dec:(h + 1) * dec, (h // rep) * HEAD_DIM:(h // rep + 1) * HEAD_DIM] for h in range(A_HEADS)],
            axis=1).astype(oa_o.dtype)


def _att0_sample_dsa(page_table, aq_bd, bias, akt_new, avt_new, cache_akt, cache_avt):
    db, n_pages = page_table.shape
    group = math.gcd(PAGE_GROUP, n_pages)
    n_groups = n_pages // group
    rows = aq_bd.shape[1]
    dec = rows // A_HEADS
    per_b = lambda b, s, pt: (b, 0, 0)
    a_page = (None, None, A_KV_HEADS, HEAD_DIM, LANES)
    grid_spec = pltpu.PrefetchScalarGridSpec(
        num_scalar_prefetch=1,
        grid=(db, n_groups + 1),
        in_specs=[pl.BlockSpec((None, rows, _A_KV_W), per_b),
                  pl.BlockSpec((group, None, dec, LANES), lambda b, s, pt: (s, b, 0, 0)),
                  pl.BlockSpec((None, _A_KV_W, LANES), per_b), pl.BlockSpec((None, _A_KV_W, LANES), per_b)]
        + _page_specs(a_page, n_groups, group, False) + _page_specs(a_page, n_groups, group, False),
        out_specs=pl.BlockSpec((None, dec, A_HEADS * HEAD_DIM), per_b),
        scratch_shapes=[pltpu.VMEM((rows, 1), F32), pltpu.VMEM((rows, 1), F32), pltpu.VMEM((rows, _A_KV_W), F32)],
    )
    return pl.pallas_call(
        functools.partial(_att0_sample_dsa_body, n_groups, dec, group),
        grid_spec=grid_spec,
        out_shape=jax.ShapeDtypeStruct((db, dec, A_HEADS * HEAD_DIM), BF16),
        compiler_params=_cparams("parallel", "arbitrary"),
        name="att0_sample_dsa",
    )(page_table, aq_bd, bias, akt_new, avt_new, *([cache_akt] * group + [cache_avt] * group))


def _att1_sample_body(n_groups, dec, group, lambda_init, pt_ref, cq_ref, ckn_ref, cvn_ref, *refs):
    del pt_ref
    ckc_refs, cvc_refs = refs[:group], refs[group:2 * group]
    lq1_ref, lk1_ref, lq2_ref, lk2_ref, gsub_ref, o_o, m_ref, l_ref, acc_ref = refs[2 * group:]
    s = pl.program_id(1)
    rows_g = 2 * (C_HEADS // C_KV_HEADS) * dec

    @pl.when(s == 0)
    def _():
        m_ref[...] = jnp.full_like(m_ref, MASKED)
        l_ref[...] = jnp.zeros_like(l_ref)
        acc_ref[...] = jnp.zeros_like(acc_ref)

    def step(keys_of, values_of, n_pages, bias):
        sc = jnp.concatenate(
            [jnp.concatenate([_dot_nt(cq_ref[g * rows_g:(g + 1) * rows_g, :], keys_of(g, i)) for i in range(n_pages)],
                             axis=1) for g in range(C_KV_HEADS)], axis=0)
        if bias is not None:
            sc = sc + bias
        m = m_ref[...]
        m_new = jnp.maximum(m, jnp.max(sc, axis=1, keepdims=True))
        alpha = jnp.exp(m - m_new)
        p = jnp.exp(sc - m_new)
        p16 = p.astype(BF16)
        pv = jnp.concatenate(
            [sum(_dot(p16[g * rows_g:(g + 1) * rows_g, i * LANES:(i + 1) * LANES], values_of(g, i))
                 for i in range(n_pages)) for g in range(C_KV_HEADS)], axis=0)
        m_ref[...] = m_new
        l_ref[...] = alpha * l_ref[...] + jnp.sum(p, axis=1, keepdims=True)
        acc_ref[...] = alpha * acc_ref[...] + pv

    @pl.when(s < n_groups)
    def _():
        head_rows = [pl.ds(g, LANES, stride=C_KV_HEADS) for g in range(C_KV_HEADS)]
        step(lambda g, i: ckc_refs[i][head_rows[g], :].astype(BF16),
             lambda g, i: cvc_refs[i][head_rows[g], :].astype(BF16), group, None)

    @pl.when(s == n_groups)
    def _():
        rows = C_KV_HEADS * rows_g
        row = lax.broadcasted_iota(I32, (rows, LANES), 0)
        col = lax.broadcasted_iota(I32, (rows, LANES), 1)
        bias = jnp.where(col <= (row & (dec - 1)), 0.0, MASKED)
        step(lambda g, i: ckn_ref[:, g * LANES:(g + 1) * LANES],
             lambda g, i: cvn_ref[:, g * LANES:(g + 1) * LANES], 1, bias)
        lam = _diff_lambda(lq1_ref, lk1_ref, lq2_ref, lk2_ref, lambda_init)
        acc = acc_ref[...]
        l = l_ref[...]
        outs = []
        rep = C_HEADS // C_KV_HEADS
        for h in range(C_HEADS):
            g, r = h // rep, h % rep
            r1 = ((2 * g) * rep + r) * dec
            r2 = ((2 * g + 1) * rep + r) * dec
            outs.append(_diff_finish(acc[r1:r1 + dec], l[r1:r1 + dec], acc[r2:r2 + dec], l[r2:r2 + dec],
                                     lam, gsub_ref[...], lambda_init))
        o_o[...] = jnp.concatenate(outs, axis=1).astype(o_o.dtype)


def _att1_sample(page_table, cq_bd, ck_new, cv_new, cache_ck, cache_cv, lq1, lk1, lq2, lk2, gsub, lambda_init):
    db, n_pages = page_table.shape
    group = math.gcd(PAGE_GROUP, n_pages)
    n_groups = n_pages // group
    rows = cq_bd.shape[1]
    dec = rows // (2 * C_HEADS)
    per_b = lambda b, s, pt: (b, 0, 0)
    const = lambda b, s, pt: (0, 0)
    vec = lambda a: a.reshape(1, -1)
    c_page = (None, None, LANES * C_KV_HEADS, LANES)
    grid_spec = pltpu.PrefetchScalarGridSpec(
        num_scalar_prefetch=1,
        grid=(db, n_groups + 1),
        in_specs=[pl.BlockSpec((None, rows, LANES), per_b),
                  pl.BlockSpec((None, LANES, _C_KV_W), per_b), pl.BlockSpec((None, LANES, _C_KV_W), per_b)]
        + _page_specs(c_page, n_groups, group, False) + _page_specs(c_page, n_groups, group, False)
        + [pl.BlockSpec((1, HEAD_DIM), const), pl.BlockSpec((1, HEAD_DIM), const),
           pl.BlockSpec((1, HEAD_DIM), const), pl.BlockSpec((1, HEAD_DIM), const), pl.BlockSpec((1, LANES), const)],
        out_specs=pl.BlockSpec((None, dec, _C_Q_W), per_b),
        scratch_shapes=[pltpu.VMEM((rows, 1), F32), pltpu.VMEM((rows, 1), F32), pltpu.VMEM((rows, LANES), F32)],
    )
    return pl.pallas_call(
        functools.partial(_att1_sample_body, n_groups, dec, group, lambda_init),
        grid_spec=grid_spec,
        out_shape=jax.ShapeDtypeStruct((db, dec, _C_Q_W), BF16),
        compiler_params=_cparams("parallel", "arbitrary"),
        name="att1_sample",
    )(page_table, cq_bd, ck_new, cv_new, *([cache_ck] * group + [cache_cv] * group),
      vec(lq1), vec(lk1), vec(lq2), vec(lk2), vec(gsub))


def _rope_tables(pos):
    half = HEAD_DIM // 2
    inv_freq = ROPE_THETA ** (-jnp.arange(half, dtype=F32) / half)
    ang = pos.astype(F32)[:, None] * inv_freq[None, :]
    cos, sin = jnp.cos(ang), jnp.sin(ang)
    return jnp.concatenate([cos] * 4, axis=1), jnp.concatenate([-sin, sin, -sin, sin], axis=1)


def _block_diag_rows(q_hm, db, dec, slot_of_head, n_slots):
    heads, _, w = q_hm.shape
    q = q_hm.reshape(heads, db, dec, w).transpose(1, 0, 2, 3)
    onehot = (jnp.asarray(slot_of_head)[:, None] == jnp.arange(n_slots)[None, :]).astype(q.dtype)
    return (q[:, :, :, None, :] * onehot[None, :, None, :, None]).reshape(db, heads * dec, n_slots * w)


def kernel(x_prompt, x_sample, cache_a_k, cache_a_v, cache_a_kidx, cache_b_k, cache_b_v, cache_c_k, cache_c_v,
           page_table, norm_mix_pre, norm_mix_post, norm_ffn_pre, norm_ffn_post, w_in_ab, w_out_ab, w_in_c, w_out_c,
           c_lambda_q1, c_lambda_k1, c_lambda_q2, c_lambda_k2, c_subln, w_up, w_down):
    bsz, t, d = x_prompt.shape
    db, dec, _ = x_sample.shape
    n_phys, page = cache_a_k.shape[1], cache_a_k.shape[2]
    n_pages = page_table.shape[1]
    past = n_pages * page
    assert page == LANES and t % KC == 0 and dec & (dec - 1) == 0 and dec <= 8
    mp, ms = bsz * t, db * dec

    cos_p, sin_p = _rope_tables(jnp.arange(t, dtype=jnp.int32))
    cos_s, sin_s = _rope_tables(jnp.arange(past, past + dec, dtype=jnp.int32))
    cos_s, sin_s = jnp.tile(cos_s, (db, 1)), jnp.tile(sin_s, (db, 1))

    xp = x_prompt.reshape(mp, d)
    xs = x_sample.reshape(ms, d)
    w_ab = _prep_w_ab(w_in_ab[0])
    w_out_ab16 = w_out_ab[0].astype(BF16)
    w_c16 = w_in_c[0].astype(BF16)
    w_out_c16 = w_out_c[0].astype(BF16)
    w_up16 = w_up.astype(BF16)
    w_down16 = w_down.astype(BF16)
    lambda_init = 0.8 - 0.6 * math.exp(-0.3 * 1)

    (aq_p, iq_p, bq_p, ak_p, akt_p, av_p, av16_p, ik_p, ikt_p, iw_p, bk_p, bkt_p, bv_p, bv16_p) = _in_proj_ab(
        xp, norm_mix_pre[0], w_ab, cos_p, sin_p)
    o_p = _att0_prompt(aq_p, iq_p, iw_p, bq_p, akt_p, av16_p, ikt_p, bkt_p, bv16_p, bsz, t)
    xp = _post_block(o_p, xp, w_out_ab16, norm_mix_post[0], norm_ffn_pre[0], w_up16[0], w_down16[0], norm_ffn_post[0])

    (aq_s, iq_s, bq_s, ak_s, _, av_s, _, ik_s, _, iw_s, bk_s, _, bv_s, _) = _in_proj_ab(
        xs, norm_mix_pre[0], w_ab, cos_s, sin_s)
    iq_st = iq_s.reshape(IDX_HEADS, db, dec, IDX_DIM).transpose(1, 0, 2, 3).reshape(db, IDX_HEADS * dec, IDX_DIM)
    iw_st = iw_s.reshape(db, dec, IDX_HEADS).transpose(0, 2, 1).reshape(db, IDX_HEADS * dec, 1)
    bq_bd = _block_diag_rows(bq_s, db, dec, list(range(B_HEADS)), B_HEADS)
    aq_bd = _block_diag_rows(aq_s, db, dec, [h // (A_HEADS // A_KV_HEADS) for h in range(A_HEADS)], A_KV_HEADS)
    new16 = lambda a: jnp.pad(a.reshape(db, dec, -1).astype(BF16), ((0, 0), (0, LANES - dec), (0, 0)))
    new16t = lambda a: jnp.pad(a.reshape(db, dec, -1).astype(BF16).transpose(0, 2, 1), ((0, 0), (0, 0), (0, LANES - dec)))
    scores, ob_s = _att0_sample_walk(
        page_table, iq_st, iw_st, bq_bd, new16t(ik_s), new16t(bk_s), new16t(bv_s),
        _keys_on_lanes(cache_a_kidx), _keys_on_lanes(cache_b_k), _keys_on_lanes(cache_b_v))
    n_sel_s = min(TOPK_MAX, (past + dec) // 4)
    n_alloc = scores.shape[0]
    bias = _select_sample(scores.reshape(n_alloc, ms, LANES), n_pages + 1, n_sel_s).reshape(n_alloc, db, dec, LANES)
    oa_s = _att0_sample_dsa(page_table, aq_bd, bias, new16t(ak_s), new16t(av_s),
                            _keys_on_lanes(cache_a_k), _keys_on_lanes(cache_a_v))
    o_s = jnp.concatenate([oa_s, ob_s], axis=-1).reshape(ms, -1)
    xs = _post_block(o_s, xs, w_out_ab16, norm_mix_post[0], norm_ffn_pre[0], w_up16[0], w_down16[0], norm_ffn_post[0])

    cq_p, ck_p, ckt_p, cv_p, cv16_p = _in_proj_c(xp, norm_mix_pre[1], w_c16, cos_p, sin_p)
    o_p = _att1_prompt(cq_p, ckt_p, cv16_p, c_lambda_q1[0], c_lambda_k1[0], c_lambda_q2[0], c_lambda_k2[0], c_subln[0],
                       lambda_init, bsz, t)
    yp = _post_block(o_p, xp, w_out_c16, norm_mix_post[1], norm_ffn_pre[1], w_up16[1], w_down16[1], norm_ffn_post[1])

    cq_s, ck_s, _, cv_s, _ = _in_proj_c(xs, norm_mix_pre[1], w_c16, cos_s, sin_s)
    rep = C_HEADS // C_KV_HEADS
    cq_bd = _block_diag_rows(cq_s, db, dec, [(slot // rep) % 2 for slot in range(2 * C_HEADS)], 2)
    o_s = _att1_sample(page_table, cq_bd, new16(ck_s), new16(cv_s),
                       cache_c_k.reshape(-1, n_phys, page * C_KV_HEADS, LANES),
                       cache_c_v.reshape(-1, n_phys, page * C_KV_HEADS, LANES),
                       c_lambda_q1[0], c_lambda_k1[0], c_lambda_q2[0], c_lambda_k2[0], c_subln[0], lambda_init)
    ys = _post_block(o_s.reshape(ms, -1), xs, w_out_c16, norm_mix_post[1], norm_ffn_pre[1], w_up16[1], w_down16[1],
                     norm_ffn_post[1])

    def rows(a, b_, t_, heads, width):
        shape = (1, b_, t_, heads, width) if heads else (1, b_, t_, width)
        return a.reshape(shape)

    return (yp.reshape(bsz, t, d), ys.reshape(db, dec, d),
            rows(ak_p, bsz, t, A_KV_HEADS, HEAD_DIM), rows(av_p, bsz, t, A_KV_HEADS, HEAD_DIM),
            rows(ik_p, bsz, t, 0, IDX_DIM),
            rows(bk_p, bsz, t, B_HEADS, HEAD_DIM), rows(bv_p, bsz, t, B_HEADS, HEAD_DIM),
            rows(ck_p, bsz, t, C_KV_HEADS, 2 * HEAD_DIM), rows(cv_p, bsz, t, C_KV_HEADS, 2 * HEAD_DIM),
            rows(ak_s, db, dec, A_KV_HEADS, HEAD_DIM), rows(av_s, db, dec, A_KV_HEADS, HEAD_DIM),
            rows(ik_s, db, dec, 0, IDX_DIM),
            rows(bk_s, db, dec, B_HEADS, HEAD_DIM), rows(bv_s, db, dec, B_HEADS, HEAD_DIM),
            rows(ck_s, db, dec, C_KV_HEADS, 2 * HEAD_DIM), rows(cv_s, db, dec, C_KV_HEADS, 2 * HEAD_DIM))
```

```python
import functools
import math

import jax
import jax.numpy as jnp
from jax import lax
from jax.experimental import pallas as pl
from jax.experimental.pallas import tpu as pltpu

F32 = jnp.float32
BF16 = jnp.bfloat16
I32 = jnp.int32

HEAD_DIM = 64
A_HEADS = 8
A_KV_HEADS = 2
IDX_HEADS = 8
IDX_DIM = 64
TOPK_MAX = 256
B_HEADS = 8
C_HEADS = 8
C_KV_HEADS = 4
ROPE_THETA = 10000.0
NORM_EPS = 1e-6
SUBLN_EPS = 1e-5

LANES = 128
KC = 2 * LANES
QB = KC
QB_DIFF = 256
PAGE_GROUP = 8
ROW_TILE = 256
MLP_ROW_TILE = 512
MASKED = -1e30
QK_SCALE = HEAD_DIM ** -0.5
EXP_SCALE = QK_SCALE * math.log2(math.e)
VMEM_LIMIT = 56 * 1024 * 1024


def _cparams(*sem):
    return pltpu.CompilerParams(dimension_semantics=sem, vmem_limit_bytes=VMEM_LIMIT)


def _rms(x, g, eps):
    return x * lax.rsqrt(jnp.mean(x * x, axis=-1, keepdims=True) + eps) * g


def _dot(a, b):
    return jnp.dot(a, b, preferred_element_type=F32)


def _dot_nt(a, b):
    return lax.dot_general(a, b, (((1,), (1,)), ((), ())), preferred_element_type=F32)


def _rope_chunk(seg, cos, sin_signed, first_half):
    partner = jnp.where(first_half, pltpu.roll(seg, LANES - HEAD_DIM // 2, 1), pltpu.roll(seg, HEAD_DIM // 2, 1))
    return seg * cos + partner * sin_signed


def _rope_wide(y, cos, sin_signed, first_half):
    return jnp.concatenate([_rope_chunk(y[:, c * LANES:(c + 1) * LANES], cos, sin_signed, first_half)
                            for c in range(y.shape[1] // LANES)], axis=1)


def _first_half_mask(shape):
    lane = lax.broadcasted_iota(I32, shape, 1)
    return (lane & (HEAD_DIM - 1)) < HEAD_DIM // 2


def _split_heads(y, out_ref, scale=None):
    for h in range(y.shape[1] // HEAD_DIM):
        v = y[:, h * HEAD_DIM:(h + 1) * HEAD_DIM]
        out_ref[h] = (v if scale is None else v * scale).astype(out_ref.dtype)


def _split_heads_t(y, out_ref):
    yt = y.T
    for h in range(yt.shape[0] // HEAD_DIM):
        out_ref[h] = yt[h * HEAD_DIM:(h + 1) * HEAD_DIM, :].astype(out_ref.dtype)


_AB_AQ, _AB_IQ, _AB_AK, _AB_AV, _AB_BQ, _AB_BK, _AB_BV, _AB_TAIL, _AB_WIDTH = 0, 512, 1024, 1152, 1280, 1792, 2304, 2816, 2944
_A_KV_W = A_KV_HEADS * HEAD_DIM
_B_W = B_HEADS * HEAD_DIM


def _prep_w_ab(w):
    d = w.shape[0]
    sizes = (A_HEADS * HEAD_DIM, _A_KV_W, _A_KV_W, IDX_HEADS * IDX_DIM, IDX_DIM, IDX_HEADS, _B_W, _B_W, _B_W)
    offs = [0]
    for s in sizes:
        offs.append(offs[-1] + s)
    a_q, a_k, a_v, i_q, i_k, i_w, b_q, b_k, b_v = (w[:, offs[i]:offs[i + 1]] for i in range(9))
    pad = jnp.zeros((d, _AB_WIDTH - _AB_TAIL - IDX_DIM - IDX_HEADS), w.dtype)
    return jnp.concatenate([a_q, i_q, a_k, a_v, b_q, b_k, b_v, i_k, i_w, pad], axis=1).astype(BF16)


def _store_kv(y, rows_o, per_batch):
    if per_batch:
        _split_heads_t(y, rows_o)
    else:
        rows_o[...] = y


def _in_ab_body(per_batch, x_ref, g_ref, w_ref, cos_ref, sin_ref,
                aq_o, iq_o, bq_o, ak_o, akt_o, av_o, av16_o, ik_o, ikt_o, iw_o, bk_o, bkt_o, bv_o, bv16_o):
    xn = _rms(x_ref[...], g_ref[...], NORM_EPS).astype(BF16)
    cos = cos_ref[...]
    sin = sin_ref[...]
    first = _first_half_mask(cos.shape)

    def proj(lo, width):
        return _dot(xn, w_ref[:, lo:lo + width])

    _split_heads(_rope_wide(proj(_AB_AQ, 512), cos, sin, first), aq_o, EXP_SCALE)
    _split_heads(_rope_wide(proj(_AB_IQ, 512), cos, sin, first), iq_o, QK_SCALE)
    _split_heads(proj(_AB_BQ, 512), bq_o, EXP_SCALE)

    ak = _rope_chunk(proj(_AB_AK, _A_KV_W), cos, sin, first)
    _store_kv(ak, ak_o, per_batch)
    _split_heads_t(ak, akt_o)
    av = proj(_AB_AV, _A_KV_W)
    _store_kv(av, av_o, per_batch)
    av16_o[...] = av.astype(BF16)

    bk = proj(_AB_BK, _B_W)
    _store_kv(bk, bk_o, per_batch)
    _split_heads_t(bk, bkt_o)
    bv = proj(_AB_BV, _B_W)
    _store_kv(bv, bv_o, per_batch)
    bv16_o[...] = bv.astype(BF16)

    tail = proj(_AB_TAIL, LANES)
    ik = _rope_chunk(tail, cos, sin, first)
    ik_t = ik.T[:IDX_DIM, :]
    ik_o[...] = ik_t if per_batch else ik[:, :IDX_DIM]
    ikt_o[...] = ik_t.astype(BF16)
    iw_o[...] = tail[:, IDX_DIM:IDX_DIM + IDX_HEADS] * (IDX_HEADS ** -0.5)


def _in_proj_ab(x2d, g, w_perm, cos, sin, batch_tokens=None):
    m, d = x2d.shape
    tm = min(ROW_TILE, m)
    n_pos_blocks = cos.shape[0] // tm
    row = lambda i: (i, 0)
    hrow = lambda i: (0, i, 0)
    hcol = lambda i: (0, 0, i)
    pos = lambda i: (i % n_pos_blocks, 0)
    const = lambda i: (0, 0)
    sds = jax.ShapeDtypeStruct
    hm = lambda heads: sds((heads, m, HEAD_DIM), BF16)
    hmt = lambda heads: sds((heads, HEAD_DIM, m), BF16)
    hspec = lambda heads: pl.BlockSpec((heads, tm, HEAD_DIM), hrow)
    htspec = lambda heads: pl.BlockSpec((heads, HEAD_DIM, tm), hcol)
    rspec = lambda width: pl.BlockSpec((tm, width), row)
    per_batch = batch_tokens is not None
    if per_batch:
        nt = batch_tokens // tm
        bsz = m // batch_tokens
        kv = lambda heads: (sds((bsz, heads, HEAD_DIM, batch_tokens), F32),
                            pl.BlockSpec((None, heads, HEAD_DIM, tm), lambda i: (i // nt, 0, 0, i % nt)))
        ik = (sds((bsz, IDX_DIM, batch_tokens), F32), pl.BlockSpec((None, IDX_DIM, tm), lambda i: (i // nt, 0, i % nt)))
    else:
        kv = lambda heads: (sds((m, heads * HEAD_DIM), F32), rspec(heads * HEAD_DIM))
        ik = (sds((m, IDX_DIM), F32), rspec(IDX_DIM))
    outs = (
        (hm(A_HEADS), hspec(A_HEADS)), (hm(IDX_HEADS), hspec(IDX_HEADS)), (hm(B_HEADS), hspec(B_HEADS)),
        kv(A_KV_HEADS), (hmt(A_KV_HEADS), htspec(A_KV_HEADS)), kv(A_KV_HEADS), (sds((m, _A_KV_W), BF16), rspec(_A_KV_W)),
        ik, (sds((IDX_DIM, m), BF16), pl.BlockSpec((IDX_DIM, tm), lambda i: (0, i))),
        (sds((m, IDX_HEADS), F32), rspec(IDX_HEADS)),
        kv(B_HEADS), (hmt(B_HEADS), htspec(B_HEADS)), kv(B_HEADS), (sds((m, _B_W), BF16), rspec(_B_W)),
    )
    return pl.pallas_call(
        functools.partial(_in_ab_body, per_batch),
        grid=(m // tm,),
        in_specs=[pl.BlockSpec((tm, d), row), pl.BlockSpec((1, d), const), pl.BlockSpec((d, _AB_WIDTH), const),
                  pl.BlockSpec((tm, LANES), pos), pl.BlockSpec((tm, LANES), pos)],
        out_specs=tuple(o[1] for o in outs),
        out_shape=tuple(o[0] for o in outs),
        compiler_params=_cparams("parallel"),
        name="in_proj_ab",
    )(x2d, g.reshape(1, d), w_perm, cos, sin)


_C_Q_W = C_HEADS * 2 * HEAD_DIM
_C_KV_W = C_KV_HEADS * 2 * HEAD_DIM


def _store_kv_pairs(y, rows_o, interleaved):
    if interleaved:
        for g in range(C_KV_HEADS):
            rows_o[pl.ds(g, y.shape[0], stride=C_KV_HEADS), :] = y[:, g * LANES:(g + 1) * LANES]
    else:
        rows_o[...] = y


def _in_c_body(interleaved, x_ref, g_ref, w_ref, cos_ref, sin_ref, cq_o, ck_o, ckt_o, cv_o, cv16_o):
    xn = _rms(x_ref[...], g_ref[...], NORM_EPS).astype(BF16)
    cos = cos_ref[...]
    sin = sin_ref[...]
    first = _first_half_mask(cos.shape)

    def proj(lo, width):
        return _dot(xn, w_ref[:, lo:lo + width])

    rep = C_HEADS // C_KV_HEADS
    for block in range(2):
        q = _rope_wide(proj(block * 512, 512), cos, sin, first)
        for i in range(8):
            head, half = divmod(block * 8 + i, 2)
            slot = ((head // rep) * 2 + half) * rep + head % rep
            cq_o[slot] = (q[:, i * HEAD_DIM:(i + 1) * HEAD_DIM] * EXP_SCALE).astype(BF16)
    k = _rope_wide(proj(_C_Q_W, _C_KV_W), cos, sin, first)
    _store_kv_pairs(k, ck_o, interleaved)
    _split_heads_t(k, ckt_o)
    v = proj(_C_Q_W + _C_KV_W, _C_KV_W)
    _store_kv_pairs(v, cv_o, interleaved)
    cv16_o[...] = v.astype(BF16)


def _in_proj_c(x2d, g, w_bf16, cos, sin, interleaved=False):
    m, d = x2d.shape
    tm = min(ROW_TILE, m)
    n_pos_blocks = cos.shape[0] // tm
    row = lambda i: (i, 0)
    hrow = lambda i: (0, i, 0)
    hcol = lambda i: (0, 0, i)
    pos = lambda i: (i % n_pos_blocks, 0)
    const = lambda i: (0, 0)
    width = _C_Q_W + 2 * _C_KV_W
    sds = jax.ShapeDtypeStruct
    if interleaved:
        kv = (sds((m * C_KV_HEADS, LANES), F32), pl.BlockSpec((tm * C_KV_HEADS, LANES), row))
    else:
        kv = (sds((m, _C_KV_W), F32), pl.BlockSpec((tm, _C_KV_W), row))
    outs = (
        (sds((2 * C_HEADS, m, HEAD_DIM), BF16), pl.BlockSpec((2 * C_HEADS, tm, HEAD_DIM), hrow)),
        kv, (sds((2 * C_KV_HEADS, HEAD_DIM, m), BF16), pl.BlockSpec((2 * C_KV_HEADS, HEAD_DIM, tm), hcol)),
        kv, (sds((m, _C_KV_W), BF16), pl.BlockSpec((tm, _C_KV_W), row)),
    )
    return pl.pallas_call(
        functools.partial(_in_c_body, interleaved),
        grid=(m // tm,),
        in_specs=[pl.BlockSpec((tm, d), row), pl.BlockSpec((1, d), const), pl.BlockSpec((d, width), const),
                  pl.BlockSpec((tm, LANES), pos), pl.BlockSpec((tm, LANES), pos)],
        out_specs=tuple(o[1] for o in outs),
        out_shape=tuple(o[0] for o in outs),
        compiler_params=_cparams("parallel"),
        name="in_proj_c",
    )(x2d, g.reshape(1, d), w_bf16, cos, sin)


def _post_body(o_ref, x_ref, wout_ref, gpost_ref, gpre_ref, wup_ref, wdown_ref, gfpost_ref, y_ref):
    mix = _dot(o_ref[...], wout_ref[...])
    x1 = x_ref[...] + _rms(mix, gpost_ref[...], NORM_EPS)
    hn = _rms(x1, gpre_ref[...], NORM_EPS).astype(BF16)
    d_ff = wup_ref.shape[1]
    step = 1024
    f = jnp.zeros(x1.shape, F32)
    for c in range(d_ff // step):
        u = jnp.maximum(_dot(hn, wup_ref[:, c * step:(c + 1) * step]), 0.0)
        f = f + _dot((u * u).astype(BF16), wdown_ref[c * step:(c + 1) * step, :])
    y_ref[...] = x1 + _rms(f, gfpost_ref[...], NORM_EPS)


def _post_block(o2d, x2d, w_out, g_post, g_pre, w_up, w_down, g_fpost):
    m, d = x2d.shape
    tm = min(MLP_ROW_TILE if m % MLP_ROW_TILE == 0 else ROW_TILE, m)
    assert m % tm == 0
    d_ff = w_up.shape[1]
    row = lambda i: (i, 0)
    const = lambda i: (0, 0)
    resident = lambda shape: pl.BlockSpec(shape, const, pipeline_mode=pl.Buffered(1))
    return pl.pallas_call(
        _post_body,
        grid=(m // tm,),
        in_specs=[pl.BlockSpec((tm, o2d.shape[1]), row), pl.BlockSpec((tm, d), row),
                  resident(w_out.shape), pl.BlockSpec((1, d), const), pl.BlockSpec((1, d), const),
                  resident((d, d_ff)), resident((d_ff, d)), pl.BlockSpec((1, d), const)],
        out_specs=pl.BlockSpec((tm, d), row),
        out_shape=jax.ShapeDtypeStruct((m, d), F32),
        compiler_params=_cparams("parallel"),
        name="post_block",
    )(o2d, x2d, w_out, g_post.reshape(1, d), g_pre.reshape(1, d), w_up, w_down, g_fpost.reshape(1, d))


def _order_key(s):
    s = jnp.where(s == 0.0, 0.0, s)
    b = pltpu.bitcast(s, I32)
    return b ^ ((b >> 31) & I32(0x7FFFFFFF))


def _select_bias(sc_ref, key_ref, nch, n_sel, rows, width):
    def fill_keys(j, carry):
        s = sc_ref[j]
        key_ref[j] = _order_key(s)
        return jnp.maximum(carry[0], s), jnp.minimum(carry[1], s)

    lane_max, lane_min = lax.fori_loop(0, nch, fill_keys, (jnp.full((rows, width), -jnp.inf, F32),
                                                           jnp.full((rows, width), jnp.inf, F32)))

    def count(pred):
        def body(j, c):
            hit = jnp.where(pred(key_ref[j]), 1.0, 0.0)
            return c + sum(hit[:, i * LANES:(i + 1) * LANES] for i in range(width // LANES))

        c = lax.fori_loop(0, nch, body, jnp.zeros((rows, LANES), F32))
        return jnp.sum(c, axis=1, keepdims=True)

    def count_ge(cand):
        cand_b = jnp.broadcast_to(cand, (rows, width))
        return count(lambda k: k >= cand_b)

    k_sel = float(n_sel)
    low_score = jnp.min(lane_max if n_sel <= width else lane_min, axis=1, keepdims=True)
    lo = _order_key(low_score)
    hi = _order_key(jnp.max(lane_max, axis=1, keepdims=True))
    lo_b = jnp.broadcast_to(lo, (rows, width))
    hi = jnp.where(count(lambda k: k > lo_b) < k_sel, lo, hi)

    def open_rows(lo, hi):
        return jnp.max(jnp.where(lo < hi, 1.0, 0.0))

    def bisect(carry):
        lo, hi, _ = carry
        mid = (lo | hi) - ((lo ^ hi) >> 1)
        n_ge = count_ge(mid)
        lo = jnp.where(n_ge >= k_sel, mid, lo)
        hi = jnp.where(n_ge > k_sel, hi, jnp.where(n_ge == k_sel, mid, mid - 1))
        return lo, hi, open_rows(lo, hi)

    thr, _, _ = lax.while_loop(lambda carry: carry[2] > 0.5, bisect, (lo, hi, open_rows(lo, hi)))
    thr_b = jnp.broadcast_to(thr, (rows, width))
    need_b = jnp.broadcast_to(k_sel - count(lambda k: k > thr_b), (rows, width))
    r = lax.broadcasted_iota(I32, (width, width), 0)
    c = lax.broadcasted_iota(I32, (width, width), 1)
    upper = jnp.where(r < c, 1.0, 0.0).astype(BF16)
    ones = jnp.ones((width, width), BF16)

    def tie_body(j, run):
        k = key_ref[j]
        eq = jnp.where(k == thr_b, 1.0, 0.0)
        eq16 = eq.astype(BF16)
        rank = run + _dot(eq16, upper)
        bias = jnp.where(k > thr_b, 0.0, jnp.where(eq * jnp.where(rank < need_b, 1.0, 0.0) > 0.5, 0.0, MASKED))
        sc_ref[j] = jnp.where(sc_ref[j] == -jnp.inf, MASKED, bias)
        return run + _dot(eq16, ones)

    lax.fori_loop(0, nch, tie_body, jnp.zeros((rows, width), F32))


def _weighted_values(p16, v16, v_transposed):
    return _dot_nt(p16, v16) if v_transposed else _dot(p16, v16)


def _softmax_step(s, values, m, l, acc, v_transposed=False):
    m_new = jnp.maximum(m, jnp.max(s, axis=1, keepdims=True))
    alpha = jnp.exp2(m - m_new)
    p = jnp.exp2(s - m_new)
    l = alpha * l + jnp.sum(p, axis=1, keepdims=True)
    p16 = p.astype(BF16)
    if not isinstance(values, (list, tuple)):
        pv = _weighted_values(p16, values, v_transposed)
    else:
        pv = sum(_weighted_values(p16[:, i * LANES:(i + 1) * LANES], v, v_transposed) for i, v in enumerate(values))
    return m_new, l, alpha * acc + pv


def _stick_matrices():
    r = lax.broadcasted_iota(I32, (2 * LANES, 2 * LANES), 0) & (LANES - 1)
    c = lax.broadcasted_iota(I32, (2 * LANES, 2 * LANES), 1)
    return jnp.where((c >= LANES) | (r > c), 1.0, 0.0).astype(BF16)


def _softmax_ones_step(s, v_ext16, m, la):
    tiles = s.shape[1] // LANES
    m_new = jnp.maximum(m, jnp.max(s.astype(BF16), axis=1, keepdims=True).astype(F32))
    alpha = jnp.exp2(m - m_new)
    p16 = jnp.exp2(s - jnp.concatenate([m_new] * tiles, axis=1)).astype(BF16)
    return m_new, jnp.concatenate([alpha] * (la.shape[1] // LANES), axis=1) * la + _dot(p16, v_ext16)


def _stick_weights(zs, valid, run, mats):
    rows = zs[0].shape[0]
    softplus = [jnp.maximum(z, 0.0) + jnp.log2(1.0 + jnp.exp2(-jnp.abs(z))) for z in zs]
    split = []
    for sp in softplus:
        log_stay = -sp if valid is None else jnp.where(valid, -sp, 0.0)
        hi = log_stay.astype(BF16)
        lo = (log_stay - hi.astype(F32)).astype(BF16)
        split.append(jnp.concatenate([hi, lo], axis=1))
    sums = _dot(split[0] if len(split) == 1 else jnp.concatenate(split, axis=0), mats)
    ws = []
    for n, (z, sp) in enumerate(zip(zs, softplus)):
        part = sums[n * rows:(n + 1) * rows]
        w = jnp.exp2((z - sp) + part[:, :LANES] + run)
        if valid is not None:
            w = jnp.where(valid, w, 0.0)
        ws.append(w.astype(BF16))
        run = run + part[:, LANES:]
    return run, ws


def _chunk_count(qi, qb=QB):
    return lax.shift_right_logical((qi + 1) * qb - 1, KC.bit_length() - 1) + 1


def _chunk_lanes(j):
    return pl.ds(pl.multiple_of(j * KC, KC), KC)


def _att0_prompt_body(n_sel, aq_ref, iq_ref, iw_ref, bq_ref, akt_ref, av_ref, ikt_ref, bkt_ref, bv_ref,
                      o_ref, sc_ref, key_ref, m_ref, la_ref, run_ref, acc_ref):
    qi = pl.program_id(1)
    n_kc = _chunk_count(qi)
    row = lax.broadcasted_iota(I32, (QB, KC), 0)
    col = lax.broadcasted_iota(I32, (QB, KC), 1)
    qpos = qi * QB + row

    def score_chunk(j, c):
        kt = ikt_ref[:, _chunk_lanes(j)]
        acc = jnp.zeros((QB, KC), F32)
        for h in range(IDX_HEADS):
            acc = acc + jnp.maximum(_dot(iq_ref[h], kt), 0.0) * iw_ref[:, h:h + 1]
        sc_ref[j] = jnp.where(j * KC + col <= qpos, acc, -jnp.inf)
        return c

    lax.fori_loop(0, n_kc, score_chunk, 0)
    _select_bias(sc_ref, key_ref, n_kc, n_sel, QB, KC)

    m_ref[...] = jnp.full_like(m_ref, MASKED)
    la_ref[...] = jnp.zeros_like(la_ref)
    rep = A_HEADS // A_KV_HEADS
    ones = jnp.ones((KC, LANES), BF16)

    def dsa_chunk(j, c):
        bias = jnp.concatenate([sc_ref[j]] * rep, axis=0)
        v_ext = jnp.concatenate([av_ref[_chunk_lanes(j), :], ones], axis=1)
        for g in range(A_KV_HEADS):
            q = aq_ref[rep * g:rep * (g + 1)].reshape(rep * QB, HEAD_DIM)
            s = _dot(q, akt_ref[g, :, _chunk_lanes(j)]) + bias
            m_ref[g], la_ref[g] = _softmax_ones_step(s, v_ext, m_ref[g], la_ref[g])
        return c

    lax.fori_loop(0, n_kc, dsa_chunk, 0)
    outs = []
    for h in range(A_HEADS):
        g, r = h // rep, h % rep
        la = la_ref[g][r * QB:(r + 1) * QB]
        outs.append(la[:, g * HEAD_DIM:(g + 1) * HEAD_DIM] / la[:, LANES:LANES + HEAD_DIM])
    o_ref[:, :A_HEADS * HEAD_DIM] = jnp.concatenate(outs, axis=1).astype(o_ref.dtype)

    run_ref[...] = jnp.zeros_like(run_ref)
    acc_ref[...] = jnp.zeros_like(acc_ref)
    mats = _stick_matrices()
    rows_all = B_HEADS * QB
    qpos_c = qi * QB + (lax.broadcasted_iota(I32, (rows_all, LANES), 0) & (QB - 1))
    col_c = lax.broadcasted_iota(I32, (rows_all, LANES), 1)

    def sb_keys(starts, on_diagonal):
        keys = [pl.ds(pl.multiple_of(start, LANES), LANES) for start in starts]
        zs = [jnp.concatenate([_dot(bq_ref[h], bkt_ref[h, :, k]) for h in range(B_HEADS)], axis=0) for k in keys]
        valid = (starts[0] + col_c < qpos_c) if on_diagonal else None
        run_ref[...], ws = _stick_weights(zs, valid, run_ref[...], mats)
        for pair in range(B_HEADS // 2):
            rows = slice(2 * pair * QB, 2 * (pair + 1) * QB)
            acc_ref[pair] += sum(_dot(w16[rows], bv_ref[k, pair * LANES:(pair + 1) * LANES]) for w16, k in zip(ws, keys))

    n_full = qi * (QB // KC)
    for sub in reversed(range(QB // LANES)):
        sb_keys([qi * QB + sub * LANES], True)

    def sb_chunk(i, c):
        j = n_full - 1 - i
        sb_keys([j * KC + sub * LANES for sub in reversed(range(KC // LANES))], False)
        return c

    lax.fori_loop(0, n_full, sb_chunk, 0)
    outs = [acc_ref[h // 2][(h % 2) * QB:(h % 2 + 1) * QB, (h % 2) * HEAD_DIM:(h % 2 + 1) * HEAD_DIM]
            for h in range(B_HEADS)]
    o_ref[:, A_HEADS * HEAD_DIM:] = jnp.concatenate(outs, axis=1).astype(o_ref.dtype)


def _att0_prompt(aq, iq, iw, bq, akt, av16, ikt, bkt, bv16, bsz, t):
    nq = t // QB
    n_sel = min(TOPK_MAX, t // 4)
    qh = lambda b, q: (0, b * nq + q, 0)
    qrow = lambda b, q: (b * nq + q, 0)
    kt = lambda b, q: (0, 0, b)
    krow = lambda b, q: (b, 0)
    width = (A_HEADS + B_HEADS) * HEAD_DIM
    rows_a = (A_HEADS // A_KV_HEADS) * QB
    return pl.pallas_call(
        functools.partial(_att0_prompt_body, n_sel),
        grid=(bsz, nq),
        in_specs=[pl.BlockSpec((A_HEADS, QB, HEAD_DIM), qh), pl.BlockSpec((IDX_HEADS, QB, IDX_DIM), qh),
                  pl.BlockSpec((QB, IDX_HEADS), qrow), pl.BlockSpec((B_HEADS, QB, HEAD_DIM), qh),
                  pl.BlockSpec((A_KV_HEADS, HEAD_DIM, t), kt), pl.BlockSpec((t, _A_KV_W), krow),
                  pl.BlockSpec((IDX_DIM, t), lambda b, q: (0, b)),
                  pl.BlockSpec((B_HEADS, HEAD_DIM, t), kt), pl.BlockSpec((t, _B_W), krow)],
        out_specs=pl.BlockSpec((QB, width), qrow),
        out_shape=jax.ShapeDtypeStruct((bsz * t, width), BF16),
        scratch_shapes=[pltpu.VMEM((t // KC, QB, KC), F32), pltpu.VMEM((t // KC, QB, KC), I32),
                        pltpu.VMEM((A_KV_HEADS, rows_a, LANES), F32), pltpu.VMEM((A_KV_HEADS, rows_a, 2 * LANES), F32),
                        pltpu.VMEM((B_HEADS * QB, LANES), F32), pltpu.VMEM((B_HEADS // 2, 2 * QB, LANES), F32)],
        compiler_params=_cparams("parallel", "arbitrary"),
        name="att0_prompt",
    )(aq, iq, iw, bq, akt, av16, ikt, bkt, bv16)


def _diff_lambda(lq1_ref, lk1_ref, lq2_ref, lk2_ref, lambda_init):
    s1 = jnp.sum(lq1_ref[...] * lk1_ref[...], axis=1, keepdims=True)
    s2 = jnp.sum(lq2_ref[...] * lk2_ref[...], axis=1, keepdims=True)
    return jnp.exp(s1) - jnp.exp(s2) + lambda_init


def _diff_finish(a1, l1, a2, l2, lam, gsub, lambda_init):
    o = a1 / l1 - lam * (a2 / l2)
    return _rms(o, gsub, SUBLN_EPS) * (1.0 - lambda_init)


def _att1_prompt_body(lambda_init, cq_ref, ckt_ref, cv_ref, lq1_ref, lk1_ref, lq2_ref, lk2_ref, gsub_ref,
                      o_ref, m_ref, la_ref):
    qi = pl.program_id(1)
    qb = o_ref.shape[0]
    rep = C_HEADS // C_KV_HEADS
    rows = rep * qb
    n_kc = _chunk_count(qi, qb)
    row = lax.broadcasted_iota(I32, (rows, KC), 0)
    col = lax.broadcasted_iota(I32, (rows, KC), 1)
    qpos = qi * qb + (row & (qb - 1))
    m_ref[...] = jnp.full_like(m_ref, MASKED)
    la_ref[...] = jnp.zeros_like(la_ref)
    ones = jnp.ones((KC, LANES), BF16)

    def chunk(j, on_diagonal):
        bias = jnp.where(j * KC + col <= qpos, 0.0, MASKED) if on_diagonal else None
        for g in range(C_KV_HEADS):
            v_ext = jnp.concatenate([cv_ref[_chunk_lanes(j), g * LANES:(g + 1) * LANES], ones], axis=1)
            for half in range(2):
                u = 2 * g + half
                q = cq_ref[rep * u:rep * (u + 1)].reshape(rows, HEAD_DIM)
                s = _dot(q, ckt_ref[u, :, _chunk_lanes(j)])
                if on_diagonal:
                    s = s + bias
                m_ref[u], la_ref[u] = _softmax_ones_step(s, v_ext, m_ref[u], la_ref[u])

    def full_chunk(j, c):
        chunk(j, False)
        return c

    lax.fori_loop(0, n_kc - 1, full_chunk, 0)
    chunk(n_kc - 1, True)

    lam = _diff_lambda(lq1_ref, lk1_ref, lq2_ref, lk2_ref, lambda_init)
    gsub = gsub_ref[...]
    for h in range(C_HEADS):
        g, r = h // rep, h % rep
        la1 = la_ref[2 * g][r * qb:(r + 1) * qb]
        la2 = la_ref[2 * g + 1][r * qb:(r + 1) * qb]
        o = _diff_finish(la1[:, :LANES], la1[:, LANES:], la2[:, :LANES], la2[:, LANES:], lam, gsub, lambda_init)
        o_ref[:, h * LANES:(h + 1) * LANES] = o.astype(o_ref.dtype)


def _att1_prompt(cq, ckt, cv16, lq1, lk1, lq2, lk2, gsub, lambda_init, bsz, t):
    qb = QB_DIFF
    nq = t // qb
    qh = lambda b, q: (0, b * nq + q, 0)
    qrow = lambda b, q: (b * nq + q, 0)
    const = lambda b, q: (0, 0)
    vec = lambda a: a.reshape(1, -1)
    rows = (C_HEADS // C_KV_HEADS) * qb
    return pl.pallas_call(
        functools.partial(_att1_prompt_body, lambda_init),
        grid=(bsz, nq),
        in_specs=[pl.BlockSpec((2 * C_HEADS, qb, HEAD_DIM), qh),
                  pl.BlockSpec((2 * C_KV_HEADS, HEAD_DIM, t), lambda b, q: (0, 0, b)),
                  pl.BlockSpec((t, _C_KV_W), lambda b, q: (b, 0)),
                  pl.BlockSpec((1, HEAD_DIM), const), pl.BlockSpec((1, HEAD_DIM), const),
                  pl.BlockSpec((1, HEAD_DIM), const), pl.BlockSpec((1, HEAD_DIM), const),
                  pl.BlockSpec((1, LANES), const)],
        out_specs=pl.BlockSpec((qb, _C_Q_W), qrow),
        out_shape=jax.ShapeDtypeStruct((bsz * t, _C_Q_W), BF16),
        scratch_shapes=[pltpu.VMEM((2 * C_KV_HEADS, rows, LANES), F32),
                        pltpu.VMEM((2 * C_KV_HEADS, rows, 2 * LANES), F32)],
        compiler_params=_cparams("parallel", "arbitrary"),
        name="att1_prompt",
    )(cq, ckt, cv16, vec(lq1), vec(lk1), vec(lq2), vec(lk2), vec(gsub))


def _page_specs(block, n_groups, group, newest_first):
    zeros = (0,) * (len(block) - 2)

    def index(i):
        if newest_first:
            return lambda b, s, pt: (0, pt[b, jnp.minimum(n_groups - s, n_groups - 1) * group + i]) + zeros
        return lambda b, s, pt: (0, pt[b, jnp.minimum(s, n_groups - 1) * group + i]) + zeros

    return [pl.BlockSpec(block, index(i)) for i in range(group)]


def _keys_on_lanes(cache):
    nd = cache.ndim
    return jnp.transpose(cache, (0, 1) + tuple(range(3, nd)) + (2,))


def _att0_sample_walk_body(dec, group, pt_ref, iq_ref, iw_ref, bq_ref, ikn_ref, bkn_ref, bvn_ref, *refs):
    del pt_ref
    ikc_refs, bkc_refs, bvc_refs = refs[:group], refs[group:2 * group], refs[2 * group:3 * group]
    sc_o, ob_o, run_ref, acc_ref = refs[3 * group:]
    s = pl.program_id(1)
    rows = B_HEADS * dec
    row = lax.broadcasted_iota(I32, (rows, LANES), 0)
    col = lax.broadcasted_iota(I32, (rows, LANES), 1)
    t_of_row = row & (dec - 1)
    mats = _stick_matrices()

    def scores(kt16):
        lg = jnp.maximum(_dot(iq_ref[...], kt16), 0.0) * iw_ref[...]
        out = lg[0:dec]
        for h in range(1, IDX_HEADS):
            out = out + lg[h * dec:(h + 1) * dec]
        return out

    def stick(kts, vts, valid):
        run_ref[...], ws = _stick_weights([_dot(bq_ref[...], kt) for kt in kts], valid, run_ref[...], mats)
        acc_ref[...] += sum(_dot_nt(w16, vt) for w16, vt in zip(ws, vts))

    @pl.when(s == 0)
    def _():
        run_ref[...] = jnp.zeros_like(run_ref)
        acc_ref[...] = jnp.zeros_like(acc_ref)
        causal = (col[:dec] < dec) & (col[:dec] <= row[:dec])
        sc_o[0] = jnp.where(causal, scores(ikn_ref[...]), -jnp.inf)
        for i in range(1, group):
            sc_o[i] = jnp.full((dec, LANES), -jnp.inf, F32)
        stick([bkn_ref[...]], [bvn_ref[...]], (col < dec) & (col < t_of_row))

    @pl.when(s > 0)
    def _():
        for i in range(group):
            sc_o[i] = scores(ikc_refs[i][...].astype(BF16))
        stick([bkc_refs[i][...].reshape(_B_W, LANES).astype(BF16) for i in reversed(range(group))],
              [bvc_refs[i][...].reshape(_B_W, LANES).astype(BF16) for i in reversed(range(group))], None)

    @pl.when(s == pl.num_programs(1) - 1)
    def _():
        acc = acc_ref[...]
        ob_o[...] = jnp.concatenate([acc[h * dec:(h + 1) * dec, h * HEAD_DIM:(h + 1) * HEAD_DIM]
                                     for h in range(B_HEADS)], axis=1).astype(ob_o.dtype)


def _att0_sample_walk(page_table, iq_st, iw_st, bq_bd, ikt_new, bkt_new, bvt_new, cache_kidx, cache_bk, cache_bv):
    db, n_pages = page_table.shape
    group = math.gcd(PAGE_GROUP, n_pages)
    n_groups = n_pages // group
    dec = iq_st.shape[1] // IDX_HEADS
    rows = B_HEADS * dec
    per_b = lambda b, s, pt: (b, 0, 0)
    chunks_of_step = lambda b, s, pt: (jnp.where(s == 0, n_groups, n_groups - s), b, 0, 0)
    b_page = (None, None, B_HEADS, HEAD_DIM, LANES)
    grid_spec = pltpu.PrefetchScalarGridSpec(
        num_scalar_prefetch=1,
        grid=(db, n_groups + 1),
        in_specs=[pl.BlockSpec((None, rows, IDX_DIM), per_b), pl.BlockSpec((None, rows, 1), per_b),
                  pl.BlockSpec((None, rows, _B_W), per_b),
                  pl.BlockSpec((None, IDX_DIM, LANES), per_b), pl.BlockSpec((None, _B_W, LANES), per_b),
                  pl.BlockSpec((None, _B_W, LANES), per_b)]
        + _page_specs((None, None, IDX_DIM, LANES), n_groups, group, True)
        + _page_specs(b_page, n_groups, group, True) + _page_specs(b_page, n_groups, group, True),
        out_specs=[pl.BlockSpec((group, None, dec, LANES), chunks_of_step), pl.BlockSpec((None, dec, _B_W), per_b)],
        scratch_shapes=[pltpu.VMEM((rows, LANES), F32), pltpu.VMEM((rows, _B_W), F32)],
    )
    return pl.pallas_call(
        functools.partial(_att0_sample_walk_body, dec, group),
        grid_spec=grid_spec,
        out_shape=(jax.ShapeDtypeStruct(((n_groups + 1) * group, db, dec, LANES), F32),
                   jax.ShapeDtypeStruct((db, dec, _B_W), BF16)),
        compiler_params=_cparams("parallel", "arbitrary"),
        name="att0_sample_walk",
    )(page_table, iq_st, iw_st, bq_bd, ikt_new, bkt_new, bvt_new,
      *([cache_kidx] * group + [cache_bk] * group + [cache_bv] * group))


def _select_sample_body(n_sel, nch, sc_ref, bias_o, key_ref):
    rows = sc_ref.shape[1]
    bias_o[...] = sc_ref[...]
    _select_bias(bias_o, key_ref, nch, n_sel, rows, LANES)


def _select_sample(scores, nch, n_sel):
    n_alloc, m, _ = scores.shape
    rows = min(m, 64)
    spec = pl.BlockSpec((n_alloc, rows, LANES), lambda i: (0, i, 0))
    return pl.pallas_call(
        functools.partial(_select_sample_body, n_sel, nch),
        grid=(m // rows,),
        in_specs=[spec],
        out_specs=spec,
        out_shape=jax.ShapeDtypeStruct(scores.shape, F32),
        scratch_shapes=[pltpu.VMEM((n_alloc, rows, LANES), I32)],
        compiler_params=_cparams("parallel"),
        name="select_sample",
    )(scores)


def _att0_sample_dsa_body(n_groups, dec, group, pt_ref, aq_ref, bias_ref, akn_ref, avn_ref, *refs):
    del pt_ref
    akc_refs, avc_refs = refs[:group], refs[group:2 * group]
    oa_o, m_ref, l_ref, acc_ref = refs[2 * group:]
    s = pl.program_id(1)

    @pl.when(s == 0)
    def _():
        m_ref[...] = jnp.full_like(m_ref, MASKED)
        l_ref[...] = jnp.zeros_like(l_ref)
        acc_ref[...] = jnp.zeros_like(acc_ref)

    def step(kts, vts):
        n = len(kts)
        bias = jnp.concatenate([jnp.concatenate([bias_ref[i]] * A_HEADS, axis=0) for i in range(n)], axis=1)
        sc = jnp.concatenate([_dot(aq_ref[...], kt) for kt in kts], axis=1) + bias
        m_ref[...], l_ref[...], acc_ref[...] = _softmax_step(sc, vts, m_ref[...], l_ref[...], acc_ref[...],
                                                             v_transposed=True)

    @pl.when(s < n_groups)
    def _():
        step([r[...].reshape(_A_KV_W, LANES).astype(BF16) for r in akc_refs],
             [r[...].reshape(_A_KV_W, LANES).astype(BF16) for r in avc_refs])

    @pl.when(s == n_groups)
    def _():
        step([akn_ref[...]], [avn_ref[...]])
        o = acc_ref[...] / l_ref[...]
        rep = A_HEADS // A_KV_HEADS
        oa_o[...] = jnp.concatenate(
            [o[h * dec:(h + 1) * dec, (h // rep) * HEAD_DIM:(h // rep + 1) * HEAD_DIM] for h in range(A_HEADS)],
            axis=1).astype(oa_o.dtype)


def _att0_sample_dsa(page_table, aq_bd, bias, akt_new, avt_new, cache_akt, cache_avt):
    db, n_pages = page_table.shape
    group = math.gcd(PAGE_GROUP, n_pages)
    n_groups = n_pages // group
    rows = aq_bd.shape[1]
    dec = rows // A_HEADS
    per_b = lambda b, s, pt: (b, 0, 0)
    a_page = (None, None, A_KV_HEADS, HEAD_DIM, LANES)
    grid_spec = pltpu.PrefetchScalarGridSpec(
        num_scalar_prefetch=1,
        grid=(db, n_groups + 1),
        in_specs=[pl.BlockSpec((None, rows, _A_KV_W), per_b),
                  pl.BlockSpec((group, None, dec, LANES), lambda b, s, pt: (s, b, 0, 0)),
                  pl.BlockSpec((None, _A_KV_W, LANES), per_b), pl.BlockSpec((None, _A_KV_W, LANES), per_b)]
        + _page_specs(a_page, n_groups, group, False) + _page_specs(a_page, n_groups, group, False),
        out_specs=pl.BlockSpec((None, dec, A_HEADS * HEAD_DIM), per_b),
        scratch_shapes=[pltpu.VMEM((rows, 1), F32), pltpu.VMEM((rows, 1), F32), pltpu.VMEM((rows, _A_KV_W), F32)],
    )
    return pl.pallas_call(
        functools.partial(_att0_sample_dsa_body, n_groups, dec, group),
        grid_spec=grid_spec,
        out_shape=jax.ShapeDtypeStruct((db, dec, A_HEADS * HEAD_DIM), BF16),
        compiler_params=_cparams("parallel", "arbitrary"),
        name="att0_sample_dsa",
    )(page_table, aq_bd, bias, akt_new, avt_new, *([cache_akt] * group + [cache_avt] * group))


def _att1_sample_body(n_groups, dec, group, lambda_init, pt_ref, cq_ref, ckn_ref, cvn_ref, *refs):
    del pt_ref
    ckc_refs, cvc_refs = refs[:group], refs[group:2 * group]
    lq1_ref, lk1_ref, lq2_ref, lk2_ref, gsub_ref, o_o, m_ref, l_ref, acc_ref = refs[2 * group:]
    s = pl.program_id(1)
    rows_g = 2 * (C_HEADS // C_KV_HEADS) * dec

    @pl.when(s == 0)
    def _():
        m_ref[...] = jnp.full_like(m_ref, MASKED)
        l_ref[...] = jnp.zeros_like(l_ref)
        acc_ref[...] = jnp.zeros_like(acc_ref)

    def step(keys_of, values_of, n_pages, bias):
        sc = jnp.concatenate(
            [jnp.concatenate([_dot_nt(cq_ref[g * rows_g:(g + 1) * rows_g, :], keys_of(g, i)) for i in range(n_pages)],
                             axis=1) for g in range(C_KV_HEADS)], axis=0)
        if bias is not None:
            sc = sc + bias
        m = m_ref[...]
        m_new = jnp.maximum(m, jnp.max(sc, axis=1, keepdims=True))
        alpha = jnp.exp2(m - m_new)
        p = jnp.exp2(sc - m_new)
        p16 = p.astype(BF16)
        pv = jnp.concatenate(
            [sum(_dot(p16[g * rows_g:(g + 1) * rows_g, i * LANES:(i + 1) * LANES], values_of(g, i))
                 for i in range(n_pages)) for g in range(C_KV_HEADS)], axis=0)
        m_ref[...] = m_new
        l_ref[...] = alpha * l_ref[...] + jnp.sum(p, axis=1, keepdims=True)
        acc_ref[...] = alpha * acc_ref[...] + pv

    @pl.when(s < n_groups)
    def _():
        head_rows = [pl.ds(g, LANES, stride=C_KV_HEADS) for g in range(C_KV_HEADS)]
        step(lambda g, i: ckc_refs[i][head_rows[g], :].astype(BF16),
             lambda g, i: cvc_refs[i][head_rows[g], :].astype(BF16), group, None)

    @pl.when(s == n_groups)
    def _():
        rows = C_KV_HEADS * rows_g
        row = lax.broadcasted_iota(I32, (rows, LANES), 0)
        col = lax.broadcasted_iota(I32, (rows, LANES), 1)
        bias = jnp.where(col <= (row & (dec - 1)), 0.0, MASKED)
        step(lambda g, i: ckn_ref[:, g * LANES:(g + 1) * LANES],
             lambda g, i: cvn_ref[:, g * LANES:(g + 1) * LANES], 1, bias)
        lam = _diff_lambda(lq1_ref, lk1_ref, lq2_ref, lk2_ref, lambda_init)
        acc = acc_ref[...]
        l = l_ref[...]
        outs = []
        rep = C_HEADS // C_KV_HEADS
        for h in range(C_HEADS):
            g, r = h // rep, h % rep
            r1 = ((2 * g) * rep + r) * dec
            r2 = ((2 * g + 1) * rep + r) * dec
            outs.append(_diff_finish(acc[r1:r1 + dec], l[r1:r1 + dec], acc[r2:r2 + dec], l[r2:r2 + dec],
                                     lam, gsub_ref[...], lambda_init))
        o_o[...] = jnp.concatenate(outs, axis=1).astype(o_o.dtype)


def _att1_sample(page_table, cq_bd, ck_new, cv_new, cache_ck, cache_cv, lq1, lk1, lq2, lk2, gsub, lambda_init):
    db, n_pages = page_table.shape
    group = math.gcd(PAGE_GROUP, n_pages)
    n_groups = n_pages // group
    rows = cq_bd.shape[1]
    dec = rows // (2 * C_HEADS)
    per_b = lambda b, s, pt: (b, 0, 0)
    const = lambda b, s, pt: (0, 0)
    vec = lambda a: a.reshape(1, -1)
    c_page = (None, None, LANES * C_KV_HEADS, LANES)
    grid_spec = pltpu.PrefetchScalarGridSpec(
        num_scalar_prefetch=1,
        grid=(db, n_groups + 1),
        in_specs=[pl.BlockSpec((None, rows, LANES), per_b),
                  pl.BlockSpec((None, LANES, _C_KV_W), per_b), pl.BlockSpec((None, LANES, _C_KV_W), per_b)]
        + _page_specs(c_page, n_groups, group, False) + _page_specs(c_page, n_groups, group, False)
        + [pl.BlockSpec((1, HEAD_DIM), const), pl.BlockSpec((1, HEAD_DIM), const),
           pl.BlockSpec((1, HEAD_DIM), const), pl.BlockSpec((1, HEAD_DIM), const), pl.BlockSpec((1, LANES), const)],
        out_specs=pl.BlockSpec((None, dec, _C_Q_W), per_b),
        scratch_shapes=[pltpu.VMEM((rows, 1), F32), pltpu.VMEM((rows, 1), F32), pltpu.VMEM((rows, LANES), F32)],
    )
    return pl.pallas_call(
        functools.partial(_att1_sample_body, n_groups, dec, group, lambda_init),
        grid_spec=grid_spec,
        out_shape=jax.ShapeDtypeStruct((db, dec, _C_Q_W), BF16),
        compiler_params=_cparams("parallel", "arbitrary"),
        name="att1_sample",
    )(page_table, cq_bd, ck_new, cv_new, *([cache_ck] * group + [cache_cv] * group),
      vec(lq1), vec(lk1), vec(lq2), vec(lk2), vec(gsub))


def _rope_tables(pos):
    half = HEAD_DIM // 2
    inv_freq = ROPE_THETA ** (-jnp.arange(half, dtype=F32) / half)
    ang = pos.astype(F32)[:, None] * inv_freq[None, :]
    cos, sin = jnp.cos(ang), jnp.sin(ang)
    return jnp.concatenate([cos] * 4, axis=1), jnp.concatenate([-sin, sin, -sin, sin], axis=1)


def _block_diag_rows(q_hm, db, dec, slot_of_head, n_slots):
    heads, _, w = q_hm.shape
    q = q_hm.reshape(heads, db, dec, w).transpose(1, 0, 2, 3)
    onehot = (jnp.asarray(slot_of_head)[:, None] == jnp.arange(n_slots)[None, :]).astype(q.dtype)
    return (q[:, :, :, None, :] * onehot[None, :, None, :, None]).reshape(db, heads * dec, n_slots * w)


def kernel(x_prompt, x_sample, cache_a_k, cache_a_v, cache_a_kidx, cache_b_k, cache_b_v, cache_c_k, cache_c_v,
           page_table, norm_mix_pre, norm_mix_post, norm_ffn_pre, norm_ffn_post, w_in_ab, w_out_ab, w_in_c, w_out_c,
           c_lambda_q1, c_lambda_k1, c_lambda_q2, c_lambda_k2, c_subln, w_up, w_down):
    bsz, t, d = x_prompt.shape
    db, dec, _ = x_sample.shape
    n_phys, page = cache_a_k.shape[1], cache_a_k.shape[2]
    n_pages = page_table.shape[1]
    past = n_pages * page
    assert page == LANES and t % KC == 0 and dec & (dec - 1) == 0 and dec <= 8
    mp, ms = bsz * t, db * dec

    cos_p, sin_p = _rope_tables(jnp.arange(t, dtype=jnp.int32))
    cos_s, sin_s = _rope_tables(jnp.arange(past, past + dec, dtype=jnp.int32))
    cos_s, sin_s = jnp.tile(cos_s, (db, 1)), jnp.tile(sin_s, (db, 1))

    xp = x_prompt.reshape(mp, d)
    xs = x_sample.reshape(ms, d)
    w_ab = _prep_w_ab(w_in_ab[0])
    w_out_ab16 = w_out_ab[0].astype(BF16)
    w_c16 = w_in_c[0].astype(BF16)
    w_out_c16 = w_out_c[0].astype(BF16)
    w_up16 = w_up.astype(BF16)
    w_down16 = w_down.astype(BF16)
    lambda_init = 0.8 - 0.6 * math.exp(-0.3 * 1)

    (aq_p, iq_p, bq_p, ak_p, akt_p, av_p, av16_p, ik_p, ikt_p, iw_p, bk_p, bkt_p, bv_p, bv16_p) = _in_proj_ab(
        xp, norm_mix_pre[0], w_ab, cos_p, sin_p, batch_tokens=t)
    o_p = _att0_prompt(aq_p, iq_p, iw_p, bq_p, akt_p, av16_p, ikt_p, bkt_p, bv16_p, bsz, t)
    xp = _post_block(o_p, xp, w_out_ab16, norm_mix_post[0], norm_ffn_pre[0], w_up16[0], w_down16[0], norm_ffn_post[0])

    (aq_s, iq_s, bq_s, ak_s, _, av_s, _, ik_s, _, iw_s, bk_s, _, bv_s, _) = _in_proj_ab(
        xs, norm_mix_pre[0], w_ab, cos_s, sin_s)
    iq_st = iq_s.reshape(IDX_HEADS, db, dec, IDX_DIM).transpose(1, 0, 2, 3).reshape(db, IDX_HEADS * dec, IDX_DIM)
    iw_st = iw_s.reshape(db, dec, IDX_HEADS).transpose(0, 2, 1).reshape(db, IDX_HEADS * dec, 1)
    bq_bd = _block_diag_rows(bq_s, db, dec, list(range(B_HEADS)), B_HEADS)
    aq_bd = _block_diag_rows(aq_s, db, dec, [h // (A_HEADS // A_KV_HEADS) for h in range(A_HEADS)], A_KV_HEADS)
    new16 = lambda a: jnp.pad(a.reshape(db, dec, -1).astype(BF16), ((0, 0), (0, LANES - dec), (0, 0)))
    new16t = lambda a: jnp.pad(a.reshape(db, dec, -1).astype(BF16).transpose(0, 2, 1), ((0, 0), (0, 0), (0, LANES - dec)))
    scores, ob_s = _att0_sample_walk(
        page_table, iq_st, iw_st, bq_bd, new16t(ik_s), new16t(bk_s), new16t(bv_s),
        _keys_on_lanes(cache_a_kidx), _keys_on_lanes(cache_b_k), _keys_on_lanes(cache_b_v))
    n_sel_s = min(TOPK_MAX, (past + dec) // 4)
    n_alloc = scores.shape[0]
    bias = _select_sample(scores.reshape(n_alloc, ms, LANES), n_pages + 1, n_sel_s).reshape(n_alloc, db, dec, LANES)
    oa_s = _att0_sample_dsa(page_table, aq_bd, bias, new16t(ak_s), new16t(av_s),
                            _keys_on_lanes(cache_a_k), _keys_on_lanes(cache_a_v))
    o_s = jnp.concatenate([oa_s, ob_s], axis=-1).reshape(ms, -1)
    xs = _post_block(o_s, xs, w_out_ab16, norm_mix_post[0], norm_ffn_pre[0], w_up16[0], w_down16[0], norm_ffn_post[0])

    cq_p, ck_p, ckt_p, cv_p, cv16_p = _in_proj_c(xp, norm_mix_pre[1], w_c16, cos_p, sin_p, interleaved=True)
    o_p = _att1_prompt(cq_p, ckt_p, cv16_p, c_lambda_q1[0], c_lambda_k1[0], c_lambda_q2[0], c_lambda_k2[0], c_subln[0],
                       lambda_init, bsz, t)
    yp = _post_block(o_p, xp, w_out_c16, norm_mix_post[1], norm_ffn_pre[1], w_up16[1], w_down16[1], norm_ffn_post[1])

    cq_s, ck_s, _, cv_s, _ = _in_proj_c(xs, norm_mix_pre[1], w_c16, cos_s, sin_s)
    rep = C_HEADS // C_KV_HEADS
    cq_bd = _block_diag_rows(cq_s, db, dec, [(slot // rep) % 2 for slot in range(2 * C_HEADS)], 2)
    o_s = _att1_sample(page_table, cq_bd, new16(ck_s), new16(cv_s),
                       cache_c_k.reshape(-1, n_phys, page * C_KV_HEADS, LANES),
                       cache_c_v.reshape(-1, n_phys, page * C_KV_HEADS, LANES),
                       c_lambda_q1[0], c_lambda_k1[0], c_lambda_q2[0], c_lambda_k2[0], c_subln[0], lambda_init)
    ys = _post_block(o_s.reshape(ms, -1), xs, w_out_c16, norm_mix_post[1], norm_ffn_pre[1], w_up16[1], w_down16[1],
                     norm_ffn_post[1])

    def rows(a, b_, t_, heads, width):
        shape = (1, b_, t_, heads, width) if heads else (1, b_, t_, width)
        return a.reshape(shape)

    per_head = lambda a: jnp.transpose(a, (0, 3, 1, 2))[None]
    return (yp.reshape(bsz, t, d), ys.reshape(db, dec, d),
            per_head(ak_p), per_head(av_p), jnp.transpose(ik_p, (0, 2, 1))[None], per_head(bk_p), per_head(bv_p),
            rows(ck_p, bsz, t, C_KV_HEADS, 2 * HEAD_DIM), rows(cv_p, bsz, t, C_KV_HEADS, 2 * HEAD_DIM),
            rows(ak_s, db, dec, A_KV_HEADS, HEAD_DIM), rows(av_s, db, dec, A_KV_HEADS, HEAD_DIM),
            rows(ik_s, db, dec, 0, IDX_DIM),
            rows(bk_s, db, dec, B_HEADS, HEAD_DIM), rows(bv_s, db, dec, B_HEADS, HEAD_DIM),
            rows(ck_s, db, dec, C_KV_HEADS, 2 * HEAD_DIM), rows(cv_s, db, dec, C_KV_HEADS, 2 * HEAD_DIM))
```

```python
import functools
import math

import jax
import jax.numpy as jnp
from jax import lax
from jax.experimental import pallas as pl
from jax.experimental.pallas import tpu as pltpu

F32 = jnp.float32
BF16 = jnp.bfloat16
I32 = jnp.int32

HEAD_DIM = 64
A_HEADS = 8
A_KV_HEADS = 2
IDX_HEADS = 8
IDX_DIM = 64
TOPK_MAX = 256
B_HEADS = 8
C_HEADS = 8
C_KV_HEADS = 4
ROPE_THETA = 10000.0
NORM_EPS = 1e-6
SUBLN_EPS = 1e-5

LANES = 128
KC = 2 * LANES
QB = KC
QB_DIFF = 256
PAGE_GROUP = 8
ROW_TILE = 256
MLP_ROW_TILE = 512
MASKED = -1e30
STICK_UNDERFLOW_LOG2 = -160.0
QK_SCALE = HEAD_DIM ** -0.5
EXP_SCALE = QK_SCALE * math.log2(math.e)
VMEM_LIMIT = 56 * 1024 * 1024


def _cparams(*sem):
    return pltpu.CompilerParams(dimension_semantics=sem, vmem_limit_bytes=VMEM_LIMIT)


def _rms(x, g, eps):
    return x * lax.rsqrt(jnp.mean(x * x, axis=-1, keepdims=True) + eps) * g


def _dot(a, b):
    return jnp.dot(a, b, preferred_element_type=F32)


def _dot_nt(a, b):
    return lax.dot_general(a, b, (((1,), (1,)), ((), ())), preferred_element_type=F32)


def _rope_chunk(seg, cos, sin_signed, first_half):
    partner = jnp.where(first_half, pltpu.roll(seg, LANES - HEAD_DIM // 2, 1), pltpu.roll(seg, HEAD_DIM // 2, 1))
    return seg * cos + partner * sin_signed


def _rope_wide(y, cos, sin_signed, first_half):
    return jnp.concatenate([_rope_chunk(y[:, c * LANES:(c + 1) * LANES], cos, sin_signed, first_half)
                            for c in range(y.shape[1] // LANES)], axis=1)


def _first_half_mask(shape):
    lane = lax.broadcasted_iota(I32, shape, 1)
    return (lane & (HEAD_DIM - 1)) < HEAD_DIM // 2


def _split_heads(y, out_ref, scale=None):
    for h in range(y.shape[1] // HEAD_DIM):
        v = y[:, h * HEAD_DIM:(h + 1) * HEAD_DIM]
        out_ref[h] = (v if scale is None else v * scale).astype(out_ref.dtype)


def _split_heads_t(y, out_ref):
    yt = y.T
    for h in range(yt.shape[0] // HEAD_DIM):
        out_ref[h] = yt[h * HEAD_DIM:(h + 1) * HEAD_DIM, :].astype(out_ref.dtype)


_AB_AQ, _AB_IQ, _AB_AK, _AB_AV, _AB_BQ, _AB_BK, _AB_BV, _AB_TAIL, _AB_WIDTH = 0, 512, 1024, 1152, 1280, 1792, 2304, 2816, 2944
_A_KV_W = A_KV_HEADS * HEAD_DIM
_B_W = B_HEADS * HEAD_DIM


def _prep_w_ab(w):
    d = w.shape[0]
    sizes = (A_HEADS * HEAD_DIM, _A_KV_W, _A_KV_W, IDX_HEADS * IDX_DIM, IDX_DIM, IDX_HEADS, _B_W, _B_W, _B_W)
    offs = [0]
    for s in sizes:
        offs.append(offs[-1] + s)
    a_q, a_k, a_v, i_q, i_k, i_w, b_q, b_k, b_v = (w[:, offs[i]:offs[i + 1]] for i in range(9))
    pad = jnp.zeros((d, _AB_WIDTH - _AB_TAIL - IDX_DIM - IDX_HEADS), w.dtype)
    return jnp.concatenate([a_q, i_q, a_k, a_v, b_q, b_k, b_v, i_k, i_w, pad], axis=1).astype(BF16)


def _store_kv(y, rows_o, per_batch):
    if per_batch:
        _split_heads_t(y, rows_o)
    else:
        rows_o[...] = y


def _in_ab_body(per_batch, x_ref, g_ref, w_ref, cos_ref, sin_ref,
                aq_o, iq_o, bq_o, ak_o, akt_o, av_o, av16_o, ik_o, ikt_o, iw_o, bk_o, bkt_o, bv_o, bv16_o):
    xn = _rms(x_ref[...], g_ref[...], NORM_EPS).astype(BF16)
    cos = cos_ref[...]
    sin = sin_ref[...]
    first = _first_half_mask(cos.shape)

    def proj(lo, width):
        return _dot(xn, w_ref[:, lo:lo + width])

    _split_heads(_rope_wide(proj(_AB_AQ, 512), cos, sin, first), aq_o, EXP_SCALE)
    _split_heads(_rope_wide(proj(_AB_IQ, 512), cos, sin, first), iq_o, QK_SCALE)
    _split_heads(proj(_AB_BQ, 512), bq_o, EXP_SCALE)

    ak = _rope_chunk(proj(_AB_AK, _A_KV_W), cos, sin, first)
    _store_kv(ak, ak_o, per_batch)
    _split_heads_t(ak, akt_o)
    av = proj(_AB_AV, _A_KV_W)
    _store_kv(av, av_o, per_batch)
    av16_o[...] = av.astype(BF16)

    bk = proj(_AB_BK, _B_W)
    _store_kv(bk, bk_o, per_batch)
    _split_heads_t(bk, bkt_o)
    bv = proj(_AB_BV, _B_W)
    _store_kv(bv, bv_o, per_batch)
    bv16_o[...] = bv.astype(BF16)

    tail = proj(_AB_TAIL, LANES)
    ik = _rope_chunk(tail, cos, sin, first)
    ik_t = ik.T[:IDX_DIM, :]
    ik_o[...] = ik_t if per_batch else ik[:, :IDX_DIM]
    ikt_o[...] = ik_t.astype(BF16)
    iw_o[...] = tail[:, IDX_DIM:IDX_DIM + IDX_HEADS] * (IDX_HEADS ** -0.5)


def _in_proj_ab(x2d, g, w_perm, cos, sin, batch_tokens=None):
    m, d = x2d.shape
    tm = min(ROW_TILE, m)
    n_pos_blocks = cos.shape[0] // tm
    row = lambda i: (i, 0)
    hrow = lambda i: (0, i, 0)
    hcol = lambda i: (0, 0, i)
    pos = lambda i: (i % n_pos_blocks, 0)
    const = lambda i: (0, 0)
    sds = jax.ShapeDtypeStruct
    hm = lambda heads: sds((heads, m, HEAD_DIM), BF16)
    hmt = lambda heads: sds((heads, HEAD_DIM, m), BF16)
    hspec = lambda heads: pl.BlockSpec((heads, tm, HEAD_DIM), hrow)
    htspec = lambda heads: pl.BlockSpec((heads, HEAD_DIM, tm), hcol)
    rspec = lambda width: pl.BlockSpec((tm, width), row)
    per_batch = batch_tokens is not None
    if per_batch:
        nt = batch_tokens // tm
        bsz = m // batch_tokens
        kv = lambda heads: (sds((bsz, heads, HEAD_DIM, batch_tokens), F32),
                            pl.BlockSpec((None, heads, HEAD_DIM, tm), lambda i: (i // nt, 0, 0, i % nt)))
        ik = (sds((bsz, IDX_DIM, batch_tokens), F32), pl.BlockSpec((None, IDX_DIM, tm), lambda i: (i // nt, 0, i % nt)))
    else:
        kv = lambda heads: (sds((m, heads * HEAD_DIM), F32), rspec(heads * HEAD_DIM))
        ik = (sds((m, IDX_DIM), F32), rspec(IDX_DIM))
    outs = (
        (hm(A_HEADS), hspec(A_HEADS)), (hm(IDX_HEADS), hspec(IDX_HEADS)), (hm(B_HEADS), hspec(B_HEADS)),
        kv(A_KV_HEADS), (hmt(A_KV_HEADS), htspec(A_KV_HEADS)), kv(A_KV_HEADS), (sds((m, _A_KV_W), BF16), rspec(_A_KV_W)),
        ik, (sds((IDX_DIM, m), BF16), pl.BlockSpec((IDX_DIM, tm), lambda i: (0, i))),
        (sds((m, IDX_HEADS), F32), rspec(IDX_HEADS)),
        kv(B_HEADS), (hmt(B_HEADS), htspec(B_HEADS)), kv(B_HEADS), (sds((m, _B_W), BF16), rspec(_B_W)),
    )
    return pl.pallas_call(
        functools.partial(_in_ab_body, per_batch),
        grid=(m // tm,),
        in_specs=[pl.BlockSpec((tm, d), row), pl.BlockSpec((1, d), const), pl.BlockSpec((d, _AB_WIDTH), const),
                  pl.BlockSpec((tm, LANES), pos), pl.BlockSpec((tm, LANES), pos)],
        out_specs=tuple(o[1] for o in outs),
        out_shape=tuple(o[0] for o in outs),
        compiler_params=_cparams("parallel"),
        name="in_proj_ab",
    )(x2d, g.reshape(1, d), w_perm, cos, sin)


_C_Q_W = C_HEADS * 2 * HEAD_DIM
_C_KV_W = C_KV_HEADS * 2 * HEAD_DIM


def _store_kv_pairs(y, rows_o, interleaved):
    if interleaved:
        for g in range(C_KV_HEADS):
            rows_o[pl.ds(g, y.shape[0], stride=C_KV_HEADS), :] = y[:, g * LANES:(g + 1) * LANES]
    else:
        rows_o[...] = y


def _in_c_body(interleaved, x_ref, g_ref, w_ref, cos_ref, sin_ref, cq_o, ck_o, ckt_o, cv_o, cv16_o):
    xn = _rms(x_ref[...], g_ref[...], NORM_EPS).astype(BF16)
    cos = cos_ref[...]
    sin = sin_ref[...]
    first = _first_half_mask(cos.shape)

    def proj(lo, width):
        return _dot(xn, w_ref[:, lo:lo + width])

    rep = C_HEADS // C_KV_HEADS
    for block in range(2):
        q = _rope_wide(proj(block * 512, 512), cos, sin, first)
        for i in range(8):
            head, half = divmod(block * 8 + i, 2)
            slot = ((head // rep) * 2 + half) * rep + head % rep
            cq_o[slot] = (q[:, i * HEAD_DIM:(i + 1) * HEAD_DIM] * EXP_SCALE).astype(BF16)
    k = _rope_wide(proj(_C_Q_W, _C_KV_W), cos, sin, first)
    _store_kv_pairs(k, ck_o, interleaved)
    _split_heads_t(k, ckt_o)
    v = proj(_C_Q_W + _C_KV_W, _C_KV_W)
    _store_kv_pairs(v, cv_o, interleaved)
    cv16_o[...] = v.astype(BF16)


def _in_proj_c(x2d, g, w_bf16, cos, sin, interleaved=False):
    m, d = x2d.shape
    tm = min(ROW_TILE, m)
    n_pos_blocks = cos.shape[0] // tm
    row = lambda i: (i, 0)
    hrow = lambda i: (0, i, 0)
    hcol = lambda i: (0, 0, i)
    pos = lambda i: (i % n_pos_blocks, 0)
    const = lambda i: (0, 0)
    width = _C_Q_W + 2 * _C_KV_W
    sds = jax.ShapeDtypeStruct
    if interleaved:
        kv = (sds((m * C_KV_HEADS, LANES), F32), pl.BlockSpec((tm * C_KV_HEADS, LANES), row))
    else:
        kv = (sds((m, _C_KV_W), F32), pl.BlockSpec((tm, _C_KV_W), row))
    outs = (
        (sds((2 * C_HEADS, m, HEAD_DIM), BF16), pl.BlockSpec((2 * C_HEADS, tm, HEAD_DIM), hrow)),
        kv, (sds((2 * C_KV_HEADS, HEAD_DIM, m), BF16), pl.BlockSpec((2 * C_KV_HEADS, HEAD_DIM, tm), hcol)),
        kv, (sds((m, _C_KV_W), BF16), pl.BlockSpec((tm, _C_KV_W), row)),
    )
    return pl.pallas_call(
        functools.partial(_in_c_body, interleaved),
        grid=(m // tm,),
        in_specs=[pl.BlockSpec((tm, d), row), pl.BlockSpec((1, d), const), pl.BlockSpec((d, width), const),
                  pl.BlockSpec((tm, LANES), pos), pl.BlockSpec((tm, LANES), pos)],
        out_specs=tuple(o[1] for o in outs),
        out_shape=tuple(o[0] for o in outs),
        compiler_params=_cparams("parallel"),
        name="in_proj_c",
    )(x2d, g.reshape(1, d), w_bf16, cos, sin)


def _post_body(o_ref, x_ref, wout_ref, gpost_ref, gpre_ref, wup_ref, wdown_ref, gfpost_ref, y_ref):
    mix = _dot(o_ref[...], wout_ref[...])
    x1 = x_ref[...] + _rms(mix, gpost_ref[...], NORM_EPS)
    hn = _rms(x1, gpre_ref[...], NORM_EPS).astype(BF16)
    d_ff = wup_ref.shape[1]
    step = 1024
    f = jnp.zeros(x1.shape, F32)
    for c in range(d_ff // step):
        u = jnp.maximum(_dot(hn, wup_ref[:, c * step:(c + 1) * step]), 0.0)
        f = f + _dot((u * u).astype(BF16), wdown_ref[c * step:(c + 1) * step, :])
    y_ref[...] = x1 + _rms(f, gfpost_ref[...], NORM_EPS)


def _post_block(o2d, x2d, w_out, g_post, g_pre, w_up, w_down, g_fpost):
    m, d = x2d.shape
    tm = min(MLP_ROW_TILE if m % MLP_ROW_TILE == 0 else ROW_TILE, m)
    assert m % tm == 0
    d_ff = w_up.shape[1]
    row = lambda i: (i, 0)
    const = lambda i: (0, 0)
    resident = lambda shape: pl.BlockSpec(shape, const, pipeline_mode=pl.Buffered(1))
    return pl.pallas_call(
        _post_body,
        grid=(m // tm,),
        in_specs=[pl.BlockSpec((tm, o2d.shape[1]), row), pl.BlockSpec((tm, d), row),
                  resident(w_out.shape), pl.BlockSpec((1, d), const), pl.BlockSpec((1, d), const),
                  resident((d, d_ff)), resident((d_ff, d)), pl.BlockSpec((1, d), const)],
        out_specs=pl.BlockSpec((tm, d), row),
        out_shape=jax.ShapeDtypeStruct((m, d), F32),
        compiler_params=_cparams("parallel"),
        name="post_block",
    )(o2d, x2d, w_out, g_post.reshape(1, d), g_pre.reshape(1, d), w_up, w_down, g_fpost.reshape(1, d))


def _order_key(s):
    s = jnp.where(s == 0.0, 0.0, s)
    b = pltpu.bitcast(s, I32)
    return b ^ ((b >> 31) & I32(0x7FFFFFFF))


def _select_bias(sc_ref, key_ref, nch, n_sel, rows, width):
    def fill_keys(j, carry):
        s = sc_ref[j]
        key_ref[j] = _order_key(s)
        return jnp.maximum(carry[0], s), jnp.minimum(carry[1], s)

    lane_max, lane_min = lax.fori_loop(0, nch, fill_keys, (jnp.full((rows, width), -jnp.inf, F32),
                                                           jnp.full((rows, width), jnp.inf, F32)))

    def count(pred):
        def body(j, c):
            hit = jnp.where(pred(key_ref[j]), 1.0, 0.0)
            return c + sum(hit[:, i * LANES:(i + 1) * LANES] for i in range(width // LANES))

        c = lax.fori_loop(0, nch, body, jnp.zeros((rows, LANES), F32))
        return jnp.sum(c, axis=1, keepdims=True)

    def count_ge(cand):
        cand_b = jnp.broadcast_to(cand, (rows, width))
        return count(lambda k: k >= cand_b)

    k_sel = float(n_sel)
    low_score = jnp.min(lane_max if n_sel <= width else lane_min, axis=1, keepdims=True)
    lo = _order_key(low_score)
    hi = _order_key(jnp.max(lane_max, axis=1, keepdims=True))
    lo_b = jnp.broadcast_to(lo, (rows, width))
    hi = jnp.where(count(lambda k: k > lo_b) < k_sel, lo, hi)

    def open_rows(lo, hi):
        return jnp.max(jnp.where(lo < hi, 1.0, 0.0))

    def bisect(carry):
        lo, hi, _ = carry
        mid = (lo | hi) - ((lo ^ hi) >> 1)
        n_ge = count_ge(mid)
        lo = jnp.where(n_ge >= k_sel, mid, lo)
        hi = jnp.where(n_ge > k_sel, hi, jnp.where(n_ge == k_sel, mid, mid - 1))
        return lo, hi, open_rows(lo, hi)

    thr, _, _ = lax.while_loop(lambda carry: carry[2] > 0.5, bisect, (lo, hi, open_rows(lo, hi)))
    thr_b = jnp.broadcast_to(thr, (rows, width))
    need_b = jnp.broadcast_to(k_sel - count(lambda k: k > thr_b), (rows, width))
    r = lax.broadcasted_iota(I32, (width, width), 0)
    c = lax.broadcasted_iota(I32, (width, width), 1)
    upper = jnp.where(r < c, 1.0, 0.0).astype(BF16)
    ones = jnp.ones((width, width), BF16)

    def tie_body(j, run):
        k = key_ref[j]
        eq = jnp.where(k == thr_b, 1.0, 0.0)
        eq16 = eq.astype(BF16)
        rank = run + _dot(eq16, upper)
        bias = jnp.where(k > thr_b, 0.0, jnp.where(eq * jnp.where(rank < need_b, 1.0, 0.0) > 0.5, 0.0, MASKED))
        sc_ref[j] = jnp.where(sc_ref[j] == -jnp.inf, MASKED, bias)
        return run + _dot(eq16, ones)

    lax.fori_loop(0, nch, tie_body, jnp.zeros((rows, width), F32))


def _weighted_values(p16, v16, v_transposed):
    return _dot_nt(p16, v16) if v_transposed else _dot(p16, v16)


def _softmax_step(s, values, m, l, acc, v_transposed=False):
    m_new = jnp.maximum(m, jnp.max(s, axis=1, keepdims=True))
    alpha = jnp.exp2(m - m_new)
    p = jnp.exp2(s - m_new)
    l = alpha * l + jnp.sum(p, axis=1, keepdims=True)
    p16 = p.astype(BF16)
    if not isinstance(values, (list, tuple)):
        pv = _weighted_values(p16, values, v_transposed)
    else:
        pv = sum(_weighted_values(p16[:, i * LANES:(i + 1) * LANES], v, v_transposed) for i, v in enumerate(values))
    return m_new, l, alpha * acc + pv


def _stick_matrices():
    r = lax.broadcasted_iota(I32, (2 * LANES, 2 * LANES), 0) & (LANES - 1)
    c = lax.broadcasted_iota(I32, (2 * LANES, 2 * LANES), 1)
    return jnp.where((c >= LANES) | (r > c), 1.0, 0.0).astype(BF16)


def _softmax_ones_step(s, v_ext16, m, la):
    tiles = s.shape[1] // LANES
    m_new = jnp.maximum(m, jnp.max(s.astype(BF16), axis=1, keepdims=True).astype(F32))
    alpha = jnp.exp2(m - m_new)
    p16 = jnp.exp2(s - jnp.concatenate([m_new] * tiles, axis=1)).astype(BF16)
    return m_new, jnp.concatenate([alpha] * (la.shape[1] // LANES), axis=1) * la + _dot(p16, v_ext16)


def _stick_weights(zs, valid, run, mats):
    rows = zs[0].shape[0]
    softplus = [jnp.maximum(z, 0.0) + jnp.log2(1.0 + jnp.exp2(-jnp.abs(z))) for z in zs]
    split = []
    for sp in softplus:
        log_stay = -sp if valid is None else jnp.where(valid, -sp, 0.0)
        hi = log_stay.astype(BF16)
        lo = (log_stay - hi.astype(F32)).astype(BF16)
        split.append(jnp.concatenate([hi, lo], axis=1))
    sums = _dot(split[0] if len(split) == 1 else jnp.concatenate(split, axis=0), mats)
    ws = []
    for n, (z, sp) in enumerate(zip(zs, softplus)):
        part = sums[n * rows:(n + 1) * rows]
        w = jnp.exp2((z - sp) + part[:, :LANES] + run)
        if valid is not None:
            w = jnp.where(valid, w, 0.0)
        ws.append(w.astype(BF16))
        run = run + part[:, LANES:]
    return run, ws


def _chunk_count(qi, qb=QB):
    return lax.shift_right_logical((qi + 1) * qb - 1, KC.bit_length() - 1) + 1


def _chunk_lanes(j):
    return pl.ds(pl.multiple_of(j * KC, KC), KC)


def _att0_prompt_body(n_sel, aq_ref, iq_ref, iw_ref, bq_ref, akt_ref, av_ref, ikt_ref, bkt_ref, bv_ref,
                      o_ref, sc_ref, key_ref, m_ref, la_ref, run_ref, acc_ref):
    qi = pl.program_id(1)
    n_kc = _chunk_count(qi)
    row = lax.broadcasted_iota(I32, (QB, KC), 0)
    col = lax.broadcasted_iota(I32, (QB, KC), 1)
    qpos = qi * QB + row

    def score_chunk(j, c):
        kt = ikt_ref[:, _chunk_lanes(j)]
        acc = jnp.zeros((QB, KC), F32)
        for h in range(IDX_HEADS):
            acc = acc + jnp.maximum(_dot(iq_ref[h], kt), 0.0) * iw_ref[:, h:h + 1]
        sc_ref[j] = jnp.where(j * KC + col <= qpos, acc, -jnp.inf)
        return c

    lax.fori_loop(0, n_kc, score_chunk, 0)
    _select_bias(sc_ref, key_ref, n_kc, n_sel, QB, KC)

    m_ref[...] = jnp.full_like(m_ref, MASKED)
    la_ref[...] = jnp.zeros_like(la_ref)
    rep = A_HEADS // A_KV_HEADS
    ones = jnp.ones((KC, LANES), BF16)

    def dsa_chunk(j, c):
        bias = jnp.concatenate([sc_ref[j]] * rep, axis=0)
        v_ext = jnp.concatenate([av_ref[_chunk_lanes(j), :], ones], axis=1)
        for g in range(A_KV_HEADS):
            q = aq_ref[rep * g:rep * (g + 1)].reshape(rep * QB, HEAD_DIM)
            s = _dot(q, akt_ref[g, :, _chunk_lanes(j)]) + bias
            m_ref[g], la_ref[g] = _softmax_ones_step(s, v_ext, m_ref[g], la_ref[g])
        return c

    lax.fori_loop(0, n_kc, dsa_chunk, 0)
    outs = []
    for h in range(A_HEADS):
        g, r = h // rep, h % rep
        la = la_ref[g][r * QB:(r + 1) * QB]
        outs.append(la[:, g * HEAD_DIM:(g + 1) * HEAD_DIM] / la[:, LANES:LANES + HEAD_DIM])
    o_ref[:, :A_HEADS * HEAD_DIM] = jnp.concatenate(outs, axis=1).astype(o_ref.dtype)

    run_ref[...] = jnp.zeros_like(run_ref)
    acc_ref[...] = jnp.zeros_like(acc_ref)
    mats = _stick_matrices()
    rows_all = B_HEADS * QB
    qpos_c = qi * QB + (lax.broadcasted_iota(I32, (rows_all, LANES), 0) & (QB - 1))
    col_c = lax.broadcasted_iota(I32, (rows_all, LANES), 1)

    def sb_keys(starts, on_diagonal):
        keys = [pl.ds(pl.multiple_of(start, LANES), LANES) for start in starts]
        zs = [jnp.concatenate([_dot(bq_ref[h], bkt_ref[h, :, k]) for h in range(B_HEADS)], axis=0) for k in keys]
        valid = (starts[0] + col_c < qpos_c) if on_diagonal else None
        run_ref[...], ws = _stick_weights(zs, valid, run_ref[...], mats)
        for pair in range(B_HEADS // 2):
            rows = slice(2 * pair * QB, 2 * (pair + 1) * QB)
            acc_ref[pair] += sum(_dot(w16[rows], bv_ref[k, pair * LANES:(pair + 1) * LANES]) for w16, k in zip(ws, keys))

    n_full = qi * (QB // KC)
    for sub in reversed(range(QB // LANES)):
        sb_keys([qi * QB + sub * LANES], True)

    def sb_live():
        return jnp.max(run_ref[...]) > STICK_UNDERFLOW_LOG2

    def sb_chunk(carry):
        i = carry[0]
        j = n_full - 1 - i
        sb_keys([j * KC + sub * LANES for sub in reversed(range(KC // LANES))], False)
        return i + 1, sb_live()

    lax.while_loop(lambda carry: (carry[0] < n_full) & carry[1], sb_chunk, (I32(0), sb_live()))
    outs = [acc_ref[h // 2][(h % 2) * QB:(h % 2 + 1) * QB, (h % 2) * HEAD_DIM:(h % 2 + 1) * HEAD_DIM]
            for h in range(B_HEADS)]
    o_ref[:, A_HEADS * HEAD_DIM:] = jnp.concatenate(outs, axis=1).astype(o_ref.dtype)


def _att0_prompt(aq, iq, iw, bq, akt, av16, ikt, bkt, bv16, bsz, t):
    nq = t // QB
    n_sel = min(TOPK_MAX, t // 4)
    qh = lambda b, q: (0, b * nq + q, 0)
    qrow = lambda b, q: (b * nq + q, 0)
    kt = lambda b, q: (0, 0, b)
    krow = lambda b, q: (b, 0)
    width = (A_HEADS + B_HEADS) * HEAD_DIM
    rows_a = (A_HEADS // A_KV_HEADS) * QB
    return pl.pallas_call(
        functools.partial(_att0_prompt_body, n_sel),
        grid=(bsz, nq),
        in_specs=[pl.BlockSpec((A_HEADS, QB, HEAD_DIM), qh), pl.BlockSpec((IDX_HEADS, QB, IDX_DIM), qh),
                  pl.BlockSpec((QB, IDX_HEADS), qrow), pl.BlockSpec((B_HEADS, QB, HEAD_DIM), qh),
                  pl.BlockSpec((A_KV_HEADS, HEAD_DIM, t), kt), pl.BlockSpec((t, _A_KV_W), krow),
                  pl.BlockSpec((IDX_DIM, t), lambda b, q: (0, b)),
                  pl.BlockSpec((B_HEADS, HEAD_DIM, t), kt), pl.BlockSpec((t, _B_W), krow)],
        out_specs=pl.BlockSpec((QB, width), qrow),
        out_shape=jax.ShapeDtypeStruct((bsz * t, width), BF16),
        scratch_shapes=[pltpu.VMEM((t // KC, QB, KC), F32), pltpu.VMEM((t // KC, QB, KC), I32),
                        pltpu.VMEM((A_KV_HEADS, rows_a, LANES), F32), pltpu.VMEM((A_KV_HEADS, rows_a, 2 * LANES), F32),
                        pltpu.VMEM((B_HEADS * QB, LANES), F32), pltpu.VMEM((B_HEADS // 2, 2 * QB, LANES), F32)],
        compiler_params=_cparams("parallel", "arbitrary"),
        name="att0_prompt",
    )(aq, iq, iw, bq, akt, av16, ikt, bkt, bv16)


def _diff_lambda(lq1_ref, lk1_ref, lq2_ref, lk2_ref, lambda_init):
    s1 = jnp.sum(lq1_ref[...] * lk1_ref[...], axis=1, keepdims=True)
    s2 = jnp.sum(lq2_ref[...] * lk2_ref[...], axis=1, keepdims=True)
    return jnp.exp(s1) - jnp.exp(s2) + lambda_init


def _diff_finish(a1, l1, a2, l2, lam, gsub, lambda_init):
    o = a1 / l1 - lam * (a2 / l2)
    return _rms(o, gsub, SUBLN_EPS) * (1.0 - lambda_init)


def _att1_prompt_body(lambda_init, cq_ref, ckt_ref, cv_ref, lq1_ref, lk1_ref, lq2_ref, lk2_ref, gsub_ref,
                      o_ref, m_ref, la_ref):
    qi = pl.program_id(1)
    qb = o_ref.shape[0]
    rep = C_HEADS // C_KV_HEADS
    rows = rep * qb
    n_kc = _chunk_count(qi, qb)
    row = lax.broadcasted_iota(I32, (rows, KC), 0)
    col = lax.broadcasted_iota(I32, (rows, KC), 1)
    qpos = qi * qb + (row & (qb - 1))
    m_ref[...] = jnp.full_like(m_ref, MASKED)
    la_ref[...] = jnp.zeros_like(la_ref)
    ones = jnp.ones((KC, LANES), BF16)

    def chunk(j, on_diagonal):
        bias = jnp.where(j * KC + col <= qpos, 0.0, MASKED) if on_diagonal else None
        for g in range(C_KV_HEADS):
            v_ext = jnp.concatenate([cv_ref[_chunk_lanes(j), g * LANES:(g + 1) * LANES], ones], axis=1)
            for half in range(2):
                u = 2 * g + half
                q = cq_ref[rep * u:rep * (u + 1)].reshape(rows, HEAD_DIM)
                s = _dot(q, ckt_ref[u, :, _chunk_lanes(j)])
                if on_diagonal:
                    s = s + bias
                m_ref[u], la_ref[u] = _softmax_ones_step(s, v_ext, m_ref[u], la_ref[u])

    def full_chunk(j, c):
        chunk(j, False)
        return c

    lax.fori_loop(0, n_kc - 1, full_chunk, 0)
    chunk(n_kc - 1, True)

    lam = _diff_lambda(lq1_ref, lk1_ref, lq2_ref, lk2_ref, lambda_init)
    gsub = gsub_ref[...]
    for h in range(C_HEADS):
        g, r = h // rep, h % rep
        la1 = la_ref[2 * g][r * qb:(r + 1) * qb]
        la2 = la_ref[2 * g + 1][r * qb:(r + 1) * qb]
        o = _diff_finish(la1[:, :LANES], la1[:, LANES:], la2[:, :LANES], la2[:, LANES:], lam, gsub, lambda_init)
        o_ref[:, h * LANES:(h + 1) * LANES] = o.astype(o_ref.dtype)


def _att1_prompt(cq, ckt, cv16, lq1, lk1, lq2, lk2, gsub, lambda_init, bsz, t):
    qb = QB_DIFF
    nq = t // qb
    qh = lambda b, q: (0, b * nq + q, 0)
    qrow = lambda b, q: (b * nq + q, 0)
    const = lambda b, q: (0, 0)
    vec = lambda a: a.reshape(1, -1)
    rows = (C_HEADS // C_KV_HEADS) * qb
    return pl.pallas_call(
        functools.partial(_att1_prompt_body, lambda_init),
        grid=(bsz, nq),
        in_specs=[pl.BlockSpec((2 * C_HEADS, qb, HEAD_DIM), qh),
                  pl.BlockSpec((2 * C_KV_HEADS, HEAD_DIM, t), lambda b, q: (0, 0, b)),
                  pl.BlockSpec((t, _C_KV_W), lambda b, q: (b, 0)),
                  pl.BlockSpec((1, HEAD_DIM), const), pl.BlockSpec((1, HEAD_DIM), const),
                  pl.BlockSpec((1, HEAD_DIM), const), pl.BlockSpec((1, HEAD_DIM), const),
                  pl.BlockSpec((1, LANES), const)],
        out_specs=pl.BlockSpec((qb, _C_Q_W), qrow),
        out_shape=jax.ShapeDtypeStruct((bsz * t, _C_Q_W), BF16),
        scratch_shapes=[pltpu.VMEM((2 * C_KV_HEADS, rows, LANES), F32),
                        pltpu.VMEM((2 * C_KV_HEADS, rows, 2 * LANES), F32)],
        compiler_params=_cparams("parallel", "arbitrary"),
        name="att1_prompt",
    )(cq, ckt, cv16, vec(lq1), vec(lk1), vec(lq2), vec(lk2), vec(gsub))


def _page_specs(block, n_groups, group, newest_first):
    zeros = (0,) * (len(block) - 2)

    def index(i):
        if newest_first:
            return lambda b, s, pt: (0, pt[b, jnp.minimum(n_groups - s, n_groups - 1) * group + i]) + zeros
        return lambda b, s, pt: (0, pt[b, jnp.minimum(s, n_groups - 1) * group + i]) + zeros

    return [pl.BlockSpec(block, index(i)) for i in range(group)]


def _keys_on_lanes(cache):
    nd = cache.ndim
    return jnp.transpose(cache, (0, 1) + tuple(range(3, nd)) + (2,))


def _att0_sample_walk_body(dec, group, pt_ref, iq_ref, iw_ref, bq_ref, ikn_ref, bkn_ref, bvn_ref, *refs):
    del pt_ref
    ikc_refs, bkc_refs, bvc_refs = refs[:group], refs[group:2 * group], refs[2 * group:3 * group]
    sc_o, ob_o, run_ref, acc_ref = refs[3 * group:]
    s = pl.program_id(1)
    rows = B_HEADS * dec
    row = lax.broadcasted_iota(I32, (rows, LANES), 0)
    col = lax.broadcasted_iota(I32, (rows, LANES), 1)
    t_of_row = row & (dec - 1)
    mats = _stick_matrices()

    def scores(kt16):
        lg = jnp.maximum(_dot(iq_ref[...], kt16), 0.0) * iw_ref[...]
        out = lg[0:dec]
        for h in range(1, IDX_HEADS):
            out = out + lg[h * dec:(h + 1) * dec]
        return out

    def stick(kts, vts, valid):
        run_ref[...], ws = _stick_weights([_dot(bq_ref[...], kt) for kt in kts], valid, run_ref[...], mats)
        acc_ref[...] += sum(_dot_nt(w16, vt) for w16, vt in zip(ws, vts))

    @pl.when(s == 0)
    def _():
        run_ref[...] = jnp.zeros_like(run_ref)
        acc_ref[...] = jnp.zeros_like(acc_ref)
        causal = (col[:dec] < dec) & (col[:dec] <= row[:dec])
        sc_o[0] = jnp.where(causal, scores(ikn_ref[...]), -jnp.inf)
        for i in range(1, group):
            sc_o[i] = jnp.full((dec, LANES), -jnp.inf, F32)
        stick([bkn_ref[...]], [bvn_ref[...]], (col < dec) & (col < t_of_row))

    @pl.when(s > 0)
    def _():
        for i in range(group):
            sc_o[i] = scores(ikc_refs[i][...].astype(BF16))

        @pl.when(jnp.max(run_ref[...]) > STICK_UNDERFLOW_LOG2)
        def _():
            stick([bkc_refs[i][...].reshape(_B_W, LANES).astype(BF16) for i in reversed(range(group))],
                  [bvc_refs[i][...].reshape(_B_W, LANES).astype(BF16) for i in reversed(range(group))], None)

    @pl.when(s == pl.num_programs(1) - 1)
    def _():
        acc = acc_ref[...]
        ob_o[...] = jnp.concatenate([acc[h * dec:(h + 1) * dec, h * HEAD_DIM:(h + 1) * HEAD_DIM]
                                     for h in range(B_HEADS)], axis=1).astype(ob_o.dtype)


def _att0_sample_walk(page_table, iq_st, iw_st, bq_bd, ikt_new, bkt_new, bvt_new, cache_kidx, cache_bk, cache_bv):
    db, n_pages = page_table.shape
    group = math.gcd(PAGE_GROUP, n_pages)
    n_groups = n_pages // group
    dec = iq_st.shape[1] // IDX_HEADS
    rows = B_HEADS * dec
    per_b = lambda b, s, pt: (b, 0, 0)
    chunks_of_step = lambda b, s, pt: (jnp.where(s == 0, n_groups, n_groups - s), b, 0, 0)
    b_page = (None, None, B_HEADS, HEAD_DIM, LANES)
    grid_spec = pltpu.PrefetchScalarGridSpec(
        num_scalar_prefetch=1,
        grid=(db, n_groups + 1),
        in_specs=[pl.BlockSpec((None, rows, IDX_DIM), per_b), pl.BlockSpec((None, rows, 1), per_b),
                  pl.BlockSpec((None, rows, _B_W), per_b),
                  pl.BlockSpec((None, IDX_DIM, LANES), per_b), pl.BlockSpec((None, _B_W, LANES), per_b),
                  pl.BlockSpec((None, _B_W, LANES), per_b)]
        + _page_specs((None, None, IDX_DIM, LANES), n_groups, group, True)
        + _page_specs(b_page, n_groups, group, True) + _page_specs(b_page, n_groups, group, True),
        out_specs=[pl.BlockSpec((group, None, dec, LANES), chunks_of_step), pl.BlockSpec((None, dec, _B_W), per_b)],
        scratch_shapes=[pltpu.VMEM((rows, LANES), F32), pltpu.VMEM((rows, _B_W), F32)],
    )
    return pl.pallas_call(
        functools.partial(_att0_sample_walk_body, dec, group),
        grid_spec=grid_spec,
        out_shape=(jax.ShapeDtypeStruct(((n_groups + 1) * group, db, dec, LANES), F32),
                   jax.ShapeDtypeStruct((db, dec, _B_W), BF16)),
        compiler_params=_cparams("parallel", "arbitrary"),
        name="att0_sample_walk",
    )(page_table, iq_st, iw_st, bq_bd, ikt_new, bkt_new, bvt_new,
      *([cache_kidx] * group + [cache_bk] * group + [cache_bv] * group))


def _select_sample_body(n_sel, nch, sc_ref, bias_o, key_ref):
    rows = sc_ref.shape[1]
    bias_o[...] = sc_ref[...]
    _select_bias(bias_o, key_ref, nch, n_sel, rows, LANES)


def _select_sample(scores, nch, n_sel):
    n_alloc, m, _ = scores.shape
    rows = min(m, 64)
    spec = pl.BlockSpec((n_alloc, rows, LANES), lambda i: (0, i, 0))
    return pl.pallas_call(
        functools.partial(_select_sample_body, n_sel, nch),
        grid=(m // rows,),
        in_specs=[spec],
        out_specs=spec,
        out_shape=jax.ShapeDtypeStruct(scores.shape, F32),
        scratch_shapes=[pltpu.VMEM((n_alloc, rows, LANES), I32)],
        compiler_params=_cparams("parallel"),
        name="select_sample",
    )(scores)


def _att0_sample_dsa_body(n_groups, dec, group, pt_ref, aq_ref, bias_ref, akn_ref, avn_ref, *refs):
    del pt_ref
    akc_refs, avc_refs = refs[:group], refs[group:2 * group]
    oa_o, m_ref, l_ref, acc_ref = refs[2 * group:]
    s = pl.program_id(1)

    @pl.when(s == 0)
    def _():
        m_ref[...] = jnp.full_like(m_ref, MASKED)
        l_ref[...] = jnp.zeros_like(l_ref)
        acc_ref[...] = jnp.zeros_like(acc_ref)

    def step(kts, vts):
        n = len(kts)
        bias = jnp.concatenate([jnp.concatenate([bias_ref[i]] * A_HEADS, axis=0) for i in range(n)], axis=1)
        sc = jnp.concatenate([_dot(aq_ref[...], kt) for kt in kts], axis=1) + bias
        m_ref[...], l_ref[...], acc_ref[...] = _softmax_step(sc, vts, m_ref[...], l_ref[...], acc_ref[...],
                                                             v_transposed=True)

    @pl.when(s < n_groups)
    def _():
        step([r[...].reshape(_A_KV_W, LANES).astype(BF16) for r in akc_refs],
             [r[...].reshape(_A_KV_W, LANES).astype(BF16) for r in avc_refs])

    @pl.when(s == n_groups)
    def _():
        step([akn_ref[...]], [avn_ref[...]])
        o = acc_ref[...] / l_ref[...]
        rep = A_HEADS // A_KV_HEADS
        oa_o[...] = jnp.concatenate(
            [o[h * dec:(h + 1) * dec, (h // rep) * HEAD_DIM:(h // rep + 1) * HEAD_DIM] for h in range(A_HEADS)],
            axis=1).astype(oa_o.dtype)


def _att0_sample_dsa(page_table, aq_bd, bias, akt_new, avt_new, cache_akt, cache_avt):
    db, n_pages = page_table.shape
    group = math.gcd(PAGE_GROUP, n_pages)
    n_groups = n_pages // group
    rows = aq_bd.shape[1]
    dec = rows // A_HEADS
    per_b = lambda b, s, pt: (b, 0, 0)
    a_page = (None, None, A_KV_HEADS, HEAD_DIM, LANES)
    grid_spec = pltpu.PrefetchScalarGridSpec(
        num_scalar_prefetch=1,
        grid=(db, n_groups + 1),
        in_specs=[pl.BlockSpec((None, rows, _A_KV_W), per_b),
                  pl.BlockSpec((group, None, dec, LANES), lambda b, s, pt: (s, b, 0, 0)),
                  pl.BlockSpec((None, _A_KV_W, LANES), per_b), pl.BlockSpec((None, _A_KV_W, LANES), per_b)]
        + _page_specs(a_page, n_groups, group, False) + _page_specs(a_page, n_groups, group, False),
        out_specs=pl.BlockSpec((None, dec, A_HEADS * HEAD_DIM), per_b),
        scratch_shapes=[pltpu.VMEM((rows, 1), F32), pltpu.VMEM((rows, 1), F32), pltpu.VMEM((rows, _A_KV_W), F32)],
    )
    return pl.pallas_call(
        functools.partial(_att0_sample_dsa_body, n_groups, dec, group),
        grid_spec=grid_spec,
        out_shape=jax.ShapeDtypeStruct((db, dec, A_HEADS * HEAD_DIM), BF16),
        compiler_params=_cparams("parallel", "arbitrary"),
        name="att0_sample_dsa",
    )(page_table, aq_bd, bias, akt_new, avt_new, *([cache_akt] * group + [cache_avt] * group))


def _att1_sample_body(n_groups, dec, group, lambda_init, pt_ref, cq_ref, ckn_ref, cvn_ref, *refs):
    del pt_ref
    ckc_refs, cvc_refs = refs[:group], refs[group:2 * group]
    lq1_ref, lk1_ref, lq2_ref, lk2_ref, gsub_ref, o_o, m_ref, l_ref, acc_ref = refs[2 * group:]
    s = pl.program_id(1)
    rows_g = 2 * (C_HEADS // C_KV_HEADS) * dec

    @pl.when(s == 0)
    def _():
        m_ref[...] = jnp.full_like(m_ref, MASKED)
        l_ref[...] = jnp.zeros_like(l_ref)
        acc_ref[...] = jnp.zeros_like(acc_ref)

    def step(keys_of, values_of, n_pages, bias):
        sc = jnp.concatenate(
            [jnp.concatenate([_dot_nt(cq_ref[g * rows_g:(g + 1) * rows_g, :], keys_of(g, i)) for i in range(n_pages)],
                             axis=1) for g in range(C_KV_HEADS)], axis=0)
        if bias is not None:
            sc = sc + bias
        m = m_ref[...]
        m_new = jnp.maximum(m, jnp.max(sc, axis=1, keepdims=True))
        alpha = jnp.exp2(m - m_new)
        p = jnp.exp2(sc - m_new)
        p16 = p.astype(BF16)
        pv = jnp.concatenate(
            [sum(_dot(p16[g * rows_g:(g + 1) * rows_g, i * LANES:(i + 1) * LANES], values_of(g, i))
                 for i in range(n_pages)) for g in range(C_KV_HEADS)], axis=0)
        m_ref[...] = m_new
        l_ref[...] = alpha * l_ref[...] + jnp.sum(p, axis=1, keepdims=True)
        acc_ref[...] = alpha * acc_ref[...] + pv

    @pl.when(s < n_groups)
    def _():
        head_rows = [pl.ds(g, LANES, stride=C_KV_HEADS) for g in range(C_KV_HEADS)]
        step(lambda g, i: ckc_refs[i][head_rows[g], :].astype(BF16),
             lambda g, i: cvc_refs[i][head_rows[g], :].astype(BF16), group, None)

    @pl.when(s == n_groups)
    def _():
        rows = C_KV_HEADS * rows_g
        row = lax.broadcasted_iota(I32, (rows, LANES), 0)
        col = lax.broadcasted_iota(I32, (rows, LANES), 1)
        bias = jnp.where(col <= (row & (dec - 1)), 0.0, MASKED)
        step(lambda g, i: ckn_ref[:, g * LANES:(g + 1) * LANES],
             lambda g, i: cvn_ref[:, g * LANES:(g + 1) * LANES], 1, bias)
        lam = _diff_lambda(lq1_ref, lk1_ref, lq2_ref, lk2_ref, lambda_init)
        acc = acc_ref[...]
        l = l_ref[...]
        outs = []
        rep = C_HEADS // C_KV_HEADS
        for h in range(C_HEADS):
            g, r = h // rep, h % rep
            r1 = ((2 * g) * rep + r) * dec
            r2 = ((2 * g + 1) * rep + r) * dec
            outs.append(_diff_finish(acc[r1:r1 + dec], l[r1:r1 + dec], acc[r2:r2 + dec], l[r2:r2 + dec],
                                     lam, gsub_ref[...], lambda_init))
        o_o[...] = jnp.concatenate(outs, axis=1).astype(o_o.dtype)


def _att1_sample(page_table, cq_bd, ck_new, cv_new, cache_ck, cache_cv, lq1, lk1, lq2, lk2, gsub, lambda_init):
    db, n_pages = page_table.shape
    group = math.gcd(PAGE_GROUP, n_pages)
    n_groups = n_pages // group
    rows = cq_bd.shape[1]
    dec = rows // (2 * C_HEADS)
    per_b = lambda b, s, pt: (b, 0, 0)
    const = lambda b, s, pt: (0, 0)
    vec = lambda a: a.reshape(1, -1)
    c_page = (None, None, LANES * C_KV_HEADS, LANES)
    grid_spec = pltpu.PrefetchScalarGridSpec(
        num_scalar_prefetch=1,
        grid=(db, n_groups + 1),
        in_specs=[pl.BlockSpec((None, rows, LANES), per_b),
                  pl.BlockSpec((None, LANES, _C_KV_W), per_b), pl.BlockSpec((None, LANES, _C_KV_W), per_b)]
        + _page_specs(c_page, n_groups, group, False) + _page_specs(c_page, n_groups, group, False)
        + [pl.BlockSpec((1, HEAD_DIM), const), pl.BlockSpec((1, HEAD_DIM), const),
           pl.BlockSpec((1, HEAD_DIM), const), pl.BlockSpec((1, HEAD_DIM), const), pl.BlockSpec((1, LANES), const)],
        out_specs=pl.BlockSpec((None, dec, _C_Q_W), per_b),
        scratch_shapes=[pltpu.VMEM((rows, 1), F32), pltpu.VMEM((rows, 1), F32), pltpu.VMEM((rows, LANES), F32)],
    )
    return pl.pallas_call(
        functools.partial(_att1_sample_body, n_groups, dec, group, lambda_init),
        grid_spec=grid_spec,
        out_shape=jax.ShapeDtypeStruct((db, dec, _C_Q_W), BF16),
        compiler_params=_cparams("parallel", "arbitrary"),
        name="att1_sample",
    )(page_table, cq_bd, ck_new, cv_new, *([cache_ck] * group + [cache_cv] * group),
      vec(lq1), vec(lk1), vec(lq2), vec(lk2), vec(gsub))


def _rope_tables(pos):
    half = HEAD_DIM // 2
    inv_freq = ROPE_THETA ** (-jnp.arange(half, dtype=F32) / half)
    ang = pos.astype(F32)[:, None] * inv_freq[None, :]
    cos, sin = jnp.cos(ang), jnp.sin(ang)
    return jnp.concatenate([cos] * 4, axis=1), jnp.concatenate([-sin, sin, -sin, sin], axis=1)


def _block_diag_rows(q_hm, db, dec, slot_of_head, n_slots):
    heads, _, w = q_hm.shape
    q = q_hm.reshape(heads, db, dec, w).transpose(1, 0, 2, 3)
    onehot = (jnp.asarray(slot_of_head)[:, None] == jnp.arange(n_slots)[None, :]).astype(q.dtype)
    return (q[:, :, :, None, :] * onehot[None, :, None, :, None]).reshape(db, heads * dec, n_slots * w)


def kernel(x_prompt, x_sample, cache_a_k, cache_a_v, cache_a_kidx, cache_b_k, cache_b_v, cache_c_k, cache_c_v,
           page_table, norm_mix_pre, norm_mix_post, norm_ffn_pre, norm_ffn_post, w_in_ab, w_out_ab, w_in_c, w_out_c,
           c_lambda_q1, c_lambda_k1, c_lambda_q2, c_lambda_k2, c_subln, w_up, w_down):
    bsz, t, d = x_prompt.shape
    db, dec, _ = x_sample.shape
    n_phys, page = cache_a_k.shape[1], cache_a_k.shape[2]
    n_pages = page_table.shape[1]
    past = n_pages * page
    assert page == LANES and t % KC == 0 and dec & (dec - 1) == 0 and dec <= 8
    mp, ms = bsz * t, db * dec

    cos_p, sin_p = _rope_tables(jnp.arange(t, dtype=jnp.int32))
    cos_s, sin_s = _rope_tables(jnp.arange(past, past + dec, dtype=jnp.int32))
    cos_s, sin_s = jnp.tile(cos_s, (db, 1)), jnp.tile(sin_s, (db, 1))

    xp = x_prompt.reshape(mp, d)
    xs = x_sample.reshape(ms, d)
    w_ab = _prep_w_ab(w_in_ab[0])
    w_out_ab16 = w_out_ab[0].astype(BF16)
    w_c16 = w_in_c[0].astype(BF16)
    w_out_c16 = w_out_c[0].astype(BF16)
    w_up16 = w_up.astype(BF16)
    w_down16 = w_down.astype(BF16)
    lambda_init = 0.8 - 0.6 * math.exp(-0.3 * 1)

    (aq_p, iq_p, bq_p, ak_p, akt_p, av_p, av16_p, ik_p, ikt_p, iw_p, bk_p, bkt_p, bv_p, bv16_p) = _in_proj_ab(
        xp, norm_mix_pre[0], w_ab, cos_p, sin_p, batch_tokens=t)
    o_p = _att0_prompt(aq_p, iq_p, iw_p, bq_p, akt_p, av16_p, ikt_p, bkt_p, bv16_p, bsz, t)
    xp = _post_block(o_p, xp, w_out_ab16, norm_mix_post[0], norm_ffn_pre[0], w_up16[0], w_down16[0], norm_ffn_post[0])

    (aq_s, iq_s, bq_s, ak_s, _, av_s, _, ik_s, _, iw_s, bk_s, _, bv_s, _) = _in_proj_ab(
        xs, norm_mix_pre[0], w_ab, cos_s, sin_s)
    iq_st = iq_s.reshape(IDX_HEADS, db, dec, IDX_DIM).transpose(1, 0, 2, 3).reshape(db, IDX_HEADS * dec, IDX_DIM)
    iw_st = iw_s.reshape(db, dec, IDX_HEADS).transpose(0, 2, 1).reshape(db, IDX_HEADS * dec, 1)
    bq_bd = _block_diag_rows(bq_s, db, dec, list(range(B_HEADS)), B_HEADS)
    aq_bd = _block_diag_rows(aq_s, db, dec, [h // (A_HEADS // A_KV_HEADS) for h in range(A_HEADS)], A_KV_HEADS)
    new16 = lambda a: jnp.pad(a.reshape(db, dec, -1).astype(BF16), ((0, 0), (0, LANES - dec), (0, 0)))
    new16t = lambda a: jnp.pad(a.reshape(db, dec, -1).astype(BF16).transpose(0, 2, 1), ((0, 0), (0, 0), (0, LANES - dec)))
    scores, ob_s = _att0_sample_walk(
        page_table, iq_st, iw_st, bq_bd, new16t(ik_s), new16t(bk_s), new16t(bv_s),
        _keys_on_lanes(cache_a_kidx), _keys_on_lanes(cache_b_k), _keys_on_lanes(cache_b_v))
    n_sel_s = min(TOPK_MAX, (past + dec) // 4)
    n_alloc = scores.shape[0]
    bias = _select_sample(scores.reshape(n_alloc, ms, LANES), n_pages + 1, n_sel_s).reshape(n_alloc, db, dec, LANES)
    oa_s = _att0_sample_dsa(page_table, aq_bd, bias, new16t(ak_s), new16t(av_s),
                            _keys_on_lanes(cache_a_k), _keys_on_lanes(cache_a_v))
    o_s = jnp.concatenate([oa_s, ob_s], axis=-1).reshape(ms, -1)
    xs = _post_block(o_s, xs, w_out_ab16, norm_mix_post[0], norm_ffn_pre[0], w_up16[0], w_down16[0], norm_ffn_post[0])

    cq_p, ck_p, ckt_p, cv_p, cv16_p = _in_proj_c(xp, norm_mix_pre[1], w_c16, cos_p, sin_p, interleaved=True)
    o_p = _att1_prompt(cq_p, ckt_p, cv16_p, c_lambda_q1[0], c_lambda_k1[0], c_lambda_q2[0], c_lambda_k2[0], c_subln[0],
                       lambda_init, bsz, t)
    yp = _post_block(o_p, xp, w_out_c16, norm_mix_post[1], norm_ffn_pre[1], w_up16[1], w_down16[1], norm_ffn_post[1])

    cq_s, ck_s, _, cv_s, _ = _in_proj_c(xs, norm_mix_pre[1], w_c16, cos_s, sin_s)
    rep = C_HEADS // C_KV_HEADS
    cq_bd = _block_diag_rows(cq_s, db, dec, [(slot // rep) % 2 for slot in range(2 * C_HEADS)], 2)
    o_s = _att1_sample(page_table, cq_bd, new16(ck_s), new16(cv_s),
                       cache_c_k.reshape(-1, n_phys, page * C_KV_HEADS, LANES),
                       cache_c_v.reshape(-1, n_phys, page * C_KV_HEADS, LANES),
                       c_lambda_q1[0], c_lambda_k1[0], c_lambda_q2[0], c_lambda_k2[0], c_subln[0], lambda_init)
    ys = _post_block(o_s.reshape(ms, -1), xs, w_out_c16, norm_mix_post[1], norm_ffn_pre[1], w_up16[1], w_down16[1],
                     norm_ffn_post[1])

    def rows(a, b_, t_, heads, width):
        shape = (1, b_, t_, heads, width) if heads else (1, b_, t_, width)
        return a.reshape(shape)

    per_head = lambda a: jnp.transpose(a, (0, 3, 1, 2))[None]
    return (yp.reshape(bsz, t, d), ys.reshape(db, dec, d),
            per_head(ak_p), per_head(av_p), jnp.transpose(ik_p, (0, 2, 1))[None], per_head(bk_p), per_head(bv_p),
            rows(ck_p, bsz, t, C_KV_HEADS, 2 * HEAD_DIM), rows(cv_p, bsz, t, C_KV_HEADS, 2 * HEAD_DIM),
            rows(ak_s, db, dec, A_KV_HEADS, HEAD_DIM), rows(av_s, db, dec, A_KV_HEADS, HEAD_DIM),
            rows(ik_s, db, dec, 0, IDX_DIM),
            rows(bk_s, db, dec, B_HEADS, HEAD_DIM), rows(bv_s, db, dec, B_HEADS, HEAD_DIM),
            rows(ck_s, db, dec, C_KV_HEADS, 2 * HEAD_DIM), rows(cv_s, db, dec, C_KV_HEADS, 2 * HEAD_DIM))
```

```python
import functools
import math

import jax
import jax.numpy as jnp
from jax import lax
from jax.experimental import pallas as pl
from jax.experimental.pallas import tpu as pltpu

F32 = jnp.float32
BF16 = jnp.bfloat16
I32 = jnp.int32

HEAD_DIM = 64
A_HEADS = 8
A_KV_HEADS = 2
IDX_HEADS = 8
IDX_DIM = 64
TOPK_MAX = 256
B_HEADS = 8
C_HEADS = 8
C_KV_HEADS = 4
ROPE_THETA = 10000.0
NORM_EPS = 1e-6
SUBLN_EPS = 1e-5

LANES = 128
KC = 2 * LANES
QB = KC
QB_DIFF = 256
PAGE_GROUP = 32
ROW_TILE = 256
MLP_ROW_TILE = 512
MASKED = -1e30
STICK_UNDERFLOW_LOG2 = -160.0
QK_SCALE = HEAD_DIM ** -0.5
EXP_SCALE = QK_SCALE * math.log2(math.e)
VMEM_LIMIT = 56 * 1024 * 1024


def _cparams(*sem):
    return pltpu.CompilerParams(dimension_semantics=sem, vmem_limit_bytes=VMEM_LIMIT)


def _rms(x, g, eps):
    return x * lax.rsqrt(jnp.mean(x * x, axis=-1, keepdims=True) + eps) * g


def _dot(a, b):
    return jnp.dot(a, b, preferred_element_type=F32)


def _dot_nt(a, b):
    return lax.dot_general(a, b, (((1,), (1,)), ((), ())), preferred_element_type=F32)


def _rope_chunk(seg, cos, sin_signed, first_half):
    partner = jnp.where(first_half, pltpu.roll(seg, LANES - HEAD_DIM // 2, 1), pltpu.roll(seg, HEAD_DIM // 2, 1))
    return seg * cos + partner * sin_signed


def _rope_wide(y, cos, sin_signed, first_half):
    return jnp.concatenate([_rope_chunk(y[:, c * LANES:(c + 1) * LANES], cos, sin_signed, first_half)
                            for c in range(y.shape[1] // LANES)], axis=1)


def _first_half_mask(shape):
    lane = lax.broadcasted_iota(I32, shape, 1)
    return (lane & (HEAD_DIM - 1)) < HEAD_DIM // 2


def _split_heads(y, out_ref, scale=None):
    for h in range(y.shape[1] // HEAD_DIM):
        v = y[:, h * HEAD_DIM:(h + 1) * HEAD_DIM]
        out_ref[h] = (v if scale is None else v * scale).astype(out_ref.dtype)


def _split_heads_t(y, out_ref):
    yt = y.T
    for h in range(yt.shape[0] // HEAD_DIM):
        out_ref[h] = yt[h * HEAD_DIM:(h + 1) * HEAD_DIM, :].astype(out_ref.dtype)


_AB_AQ, _AB_IQ, _AB_AK, _AB_AV, _AB_BQ, _AB_BK, _AB_BV, _AB_TAIL, _AB_WIDTH = 0, 512, 1024, 1152, 1280, 1792, 2304, 2816, 2944
_A_KV_W = A_KV_HEADS * HEAD_DIM
_B_W = B_HEADS * HEAD_DIM


def _prep_w_ab(w):
    d = w.shape[0]
    sizes = (A_HEADS * HEAD_DIM, _A_KV_W, _A_KV_W, IDX_HEADS * IDX_DIM, IDX_DIM, IDX_HEADS, _B_W, _B_W, _B_W)
    offs = [0]
    for s in sizes:
        offs.append(offs[-1] + s)
    a_q, a_k, a_v, i_q, i_k, i_w, b_q, b_k, b_v = (w[:, offs[i]:offs[i + 1]] for i in range(9))
    pad = jnp.zeros((d, _AB_WIDTH - _AB_TAIL - IDX_DIM - IDX_HEADS), w.dtype)
    return jnp.concatenate([a_q, i_q, a_k, a_v, b_q, b_k, b_v, i_k, i_w, pad], axis=1).astype(BF16)


def _store_kv(y, rows_o, per_batch):
    if per_batch:
        _split_heads_t(y, rows_o)
    else:
        rows_o[...] = y


def _in_ab_body(per_batch, x_ref, g_ref, w_ref, cos_ref, sin_ref,
                aq_o, iq_o, bq_o, ak_o, akt_o, av_o, av16_o, ik_o, ikt_o, iw_o, bk_o, bkt_o, bv_o, bv16_o):
    xn = _rms(x_ref[...], g_ref[...], NORM_EPS).astype(BF16)
    cos = cos_ref[...]
    sin = sin_ref[...]
    first = _first_half_mask(cos.shape)

    def proj(lo, width):
        return _dot(xn, w_ref[:, lo:lo + width])

    _split_heads(_rope_wide(proj(_AB_AQ, 512), cos, sin, first), aq_o, EXP_SCALE)
    _split_heads(_rope_wide(proj(_AB_IQ, 512), cos, sin, first), iq_o, QK_SCALE)
    _split_heads(proj(_AB_BQ, 512), bq_o, EXP_SCALE)

    ak = _rope_chunk(proj(_AB_AK, _A_KV_W), cos, sin, first)
    _store_kv(ak, ak_o, per_batch)
    _split_heads_t(ak, akt_o)
    av = proj(_AB_AV, _A_KV_W)
    _store_kv(av, av_o, per_batch)
    av16_o[...] = av.astype(BF16)

    bk = proj(_AB_BK, _B_W)
    _store_kv(bk, bk_o, per_batch)
    _split_heads_t(bk, bkt_o)
    bv = proj(_AB_BV, _B_W)
    _store_kv(bv, bv_o, per_batch)
    bv16_o[...] = bv.astype(BF16)

    tail = proj(_AB_TAIL, LANES)
    ik = _rope_chunk(tail, cos, sin, first)
    ik_t = ik.T[:IDX_DIM, :]
    ik_o[...] = ik_t if per_batch else ik[:, :IDX_DIM]
    ikt_o[...] = ik_t.astype(BF16)
    iw_o[...] = tail[:, IDX_DIM:IDX_DIM + IDX_HEADS] * (IDX_HEADS ** -0.5)


def _in_proj_ab(x2d, g, w_perm, cos, sin, batch_tokens=None):
    m, d = x2d.shape
    tm = min(ROW_TILE, m)
    n_pos_blocks = cos.shape[0] // tm
    row = lambda i: (i, 0)
    hrow = lambda i: (0, i, 0)
    hcol = lambda i: (0, 0, i)
    pos = lambda i: (i % n_pos_blocks, 0)
    const = lambda i: (0, 0)
    sds = jax.ShapeDtypeStruct
    hm = lambda heads: sds((heads, m, HEAD_DIM), BF16)
    hmt = lambda heads: sds((heads, HEAD_DIM, m), BF16)
    hspec = lambda heads: pl.BlockSpec((heads, tm, HEAD_DIM), hrow)
    htspec = lambda heads: pl.BlockSpec((heads, HEAD_DIM, tm), hcol)
    rspec = lambda width: pl.BlockSpec((tm, width), row)
    per_batch = batch_tokens is not None
    if per_batch:
        nt = batch_tokens // tm
        bsz = m // batch_tokens
        kv = lambda heads: (sds((bsz, heads, HEAD_DIM, batch_tokens), F32),
                            pl.BlockSpec((None, heads, HEAD_DIM, tm), lambda i: (i // nt, 0, 0, i % nt)))
        ik = (sds((bsz, IDX_DIM, batch_tokens), F32), pl.BlockSpec((None, IDX_DIM, tm), lambda i: (i // nt, 0, i % nt)))
    else:
        kv = lambda heads: (sds((m, heads * HEAD_DIM), F32), rspec(heads * HEAD_DIM))
        ik = (sds((m, IDX_DIM), F32), rspec(IDX_DIM))
    outs = (
        (hm(A_HEADS), hspec(A_HEADS)), (hm(IDX_HEADS), hspec(IDX_HEADS)), (hm(B_HEADS), hspec(B_HEADS)),
        kv(A_KV_HEADS), (hmt(A_KV_HEADS), htspec(A_KV_HEADS)), kv(A_KV_HEADS), (sds((m, _A_KV_W), BF16), rspec(_A_KV_W)),
        ik, (sds((IDX_DIM, m), BF16), pl.BlockSpec((IDX_DIM, tm), lambda i: (0, i))),
        (sds((m, IDX_HEADS), F32), rspec(IDX_HEADS)),
        kv(B_HEADS), (hmt(B_HEADS), htspec(B_HEADS)), kv(B_HEADS), (sds((m, _B_W), BF16), rspec(_B_W)),
    )
    return pl.pallas_call(
        functools.partial(_in_ab_body, per_batch),
        grid=(m // tm,),
        in_specs=[pl.BlockSpec((tm, d), row), pl.BlockSpec((1, d), const), pl.BlockSpec((d, _AB_WIDTH), const),
                  pl.BlockSpec((tm, LANES), pos), pl.BlockSpec((tm, LANES), pos)],
        out_specs=tuple(o[1] for o in outs),
        out_shape=tuple(o[0] for o in outs),
        compiler_params=_cparams("parallel"),
        name="in_proj_ab",
    )(x2d, g.reshape(1, d), w_perm, cos, sin)


_C_Q_W = C_HEADS * 2 * HEAD_DIM
_C_KV_W = C_KV_HEADS * 2 * HEAD_DIM


def _store_kv_pairs(y, rows_o, interleaved):
    if interleaved:
        for g in range(C_KV_HEADS):
            rows_o[pl.ds(g, y.shape[0], stride=C_KV_HEADS), :] = y[:, g * LANES:(g + 1) * LANES]
    else:
        rows_o[...] = y


def _in_c_body(interleaved, x_ref, g_ref, w_ref, cos_ref, sin_ref, cq_o, ck_o, ckt_o, cv_o, cv16_o):
    xn = _rms(x_ref[...], g_ref[...], NORM_EPS).astype(BF16)
    cos = cos_ref[...]
    sin = sin_ref[...]
    first = _first_half_mask(cos.shape)

    def proj(lo, width):
        return _dot(xn, w_ref[:, lo:lo + width])

    rep = C_HEADS // C_KV_HEADS
    for block in range(2):
        q = _rope_wide(proj(block * 512, 512), cos, sin, first)
        for i in range(8):
            head, half = divmod(block * 8 + i, 2)
            slot = ((head // rep) * 2 + half) * rep + head % rep
            cq_o[slot] = (q[:, i * HEAD_DIM:(i + 1) * HEAD_DIM] * EXP_SCALE).astype(BF16)
    k = _rope_wide(proj(_C_Q_W, _C_KV_W), cos, sin, first)
    _store_kv_pairs(k, ck_o, interleaved)
    _split_heads_t(k, ckt_o)
    v = proj(_C_Q_W + _C_KV_W, _C_KV_W)
    _store_kv_pairs(v, cv_o, interleaved)
    cv16_o[...] = v.astype(BF16)


def _in_proj_c(x2d, g, w_bf16, cos, sin, interleaved=False):
    m, d = x2d.shape
    tm = min(ROW_TILE, m)
    n_pos_blocks = cos.shape[0] // tm
    row = lambda i: (i, 0)
    hrow = lambda i: (0, i, 0)
    hcol = lambda i: (0, 0, i)
    pos = lambda i: (i % n_pos_blocks, 0)
    const = lambda i: (0, 0)
    width = _C_Q_W + 2 * _C_KV_W
    sds = jax.ShapeDtypeStruct
    if interleaved:
        kv = (sds((m * C_KV_HEADS, LANES), F32), pl.BlockSpec((tm * C_KV_HEADS, LANES), row))
    else:
        kv = (sds((m, _C_KV_W), F32), pl.BlockSpec((tm, _C_KV_W), row))
    outs = (
        (sds((2 * C_HEADS, m, HEAD_DIM), BF16), pl.BlockSpec((2 * C_HEADS, tm, HEAD_DIM), hrow)),
        kv, (sds((2 * C_KV_HEADS, HEAD_DIM, m), BF16), pl.BlockSpec((2 * C_KV_HEADS, HEAD_DIM, tm), hcol)),
        kv, (sds((m, _C_KV_W), BF16), pl.BlockSpec((tm, _C_KV_W), row)),
    )
    return pl.pallas_call(
        functools.partial(_in_c_body, interleaved),
        grid=(m // tm,),
        in_specs=[pl.BlockSpec((tm, d), row), pl.BlockSpec((1, d), const), pl.BlockSpec((d, width), const),
                  pl.BlockSpec((tm, LANES), pos), pl.BlockSpec((tm, LANES), pos)],
        out_specs=tuple(o[1] for o in outs),
        out_shape=tuple(o[0] for o in outs),
        compiler_params=_cparams("parallel"),
        name="in_proj_c",
    )(x2d, g.reshape(1, d), w_bf16, cos, sin)


def _post_body(o_ref, x_ref, wout_ref, gpost_ref, gpre_ref, wup_ref, wdown_ref, gfpost_ref, y_ref):
    mix = _dot(o_ref[...], wout_ref[...])
    x1 = x_ref[...] + _rms(mix, gpost_ref[...], NORM_EPS)
    hn = _rms(x1, gpre_ref[...], NORM_EPS).astype(BF16)
    d_ff = wup_ref.shape[1]
    step = 1024
    f = jnp.zeros(x1.shape, F32)
    for c in range(d_ff // step):
        u = jnp.maximum(_dot(hn, wup_ref[:, c * step:(c + 1) * step]), 0.0)
        f = f + _dot((u * u).astype(BF16), wdown_ref[c * step:(c + 1) * step, :])
    y_ref[...] = x1 + _rms(f, gfpost_ref[...], NORM_EPS)


def _post_block(o2d, x2d, w_out, g_post, g_pre, w_up, w_down, g_fpost):
    m, d = x2d.shape
    tm = min(MLP_ROW_TILE if m % MLP_ROW_TILE == 0 else ROW_TILE, m)
    assert m % tm == 0
    d_ff = w_up.shape[1]
    row = lambda i: (i, 0)
    const = lambda i: (0, 0)
    resident = lambda shape: pl.BlockSpec(shape, const, pipeline_mode=pl.Buffered(1))
    return pl.pallas_call(
        _post_body,
        grid=(m // tm,),
        in_specs=[pl.BlockSpec((tm, o2d.shape[1]), row), pl.BlockSpec((tm, d), row),
                  resident(w_out.shape), pl.BlockSpec((1, d), const), pl.BlockSpec((1, d), const),
                  resident((d, d_ff)), resident((d_ff, d)), pl.BlockSpec((1, d), const)],
        out_specs=pl.BlockSpec((tm, d), row),
        out_shape=jax.ShapeDtypeStruct((m, d), F32),
        compiler_params=_cparams("parallel"),
        name="post_block",
    )(o2d, x2d, w_out, g_post.reshape(1, d), g_pre.reshape(1, d), w_up, w_down, g_fpost.reshape(1, d))


def _order_key(s):
    s = jnp.where(s == 0.0, 0.0, s)
    b = pltpu.bitcast(s, I32)
    return b ^ ((b >> 31) & I32(0x7FFFFFFF))


def _select_bias(sc_ref, key_ref, nch, n_sel, rows, width):
    def fill_keys(j, carry):
        s = sc_ref[j]
        key_ref[j] = _order_key(s)
        return jnp.maximum(carry[0], s), jnp.minimum(carry[1], s)

    lane_max, lane_min = lax.fori_loop(0, nch, fill_keys, (jnp.full((rows, width), -jnp.inf, F32),
                                                           jnp.full((rows, width), jnp.inf, F32)))

    def count(pred):
        def body(j, c):
            hit = jnp.where(pred(key_ref[j]), 1.0, 0.0)
            return c + sum(hit[:, i * LANES:(i + 1) * LANES] for i in range(width // LANES))

        c = lax.fori_loop(0, nch, body, jnp.zeros((rows, LANES), F32))
        return jnp.sum(c, axis=1, keepdims=True)

    def count_ge(cand):
        cand_b = jnp.broadcast_to(cand, (rows, width))
        return count(lambda k: k >= cand_b)

    k_sel = float(n_sel)
    low_score = jnp.min(lane_max if n_sel <= width else lane_min, axis=1, keepdims=True)
    lo = _order_key(low_score)
    hi = _order_key(jnp.max(lane_max, axis=1, keepdims=True))
    lo_b = jnp.broadcast_to(lo, (rows, width))
    hi = jnp.where(count(lambda k: k > lo_b) < k_sel, lo, hi)

    def open_rows(lo, hi):
        return jnp.max(jnp.where(lo < hi, 1.0, 0.0))

    def bisect(carry):
        lo, hi, _ = carry
        mid = (lo | hi) - ((lo ^ hi) >> 1)
        n_ge = count_ge(mid)
        lo = jnp.where(n_ge >= k_sel, mid, lo)
        hi = jnp.where(n_ge > k_sel, hi, jnp.where(n_ge == k_sel, mid, mid - 1))
        return lo, hi, open_rows(lo, hi)

    thr, _, _ = lax.while_loop(lambda carry: carry[2] > 0.5, bisect, (lo, hi, open_rows(lo, hi)))
    thr_b = jnp.broadcast_to(thr, (rows, width))
    need_b = jnp.broadcast_to(k_sel - count(lambda k: k > thr_b), (rows, width))
    r = lax.broadcasted_iota(I32, (width, width), 0)
    c = lax.broadcasted_iota(I32, (width, width), 1)
    upper = jnp.where(r < c, 1.0, 0.0).astype(BF16)
    ones = jnp.ones((width, width), BF16)

    def tie_body(j, run):
        k = key_ref[j]
        eq = jnp.where(k == thr_b, 1.0, 0.0)
        eq16 = eq.astype(BF16)
        rank = run + _dot(eq16, upper)
        bias = jnp.where(k > thr_b, 0.0, jnp.where(eq * jnp.where(rank < need_b, 1.0, 0.0) > 0.5, 0.0, MASKED))
        sc_ref[j] = jnp.where(sc_ref[j] == -jnp.inf, MASKED, bias)
        return run + _dot(eq16, ones)

    lax.fori_loop(0, nch, tie_body, jnp.zeros((rows, width), F32))


def _weighted_values(p16, v16, v_transposed):
    return _dot_nt(p16, v16) if v_transposed else _dot(p16, v16)


def _softmax_step(s, values, m, l, acc, v_transposed=False):
    m_new = jnp.maximum(m, jnp.max(s, axis=1, keepdims=True))
    alpha = jnp.exp2(m - m_new)
    p = jnp.exp2(s - m_new)
    l = alpha * l + jnp.sum(p, axis=1, keepdims=True)
    p16 = p.astype(BF16)
    if not isinstance(values, (list, tuple)):
        pv = _weighted_values(p16, values, v_transposed)
    else:
        pv = sum(_weighted_values(p16[:, i * LANES:(i + 1) * LANES], v, v_transposed) for i, v in enumerate(values))
    return m_new, l, alpha * acc + pv


def _stick_matrices():
    r = lax.broadcasted_iota(I32, (2 * LANES, 2 * LANES), 0) & (LANES - 1)
    c = lax.broadcasted_iota(I32, (2 * LANES, 2 * LANES), 1)
    return jnp.where((c >= LANES) | (r > c), 1.0, 0.0).astype(BF16)


def _softmax_ones_step(s, v_ext16, m, la):
    tiles = s.shape[1] // LANES
    m_new = jnp.maximum(m, jnp.max(s.astype(BF16), axis=1, keepdims=True).astype(F32))
    alpha = jnp.exp2(m - m_new)
    p16 = jnp.exp2(s - jnp.concatenate([m_new] * tiles, axis=1)).astype(BF16)
    return m_new, jnp.concatenate([alpha] * (la.shape[1] // LANES), axis=1) * la + _dot(p16, v_ext16)


def _stick_weights(zs, valid, run, mats):
    rows = zs[0].shape[0]
    softplus = [jnp.maximum(z, 0.0) + jnp.log2(1.0 + jnp.exp2(-jnp.abs(z))) for z in zs]
    split = []
    for sp in softplus:
        log_stay = -sp if valid is None else jnp.where(valid, -sp, 0.0)
        hi = log_stay.astype(BF16)
        lo = (log_stay - hi.astype(F32)).astype(BF16)
        split.append(jnp.concatenate([hi, lo], axis=1))
    sums = _dot(split[0] if len(split) == 1 else jnp.concatenate(split, axis=0), mats)
    ws = []
    for n, (z, sp) in enumerate(zip(zs, softplus)):
        part = sums[n * rows:(n + 1) * rows]
        w = jnp.exp2((z - sp) + part[:, :LANES] + run)
        if valid is not None:
            w = jnp.where(valid, w, 0.0)
        ws.append(w.astype(BF16))
        run = run + part[:, LANES:]
    return run, ws


def _chunk_count(qi, qb=QB):
    return lax.shift_right_logical((qi + 1) * qb - 1, KC.bit_length() - 1) + 1


def _chunk_lanes(j):
    return pl.ds(pl.multiple_of(j * KC, KC), KC)


def _att0_prompt_body(n_sel, aq_ref, iq_ref, iw_ref, bq_ref, akt_ref, av_ref, ikt_ref, bkt_ref, bv_ref,
                      o_ref, sc_ref, key_ref, m_ref, la_ref, run_ref, acc_ref):
    qi = pl.program_id(1)
    n_kc = _chunk_count(qi)
    row = lax.broadcasted_iota(I32, (QB, KC), 0)
    col = lax.broadcasted_iota(I32, (QB, KC), 1)
    qpos = qi * QB + row

    def score_chunk(j, c):
        kt = ikt_ref[:, _chunk_lanes(j)]
        acc = jnp.zeros((QB, KC), F32)
        for h in range(IDX_HEADS):
            acc = acc + jnp.maximum(_dot(iq_ref[h], kt), 0.0) * iw_ref[:, h:h + 1]
        sc_ref[j] = jnp.where(j * KC + col <= qpos, acc, -jnp.inf)
        return c

    lax.fori_loop(0, n_kc, score_chunk, 0)
    _select_bias(sc_ref, key_ref, n_kc, n_sel, QB, KC)

    m_ref[...] = jnp.full_like(m_ref, MASKED)
    la_ref[...] = jnp.zeros_like(la_ref)
    rep = A_HEADS // A_KV_HEADS
    ones = jnp.ones((KC, LANES), BF16)

    def dsa_chunk(j, c):
        bias = jnp.concatenate([sc_ref[j]] * rep, axis=0)
        v_ext = jnp.concatenate([av_ref[_chunk_lanes(j), :], ones], axis=1)
        for g in range(A_KV_HEADS):
            q = aq_ref[rep * g:rep * (g + 1)].reshape(rep * QB, HEAD_DIM)
            s = _dot(q, akt_ref[g, :, _chunk_lanes(j)]) + bias
            m_ref[g], la_ref[g] = _softmax_ones_step(s, v_ext, m_ref[g], la_ref[g])
        return c

    lax.fori_loop(0, n_kc, dsa_chunk, 0)
    outs = []
    for h in range(A_HEADS):
        g, r = h // rep, h % rep
        la = la_ref[g][r * QB:(r + 1) * QB]
        outs.append(la[:, g * HEAD_DIM:(g + 1) * HEAD_DIM] / la[:, LANES:LANES + HEAD_DIM])
    o_ref[:, :A_HEADS * HEAD_DIM] = jnp.concatenate(outs, axis=1).astype(o_ref.dtype)

    run_ref[...] = jnp.zeros_like(run_ref)
    acc_ref[...] = jnp.zeros_like(acc_ref)
    mats = _stick_matrices()
    rows_all = B_HEADS * QB
    qpos_c = qi * QB + (lax.broadcasted_iota(I32, (rows_all, LANES), 0) & (QB - 1))
    col_c = lax.broadcasted_iota(I32, (rows_all, LANES), 1)

    def sb_keys(starts, on_diagonal):
        keys = [pl.ds(pl.multiple_of(start, LANES), LANES) for start in starts]
        zs = [jnp.concatenate([_dot(bq_ref[h], bkt_ref[h, :, k]) for h in range(B_HEADS)], axis=0) for k in keys]
        valid = (starts[0] + col_c < qpos_c) if on_diagonal else None
        run_ref[...], ws = _stick_weights(zs, valid, run_ref[...], mats)
        for pair in range(B_HEADS // 2):
            rows = slice(2 * pair * QB, 2 * (pair + 1) * QB)
            acc_ref[pair] += sum(_dot(w16[rows], bv_ref[k, pair * LANES:(pair + 1) * LANES]) for w16, k in zip(ws, keys))

    n_full = qi * (QB // KC)
    for sub in reversed(range(QB // LANES)):
        sb_keys([qi * QB + sub * LANES], True)

    def sb_live():
        return jnp.max(run_ref[...]) > STICK_UNDERFLOW_LOG2

    def sb_chunk(carry):
        i = carry[0]
        j = n_full - 1 - i
        sb_keys([j * KC + sub * LANES for sub in reversed(range(KC // LANES))], False)
        return i + 1, sb_live()

    lax.while_loop(lambda carry: (carry[0] < n_full) & carry[1], sb_chunk, (I32(0), sb_live()))
    outs = [acc_ref[h // 2][(h % 2) * QB:(h % 2 + 1) * QB, (h % 2) * HEAD_DIM:(h % 2 + 1) * HEAD_DIM]
            for h in range(B_HEADS)]
    o_ref[:, A_HEADS * HEAD_DIM:] = jnp.concatenate(outs, axis=1).astype(o_ref.dtype)


def _att0_prompt(aq, iq, iw, bq, akt, av16, ikt, bkt, bv16, bsz, t):
    nq = t // QB
    n_sel = min(TOPK_MAX, t // 4)
    qh = lambda b, q: (0, b * nq + q, 0)
    qrow = lambda b, q: (b * nq + q, 0)
    kt = lambda b, q: (0, 0, b)
    krow = lambda b, q: (b, 0)
    width = (A_HEADS + B_HEADS) * HEAD_DIM
    rows_a = (A_HEADS // A_KV_HEADS) * QB
    return pl.pallas_call(
        functools.partial(_att0_prompt_body, n_sel),
        grid=(bsz, nq),
        in_specs=[pl.BlockSpec((A_HEADS, QB, HEAD_DIM), qh), pl.BlockSpec((IDX_HEADS, QB, IDX_DIM), qh),
                  pl.BlockSpec((QB, IDX_HEADS), qrow), pl.BlockSpec((B_HEADS, QB, HEAD_DIM), qh),
                  pl.BlockSpec((A_KV_HEADS, HEAD_DIM, t), kt), pl.BlockSpec((t, _A_KV_W), krow),
                  pl.BlockSpec((IDX_DIM, t), lambda b, q: (0, b)),
                  pl.BlockSpec((B_HEADS, HEAD_DIM, t), kt), pl.BlockSpec((t, _B_W), krow)],
        out_specs=pl.BlockSpec((QB, width), qrow),
        out_shape=jax.ShapeDtypeStruct((bsz * t, width), BF16),
        scratch_shapes=[pltpu.VMEM((t // KC, QB, KC), F32), pltpu.VMEM((t // KC, QB, KC), I32),
                        pltpu.VMEM((A_KV_HEADS, rows_a, LANES), F32), pltpu.VMEM((A_KV_HEADS, rows_a, 2 * LANES), F32),
                        pltpu.VMEM((B_HEADS * QB, LANES), F32), pltpu.VMEM((B_HEADS // 2, 2 * QB, LANES), F32)],
        compiler_params=_cparams("parallel", "arbitrary"),
        name="att0_prompt",
    )(aq, iq, iw, bq, akt, av16, ikt, bkt, bv16)


def _diff_lambda(lq1_ref, lk1_ref, lq2_ref, lk2_ref, lambda_init):
    s1 = jnp.sum(lq1_ref[...] * lk1_ref[...], axis=1, keepdims=True)
    s2 = jnp.sum(lq2_ref[...] * lk2_ref[...], axis=1, keepdims=True)
    return jnp.exp(s1) - jnp.exp(s2) + lambda_init


def _diff_finish(a1, l1, a2, l2, lam, gsub, lambda_init):
    o = a1 / l1 - lam * (a2 / l2)
    return _rms(o, gsub, SUBLN_EPS) * (1.0 - lambda_init)


def _att1_prompt_body(lambda_init, cq_ref, ckt_ref, cv_ref, lq1_ref, lk1_ref, lq2_ref, lk2_ref, gsub_ref,
                      o_ref, m_ref, la_ref):
    qi = pl.program_id(1)
    qb = o_ref.shape[0]
    rep = C_HEADS // C_KV_HEADS
    rows = rep * qb
    n_kc = _chunk_count(qi, qb)
    row = lax.broadcasted_iota(I32, (rows, KC), 0)
    col = lax.broadcasted_iota(I32, (rows, KC), 1)
    qpos = qi * qb + (row & (qb - 1))
    m_ref[...] = jnp.full_like(m_ref, MASKED)
    la_ref[...] = jnp.zeros_like(la_ref)
    ones = jnp.ones((KC, LANES), BF16)

    def chunk(j, on_diagonal):
        bias = jnp.where(j * KC + col <= qpos, 0.0, MASKED) if on_diagonal else None
        for g in range(C_KV_HEADS):
            v_ext = jnp.concatenate([cv_ref[_chunk_lanes(j), g * LANES:(g + 1) * LANES], ones], axis=1)
            for half in range(2):
                u = 2 * g + half
                q = cq_ref[rep * u:rep * (u + 1)].reshape(rows, HEAD_DIM)
                s = _dot(q, ckt_ref[u, :, _chunk_lanes(j)])
                if on_diagonal:
                    s = s + bias
                m_ref[u], la_ref[u] = _softmax_ones_step(s, v_ext, m_ref[u], la_ref[u])

    def full_chunk(j, c):
        chunk(j, False)
        return c

    lax.fori_loop(0, n_kc - 1, full_chunk, 0)
    chunk(n_kc - 1, True)

    lam = _diff_lambda(lq1_ref, lk1_ref, lq2_ref, lk2_ref, lambda_init)
    gsub = gsub_ref[...]
    for h in range(C_HEADS):
        g, r = h // rep, h % rep
        la1 = la_ref[2 * g][r * qb:(r + 1) * qb]
        la2 = la_ref[2 * g + 1][r * qb:(r + 1) * qb]
        o = _diff_finish(la1[:, :LANES], la1[:, LANES:], la2[:, :LANES], la2[:, LANES:], lam, gsub, lambda_init)
        o_ref[:, h * LANES:(h + 1) * LANES] = o.astype(o_ref.dtype)


def _att1_prompt(cq, ckt, cv16, lq1, lk1, lq2, lk2, gsub, lambda_init, bsz, t):
    qb = QB_DIFF
    nq = t // qb
    qh = lambda b, q: (0, b * nq + q, 0)
    qrow = lambda b, q: (b * nq + q, 0)
    const = lambda b, q: (0, 0)
    vec = lambda a: a.reshape(1, -1)
    rows = (C_HEADS // C_KV_HEADS) * qb
    return pl.pallas_call(
        functools.partial(_att1_prompt_body, lambda_init),
        grid=(bsz, nq),
        in_specs=[pl.BlockSpec((2 * C_HEADS, qb, HEAD_DIM), qh),
                  pl.BlockSpec((2 * C_KV_HEADS, HEAD_DIM, t), lambda b, q: (0, 0, b)),
                  pl.BlockSpec((t, _C_KV_W), lambda b, q: (b, 0)),
                  pl.BlockSpec((1, HEAD_DIM), const), pl.BlockSpec((1, HEAD_DIM), const),
                  pl.BlockSpec((1, HEAD_DIM), const), pl.BlockSpec((1, HEAD_DIM), const),
                  pl.BlockSpec((1, LANES), const)],
        out_specs=pl.BlockSpec((qb, _C_Q_W), qrow),
        out_shape=jax.ShapeDtypeStruct((bsz * t, _C_Q_W), BF16),
        scratch_shapes=[pltpu.VMEM((2 * C_KV_HEADS, rows, LANES), F32),
                        pltpu.VMEM((2 * C_KV_HEADS, rows, 2 * LANES), F32)],
        compiler_params=_cparams("parallel", "arbitrary"),
        name="att1_prompt",
    )(cq, ckt, cv16, vec(lq1), vec(lk1), vec(lq2), vec(lk2), vec(gsub))


def _page_specs(block, n_groups, group, newest_first):
    zeros = (0,) * (len(block) - 2)

    def index(i):
        if newest_first:
            return lambda b, s, pt: (0, pt[b, jnp.minimum(n_groups - s, n_groups - 1) * group + i]) + zeros
        return lambda b, s, pt: (0, pt[b, jnp.minimum(s, n_groups - 1) * group + i]) + zeros

    return [pl.BlockSpec(block, index(i)) for i in range(group)]


def _keys_on_lanes(cache):
    nd = cache.ndim
    return jnp.transpose(cache, (0, 1) + tuple(range(3, nd)) + (2,))


def _att0_sample_walk_body(dec, group, pt_ref, iq_ref, iw_ref, bq_ref, ikn_ref, bkn_ref, bvn_ref, *refs):
    del pt_ref
    ikc_refs, bkc_refs, bvc_refs = refs[:group], refs[group:2 * group], refs[2 * group:3 * group]
    sc_o, ob_o, run_ref, acc_ref = refs[3 * group:]
    s = pl.program_id(1)
    rows = B_HEADS * dec
    row = lax.broadcasted_iota(I32, (rows, LANES), 0)
    col = lax.broadcasted_iota(I32, (rows, LANES), 1)
    t_of_row = row & (dec - 1)
    mats = _stick_matrices()

    def scores(kt16):
        lg = jnp.maximum(_dot(iq_ref[...], kt16), 0.0) * iw_ref[...]
        out = lg[0:dec]
        for h in range(1, IDX_HEADS):
            out = out + lg[h * dec:(h + 1) * dec]
        return out

    def stick(kts, vts, valid):
        run_ref[...], ws = _stick_weights([_dot(bq_ref[...], kt) for kt in kts], valid, run_ref[...], mats)
        acc_ref[...] += sum(_dot_nt(w16, vt) for w16, vt in zip(ws, vts))

    @pl.when(s == 0)
    def _():
        run_ref[...] = jnp.zeros_like(run_ref)
        acc_ref[...] = jnp.zeros_like(acc_ref)
        causal = (col[:dec] < dec) & (col[:dec] <= row[:dec])
        sc_o[0] = jnp.where(causal, scores(ikn_ref[...]), -jnp.inf)
        for i in range(1, group):
            sc_o[i] = jnp.full((dec, LANES), -jnp.inf, F32)
        stick([bkn_ref[...]], [bvn_ref[...]], (col < dec) & (col < t_of_row))

    @pl.when(s > 0)
    def _():
        for i in range(group):
            sc_o[i] = scores(ikc_refs[i][...].astype(BF16))

        @pl.when(jnp.max(run_ref[...]) > STICK_UNDERFLOW_LOG2)
        def _():
            stick([bkc_refs[i][...].reshape(_B_W, LANES).astype(BF16) for i in reversed(range(group))],
                  [bvc_refs[i][...].reshape(_B_W, LANES).astype(BF16) for i in reversed(range(group))], None)

    @pl.when(s == pl.num_programs(1) - 1)
    def _():
        acc = acc_ref[...]
        ob_o[...] = jnp.concatenate([acc[h * dec:(h + 1) * dec, h * HEAD_DIM:(h + 1) * HEAD_DIM]
                                     for h in range(B_HEADS)], axis=1).astype(ob_o.dtype)


def _att0_sample_walk(page_table, iq_st, iw_st, bq_bd, ikt_new, bkt_new, bvt_new, cache_kidx, cache_bk, cache_bv):
    db, n_pages = page_table.shape
    group = math.gcd(PAGE_GROUP, n_pages)
    n_groups = n_pages // group
    dec = iq_st.shape[1] // IDX_HEADS
    rows = B_HEADS * dec
    per_b = lambda b, s, pt: (b, 0, 0)
    chunks_of_step = lambda b, s, pt: (jnp.where(s == 0, n_groups, n_groups - s), b, 0, 0)
    b_page = (None, None, B_HEADS, HEAD_DIM, LANES)
    grid_spec = pltpu.PrefetchScalarGridSpec(
        num_scalar_prefetch=1,
        grid=(db, n_groups + 1),
        in_specs=[pl.BlockSpec((None, rows, IDX_DIM), per_b), pl.BlockSpec((None, rows, 1), per_b),
                  pl.BlockSpec((None, rows, _B_W), per_b),
                  pl.BlockSpec((None, IDX_DIM, LANES), per_b), pl.BlockSpec((None, _B_W, LANES), per_b),
                  pl.BlockSpec((None, _B_W, LANES), per_b)]
        + _page_specs((None, None, IDX_DIM, LANES), n_groups, group, True)
        + _page_specs(b_page, n_groups, group, True) + _page_specs(b_page, n_groups, group, True),
        out_specs=[pl.BlockSpec((group, None, dec, LANES), chunks_of_step), pl.BlockSpec((None, dec, _B_W), per_b)],
        scratch_shapes=[pltpu.VMEM((rows, LANES), F32), pltpu.VMEM((rows, _B_W), F32)],
    )
    return pl.pallas_call(
        functools.partial(_att0_sample_walk_body, dec, group),
        grid_spec=grid_spec,
        out_shape=(jax.ShapeDtypeStruct(((n_groups + 1) * group, db, dec, LANES), F32),
                   jax.ShapeDtypeStruct((db, dec, _B_W), BF16)),
        compiler_params=_cparams("parallel", "arbitrary"),
        name="att0_sample_walk",
    )(page_table, iq_st, iw_st, bq_bd, ikt_new, bkt_new, bvt_new,
      *([cache_kidx] * group + [cache_bk] * group + [cache_bv] * group))


def _select_sample_body(n_sel, nch, sc_ref, bias_o, key_ref):
    rows = sc_ref.shape[1]
    bias_o[...] = sc_ref[...]
    _select_bias(bias_o, key_ref, nch, n_sel, rows, LANES)


def _select_sample(scores, nch, n_sel):
    n_alloc, m, _ = scores.shape
    rows = min(m, 64)
    spec = pl.BlockSpec((n_alloc, rows, LANES), lambda i: (0, i, 0))
    return pl.pallas_call(
        functools.partial(_select_sample_body, n_sel, nch),
        grid=(m // rows,),
        in_specs=[spec],
        out_specs=spec,
        out_shape=jax.ShapeDtypeStruct(scores.shape, F32),
        scratch_shapes=[pltpu.VMEM((n_alloc, rows, LANES), I32)],
        compiler_params=_cparams("parallel"),
        name="select_sample",
    )(scores)


def _att0_sample_dsa_body(n_groups, dec, group, pt_ref, aq_ref, bias_ref, akn_ref, avn_ref, *refs):
    del pt_ref
    akc_refs, avc_refs = refs[:group], refs[group:2 * group]
    oa_o, m_ref, l_ref, acc_ref = refs[2 * group:]
    s = pl.program_id(1)

    @pl.when(s == 0)
    def _():
        m_ref[...] = jnp.full_like(m_ref, MASKED)
        l_ref[...] = jnp.zeros_like(l_ref)
        acc_ref[...] = jnp.zeros_like(acc_ref)

    def step(kts, vts):
        n = len(kts)
        bias = jnp.concatenate([jnp.concatenate([bias_ref[i]] * A_HEADS, axis=0) for i in range(n)], axis=1)
        sc = jnp.concatenate([_dot(aq_ref[...], kt) for kt in kts], axis=1) + bias
        m_ref[...], l_ref[...], acc_ref[...] = _softmax_step(sc, vts, m_ref[...], l_ref[...], acc_ref[...],
                                                             v_transposed=True)

    @pl.when(s < n_groups)
    def _():
        step([r[...].reshape(_A_KV_W, LANES).astype(BF16) for r in akc_refs],
             [r[...].reshape(_A_KV_W, LANES).astype(BF16) for r in avc_refs])

    @pl.when(s == n_groups)
    def _():
        step([akn_ref[...]], [avn_ref[...]])
        o = acc_ref[...] / l_ref[...]
        rep = A_HEADS // A_KV_HEADS
        oa_o[...] = jnp.concatenate(
            [o[h * dec:(h + 1) * dec, (h // rep) * HEAD_DIM:(h // rep + 1) * HEAD_DIM] for h in range(A_HEADS)],
            axis=1).astype(oa_o.dtype)


def _att0_sample_dsa(page_table, aq_bd, bias, akt_new, avt_new, cache_akt, cache_avt):
    db, n_pages = page_table.shape
    group = math.gcd(PAGE_GROUP, n_pages)
    n_groups = n_pages // group
    rows = aq_bd.shape[1]
    dec = rows // A_HEADS
    per_b = lambda b, s, pt: (b, 0, 0)
    a_page = (None, None, A_KV_HEADS, HEAD_DIM, LANES)
    grid_spec = pltpu.PrefetchScalarGridSpec(
        num_scalar_prefetch=1,
        grid=(db, n_groups + 1),
        in_specs=[pl.BlockSpec((None, rows, _A_KV_W), per_b),
                  pl.BlockSpec((group, None, dec, LANES), lambda b, s, pt: (s, b, 0, 0)),
                  pl.BlockSpec((None, _A_KV_W, LANES), per_b), pl.BlockSpec((None, _A_KV_W, LANES), per_b)]
        + _page_specs(a_page, n_groups, group, False) + _page_specs(a_page, n_groups, group, False),
        out_specs=pl.BlockSpec((None, dec, A_HEADS * HEAD_DIM), per_b),
        scratch_shapes=[pltpu.VMEM((rows, 1), F32), pltpu.VMEM((rows, 1), F32), pltpu.VMEM((rows, _A_KV_W), F32)],
    )
    return pl.pallas_call(
        functools.partial(_att0_sample_dsa_body, n_groups, dec, group),
        grid_spec=grid_spec,
        out_shape=jax.ShapeDtypeStruct((db, dec, A_HEADS * HEAD_DIM), BF16),
        compiler_params=_cparams("parallel", "arbitrary"),
        name="att0_sample_dsa",
    )(page_table, aq_bd, bias, akt_new, avt_new, *([cache_akt] * group + [cache_avt] * group))


def _att1_sample_body(n_groups, dec, group, lambda_init, pt_ref, cq_ref, ckn_ref, cvn_ref, *refs):
    del pt_ref
    ckc_refs, cvc_refs = refs[:group], refs[group:2 * group]
    lq1_ref, lk1_ref, lq2_ref, lk2_ref, gsub_ref, o_o, m_ref, l_ref, acc_ref = refs[2 * group:]
    s = pl.program_id(1)
    rows_g = 2 * (C_HEADS // C_KV_HEADS) * dec

    @pl.when(s == 0)
    def _():
        m_ref[...] = jnp.full_like(m_ref, MASKED)
        l_ref[...] = jnp.zeros_like(l_ref)
        acc_ref[...] = jnp.zeros_like(acc_ref)

    def step(keys_of, values_of, n_pages, bias):
        sc = jnp.concatenate(
            [jnp.concatenate([_dot_nt(cq_ref[g * rows_g:(g + 1) * rows_g, :], keys_of(g, i)) for i in range(n_pages)],
                             axis=1) for g in range(C_KV_HEADS)], axis=0)
        if bias is not None:
            sc = sc + bias
        m = m_ref[...]
        m_new = jnp.maximum(m, jnp.max(sc, axis=1, keepdims=True))
        alpha = jnp.exp2(m - m_new)
        p = jnp.exp2(sc - m_new)
        p16 = p.astype(BF16)
        pv = jnp.concatenate(
            [sum(_dot(p16[g * rows_g:(g + 1) * rows_g, i * LANES:(i + 1) * LANES], values_of(g, i))
                 for i in range(n_pages)) for g in range(C_KV_HEADS)], axis=0)
        m_ref[...] = m_new
        l_ref[...] = alpha * l_ref[...] + jnp.sum(p, axis=1, keepdims=True)
        acc_ref[...] = alpha * acc_ref[...] + pv

    @pl.when(s < n_groups)
    def _():
        head_rows = [pl.ds(g, LANES, stride=C_KV_HEADS) for g in range(C_KV_HEADS)]
        step(lambda g, i: ckc_refs[i][head_rows[g], :].astype(BF16),
             lambda g, i: cvc_refs[i][head_rows[g], :].astype(BF16), group, None)

    @pl.when(s == n_groups)
    def _():
        rows = C_KV_HEADS * rows_g
        row = lax.broadcasted_iota(I32, (rows, LANES), 0)
        col = lax.broadcasted_iota(I32, (rows, LANES), 1)
        bias = jnp.where(col <= (row & (dec - 1)), 0.0, MASKED)
        step(lambda g, i: ckn_ref[:, g * LANES:(g + 1) * LANES],
             lambda g, i: cvn_ref[:, g * LANES:(g + 1) * LANES], 1, bias)
        lam = _diff_lambda(lq1_ref, lk1_ref, lq2_ref, lk2_ref, lambda_init)
        acc = acc_ref[...]
        l = l_ref[...]
        outs = []
        rep = C_HEADS // C_KV_HEADS
        for h in range(C_HEADS):
            g, r = h // rep, h % rep
            r1 = ((2 * g) * rep + r) * dec
            r2 = ((2 * g + 1) * rep + r) * dec
            outs.append(_diff_finish(acc[r1:r1 + dec], l[r1:r1 + dec], acc[r2:r2 + dec], l[r2:r2 + dec],
                                     lam, gsub_ref[...], lambda_init))
        o_o[...] = jnp.concatenate(outs, axis=1).astype(o_o.dtype)


def _att1_sample(page_table, cq_bd, ck_new, cv_new, cache_ck, cache_cv, lq1, lk1, lq2, lk2, gsub, lambda_init):
    db, n_pages = page_table.shape
    group = math.gcd(PAGE_GROUP, n_pages)
    n_groups = n_pages // group
    rows = cq_bd.shape[1]
    dec = rows // (2 * C_HEADS)
    per_b = lambda b, s, pt: (b, 0, 0)
    const = lambda b, s, pt: (0, 0)
    vec = lambda a: a.reshape(1, -1)
    c_page = (None, None, LANES * C_KV_HEADS, LANES)
    grid_spec = pltpu.PrefetchScalarGridSpec(
        num_scalar_prefetch=1,
        grid=(db, n_groups + 1),
        in_specs=[pl.BlockSpec((None, rows, LANES), per_b),
                  pl.BlockSpec((None, LANES, _C_KV_W), per_b), pl.BlockSpec((None, LANES, _C_KV_W), per_b)]
        + _page_specs(c_page, n_groups, group, False) + _page_specs(c_page, n_groups, group, False)
        + [pl.BlockSpec((1, HEAD_DIM), const), pl.BlockSpec((1, HEAD_DIM), const),
           pl.BlockSpec((1, HEAD_DIM), const), pl.BlockSpec((1, HEAD_DIM), const), pl.BlockSpec((1, LANES), const)],
        out_specs=pl.BlockSpec((None, dec, _C_Q_W), per_b),
        scratch_shapes=[pltpu.VMEM((rows, 1), F32), pltpu.VMEM((rows, 1), F32), pltpu.VMEM((rows, LANES), F32)],
    )
    return pl.pallas_call(
        functools.partial(_att1_sample_body, n_groups, dec, group, lambda_init),
        grid_spec=grid_spec,
        out_shape=jax.ShapeDtypeStruct((db, dec, _C_Q_W), BF16),
        compiler_params=_cparams("parallel", "arbitrary"),
        name="att1_sample",
    )(page_table, cq_bd, ck_new, cv_new, *([cache_ck] * group + [cache_cv] * group),
      vec(lq1), vec(lk1), vec(lq2), vec(lk2), vec(gsub))


def _rope_tables(pos):
    half = HEAD_DIM // 2
    inv_freq = ROPE_THETA ** (-jnp.arange(half, dtype=F32) / half)
    ang = pos.astype(F32)[:, None] * inv_freq[None, :]
    cos, sin = jnp.cos(ang), jnp.sin(ang)
    return jnp.concatenate([cos] * 4, axis=1), jnp.concatenate([-sin, sin, -sin, sin], axis=1)


def _block_diag_rows(q_hm, db, dec, slot_of_head, n_slots):
    heads, _, w = q_hm.shape
    q = q_hm.reshape(heads, db, dec, w).transpose(1, 0, 2, 3)
    onehot = (jnp.asarray(slot_of_head)[:, None] == jnp.arange(n_slots)[None, :]).astype(q.dtype)
    return (q[:, :, :, None, :] * onehot[None, :, None, :, None]).reshape(db, heads * dec, n_slots * w)


def kernel(x_prompt, x_sample, cache_a_k, cache_a_v, cache_a_kidx, cache_b_k, cache_b_v, cache_c_k, cache_c_v,
           page_table, norm_mix_pre, norm_mix_post, norm_ffn_pre, norm_ffn_post, w_in_ab, w_out_ab, w_in_c, w_out_c,
           c_lambda_q1, c_lambda_k1, c_lambda_q2, c_lambda_k2, c_subln, w_up, w_down):
    bsz, t, d = x_prompt.shape
    db, dec, _ = x_sample.shape
    n_phys, page = cache_a_k.shape[1], cache_a_k.shape[2]
    n_pages = page_table.shape[1]
    past = n_pages * page
    assert page == LANES and t % KC == 0 and dec & (dec - 1) == 0 and dec <= 8
    mp, ms = bsz * t, db * dec

    cos_p, sin_p = _rope_tables(jnp.arange(t, dtype=jnp.int32))
    cos_s, sin_s = _rope_tables(jnp.arange(past, past + dec, dtype=jnp.int32))
    cos_s, sin_s = jnp.tile(cos_s, (db, 1)), jnp.tile(sin_s, (db, 1))

    xp = x_prompt.reshape(mp, d)
    xs = x_sample.reshape(ms, d)
    w_ab = _prep_w_ab(w_in_ab[0])
    w_out_ab16 = w_out_ab[0].astype(BF16)
    w_c16 = w_in_c[0].astype(BF16)
    w_out_c16 = w_out_c[0].astype(BF16)
    w_up16 = w_up.astype(BF16)
    w_down16 = w_down.astype(BF16)
    lambda_init = 0.8 - 0.6 * math.exp(-0.3 * 1)

    (aq_p, iq_p, bq_p, ak_p, akt_p, av_p, av16_p, ik_p, ikt_p, iw_p, bk_p, bkt_p, bv_p, bv16_p) = _in_proj_ab(
        xp, norm_mix_pre[0], w_ab, cos_p, sin_p, batch_tokens=t)
    o_p = _att0_prompt(aq_p, iq_p, iw_p, bq_p, akt_p, av16_p, ikt_p, bkt_p, bv16_p, bsz, t)
    xp = _post_block(o_p, xp, w_out_ab16, norm_mix_post[0], norm_ffn_pre[0], w_up16[0], w_down16[0], norm_ffn_post[0])

    (aq_s, iq_s, bq_s, ak_s, _, av_s, _, ik_s, _, iw_s, bk_s, _, bv_s, _) = _in_proj_ab(
        xs, norm_mix_pre[0], w_ab, cos_s, sin_s)
    iq_st = iq_s.reshape(IDX_HEADS, db, dec, IDX_DIM).transpose(1, 0, 2, 3).reshape(db, IDX_HEADS * dec, IDX_DIM)
    iw_st = iw_s.reshape(db, dec, IDX_HEADS).transpose(0, 2, 1).reshape(db, IDX_HEADS * dec, 1)
    bq_bd = _block_diag_rows(bq_s, db, dec, list(range(B_HEADS)), B_HEADS)
    aq_bd = _block_diag_rows(aq_s, db, dec, [h // (A_HEADS // A_KV_HEADS) for h in range(A_HEADS)], A_KV_HEADS)
    new16 = lambda a: jnp.pad(a.reshape(db, dec, -1).astype(BF16), ((0, 0), (0, LANES - dec), (0, 0)))
    new16t = lambda a: jnp.pad(a.reshape(db, dec, -1).astype(BF16).transpose(0, 2, 1), ((0, 0), (0, 0), (0, LANES - dec)))
    scores, ob_s = _att0_sample_walk(
        page_table, iq_st, iw_st, bq_bd, new16t(ik_s), new16t(bk_s), new16t(bv_s),
        _keys_on_lanes(cache_a_kidx), _keys_on_lanes(cache_b_k), _keys_on_lanes(cache_b_v))
    n_sel_s = min(TOPK_MAX, (past + dec) // 4)
    n_alloc = scores.shape[0]
    bias = _select_sample(scores.reshape(n_alloc, ms, LANES), n_pages + 1, n_sel_s).reshape(n_alloc, db, dec, LANES)
    oa_s = _att0_sample_dsa(page_table, aq_bd, bias, new16t(ak_s), new16t(av_s),
                            _keys_on_lanes(cache_a_k), _keys_on_lanes(cache_a_v))
    o_s = jnp.concatenate([oa_s, ob_s], axis=-1).reshape(ms, -1)
    xs = _post_block(o_s, xs, w_out_ab16, norm_mix_post[0], norm_ffn_pre[0], w_up16[0], w_down16[0], norm_ffn_post[0])

    cq_p, ck_p, ckt_p, cv_p, cv16_p = _in_proj_c(xp, norm_mix_pre[1], w_c16, cos_p, sin_p, interleaved=True)
    o_p = _att1_prompt(cq_p, ckt_p, cv16_p, c_lambda_q1[0], c_lambda_k1[0], c_lambda_q2[0], c_lambda_k2[0], c_subln[0],
                       lambda_init, bsz, t)
    yp = _post_block(o_p, xp, w_out_c16, norm_mix_post[1], norm_ffn_pre[1], w_up16[1], w_down16[1], norm_ffn_post[1])

    cq_s, ck_s, _, cv_s, _ = _in_proj_c(xs, norm_mix_pre[1], w_c16, cos_s, sin_s)
    rep = C_HEADS // C_KV_HEADS
    cq_bd = _block_diag_rows(cq_s, db, dec, [(slot // rep) % 2 for slot in range(2 * C_HEADS)], 2)
    o_s = _att1_sample(page_table, cq_bd, new16(ck_s), new16(cv_s),
                       cache_c_k.reshape(-1, n_phys, page * C_KV_HEADS, LANES),
                       cache_c_v.reshape(-1, n_phys, page * C_KV_HEADS, LANES),
                       c_lambda_q1[0], c_lambda_k1[0], c_lambda_q2[0], c_lambda_k2[0], c_subln[0], lambda_init)
    ys = _post_block(o_s.reshape(ms, -1), xs, w_out_c16, norm_mix_post[1], norm_ffn_pre[1], w_up16[1], w_down16[1],
                     norm_ffn_post[1])

    def rows(a, b_, t_, heads, width):
        shape = (1, b_, t_, heads, width) if heads else (1, b_, t_, width)
        return a.reshape(shape)

    per_head = lambda a: jnp.transpose(a, (0, 3, 1, 2))[None]
    return (yp.reshape(bsz, t, d), ys.reshape(db, dec, d),
            per_head(ak_p), per_head(av_p), jnp.transpose(ik_p, (0, 2, 1))[None], per_head(bk_p), per_head(bv_p),
            rows(ck_p, bsz, t, C_KV_HEADS, 2 * HEAD_DIM), rows(cv_p, bsz, t, C_KV_HEADS, 2 * HEAD_DIM),
            rows(ak_s, db, dec, A_KV_HEADS, HEAD_DIM), rows(av_s, db, dec, A_KV_HEADS, HEAD_DIM),
            rows(ik_s, db, dec, 0, IDX_DIM),
            rows(bk_s, db, dec, B_HEADS, HEAD_DIM), rows(bv_s, db, dec, B_HEADS, HEAD_DIM),
            rows(ck_s, db, dec, C_KV_HEADS, 2 * HEAD_DIM), rows(cv_s, db, dec, C_KV_HEADS, 2 * HEAD_DIM))
```

```python
import functools
import math

import jax
import jax.numpy as jnp
from jax import lax
from jax.experimental import pallas as pl
from jax.experimental.pallas import tpu as pltpu

F32 = jnp.float32
BF16 = jnp.bfloat16
I32 = jnp.int32

HEAD_DIM = 64
A_HEADS = 8
A_KV_HEADS = 2
IDX_HEADS = 8
IDX_DIM = 64
TOPK_MAX = 256
B_HEADS = 8
C_HEADS = 8
C_KV_HEADS = 4
ROPE_THETA = 10000.0
NORM_EPS = 1e-6
SUBLN_EPS = 1e-5

LANES = 128
KC = 2 * LANES
QB = KC
QB_DIFF = 256
PAGE_GROUP = 16
ROW_TILE = 512
MLP_ROW_TILE = 512
MASKED = -1e30
STICK_UNDERFLOW_LOG2 = -160.0
QK_SCALE = HEAD_DIM ** -0.5
EXP_SCALE = QK_SCALE * math.log2(math.e)
VMEM_LIMIT = 56 * 1024 * 1024


def _cparams(*sem):
    return pltpu.CompilerParams(dimension_semantics=sem, vmem_limit_bytes=VMEM_LIMIT)


def _rms(x, g, eps):
    return x * lax.rsqrt(jnp.mean(x * x, axis=-1, keepdims=True) + eps) * g


def _dot(a, b):
    return jnp.dot(a, b, preferred_element_type=F32)


def _dot_nt(a, b):
    return lax.dot_general(a, b, (((1,), (1,)), ((), ())), preferred_element_type=F32)


def _rope_chunk(seg, cos, sin_signed, first_half):
    partner = jnp.where(first_half, pltpu.roll(seg, LANES - HEAD_DIM // 2, 1), pltpu.roll(seg, HEAD_DIM // 2, 1))
    return seg * cos + partner * sin_signed


def _rope_wide(y, cos, sin_signed, first_half):
    return jnp.concatenate([_rope_chunk(y[:, c * LANES:(c + 1) * LANES], cos, sin_signed, first_half)
                            for c in range(y.shape[1] // LANES)], axis=1)


def _first_half_mask(shape):
    lane = lax.broadcasted_iota(I32, shape, 1)
    return (lane & (HEAD_DIM - 1)) < HEAD_DIM // 2


def _split_heads(y, out_ref, scale=None):
    for h in range(y.shape[1] // HEAD_DIM):
        v = y[:, h * HEAD_DIM:(h + 1) * HEAD_DIM]
        out_ref[h] = (v if scale is None else v * scale).astype(out_ref.dtype)


def _split_heads_t(y, out_ref):
    yt = y.T
    for h in range(yt.shape[0] // HEAD_DIM):
        out_ref[h] = yt[h * HEAD_DIM:(h + 1) * HEAD_DIM, :].astype(out_ref.dtype)


_AB_AQ, _AB_IQ, _AB_AK, _AB_AV, _AB_BQ, _AB_BK, _AB_BV, _AB_TAIL, _AB_WIDTH = 0, 512, 1024, 1152, 1280, 1792, 2304, 2816, 2944
_A_KV_W = A_KV_HEADS * HEAD_DIM
_B_W = B_HEADS * HEAD_DIM


def _prep_w_ab(w):
    d = w.shape[0]
    sizes = (A_HEADS * HEAD_DIM, _A_KV_W, _A_KV_W, IDX_HEADS * IDX_DIM, IDX_DIM, IDX_HEADS, _B_W, _B_W, _B_W)
    offs = [0]
    for s in sizes:
        offs.append(offs[-1] + s)
    a_q, a_k, a_v, i_q, i_k, i_w, b_q, b_k, b_v = (w[:, offs[i]:offs[i + 1]] for i in range(9))
    pad = jnp.zeros((d, _AB_WIDTH - _AB_TAIL - IDX_DIM - IDX_HEADS), w.dtype)
    return jnp.concatenate([a_q, i_q, a_k, a_v, b_q, b_k, b_v, i_k, i_w, pad], axis=1).astype(BF16)


def _store_kv(y, rows_o, per_batch):
    if per_batch:
        _split_heads_t(y, rows_o)
    else:
        rows_o[...] = y


def _in_ab_body(per_batch, x_ref, g_ref, w_ref, cos_ref, sin_ref,
                aq_o, iq_o, bq_o, ak_o, akt_o, av_o, av16_o, ik_o, ikt_o, iw_o, bk_o, bkt_o, bv_o, bv16_o):
    xn = _rms(x_ref[...], g_ref[...], NORM_EPS).astype(BF16)
    cos = cos_ref[...]
    sin = sin_ref[...]
    first = _first_half_mask(cos.shape)

    def proj(lo, width):
        return _dot(xn, w_ref[:, lo:lo + width])

    _split_heads(_rope_wide(proj(_AB_AQ, 512), cos, sin, first), aq_o, EXP_SCALE)
    _split_heads(_rope_wide(proj(_AB_IQ, 512), cos, sin, first), iq_o, QK_SCALE)
    _split_heads(proj(_AB_BQ, 512), bq_o, EXP_SCALE)

    ak = _rope_chunk(proj(_AB_AK, _A_KV_W), cos, sin, first)
    _store_kv(ak, ak_o, per_batch)
    _split_heads_t(ak, akt_o)
    av = proj(_AB_AV, _A_KV_W)
    _store_kv(av, av_o, per_batch)
    av16_o[...] = av.astype(BF16)

    bk = proj(_AB_BK, _B_W)
    _store_kv(bk, bk_o, per_batch)
    _split_heads_t(bk, bkt_o)
    bv = proj(_AB_BV, _B_W)
    _store_kv(bv, bv_o, per_batch)
    bv16_o[...] = bv.astype(BF16)

    tail = proj(_AB_TAIL, LANES)
    ik = _rope_chunk(tail, cos, sin, first)
    ik_t = ik.T[:IDX_DIM, :]
    ik_o[...] = ik_t if per_batch else ik[:, :IDX_DIM]
    ikt_o[...] = ik_t.astype(BF16)
    iw_o[...] = tail[:, IDX_DIM:IDX_DIM + IDX_HEADS] * (IDX_HEADS ** -0.5)


def _in_proj_ab(x2d, g, w_perm, cos, sin, batch_tokens=None):
    m, d = x2d.shape
    tm = min(ROW_TILE, m)
    n_pos_blocks = cos.shape[0] // tm
    row = lambda i: (i, 0)
    hrow = lambda i: (0, i, 0)
    hcol = lambda i: (0, 0, i)
    pos = lambda i: (i % n_pos_blocks, 0)
    const = lambda i: (0, 0)
    sds = jax.ShapeDtypeStruct
    hm = lambda heads: sds((heads, m, HEAD_DIM), BF16)
    hmt = lambda heads: sds((heads, HEAD_DIM, m), BF16)
    hspec = lambda heads: pl.BlockSpec((heads, tm, HEAD_DIM), hrow)
    htspec = lambda heads: pl.BlockSpec((heads, HEAD_DIM, tm), hcol)
    rspec = lambda width: pl.BlockSpec((tm, width), row)
    per_batch = batch_tokens is not None
    if per_batch:
        nt = batch_tokens // tm
        bsz = m // batch_tokens
        kv = lambda heads: (sds((bsz, heads, HEAD_DIM, batch_tokens), F32),
                            pl.BlockSpec((None, heads, HEAD_DIM, tm), lambda i: (i // nt, 0, 0, i % nt)))
        ik = (sds((bsz, IDX_DIM, batch_tokens), F32), pl.BlockSpec((None, IDX_DIM, tm), lambda i: (i // nt, 0, i % nt)))
    else:
        kv = lambda heads: (sds((m, heads * HEAD_DIM), F32), rspec(heads * HEAD_DIM))
        ik = (sds((m, IDX_DIM), F32), rspec(IDX_DIM))
    outs = (
        (hm(A_HEADS), hspec(A_HEADS)), (hm(IDX_HEADS), hspec(IDX_HEADS)), (hm(B_HEADS), hspec(B_HEADS)),
        kv(A_KV_HEADS), (hmt(A_KV_HEADS), htspec(A_KV_HEADS)), kv(A_KV_HEADS), (sds((m, _A_KV_W), BF16), rspec(_A_KV_W)),
        ik, (sds((IDX_DIM, m), BF16), pl.BlockSpec((IDX_DIM, tm), lambda i: (0, i))),
        (sds((m, IDX_HEADS), F32), rspec(IDX_HEADS)),
        kv(B_HEADS), (hmt(B_HEADS), htspec(B_HEADS)), kv(B_HEADS), (sds((m, _B_W), BF16), rspec(_B_W)),
    )
    return pl.pallas_call(
        functools.partial(_in_ab_body, per_batch),
        grid=(m // tm,),
        in_specs=[pl.BlockSpec((tm, d), row), pl.BlockSpec((1, d), const), pl.BlockSpec((d, _AB_WIDTH), const),
                  pl.BlockSpec((tm, LANES), pos), pl.BlockSpec((tm, LANES), pos)],
        out_specs=tuple(o[1] for o in outs),
        out_shape=tuple(o[0] for o in outs),
        compiler_params=_cparams("parallel"),
        name="in_proj_ab",
    )(x2d, g.reshape(1, d), w_perm, cos, sin)


_C_Q_W = C_HEADS * 2 * HEAD_DIM
_C_KV_W = C_KV_HEADS * 2 * HEAD_DIM


def _store_kv_pairs(y, rows_o, interleaved):
    if interleaved:
        for g in range(C_KV_HEADS):
            rows_o[pl.ds(g, y.shape[0], stride=C_KV_HEADS), :] = y[:, g * LANES:(g + 1) * LANES]
    else:
        rows_o[...] = y


def _in_c_body(interleaved, x_ref, g_ref, w_ref, cos_ref, sin_ref, cq_o, ck_o, ckt_o, cv_o, cv16_o):
    xn = _rms(x_ref[...], g_ref[...], NORM_EPS).astype(BF16)
    cos = cos_ref[...]
    sin = sin_ref[...]
    first = _first_half_mask(cos.shape)

    def proj(lo, width):
        return _dot(xn, w_ref[:, lo:lo + width])

    rep = C_HEADS // C_KV_HEADS
    for block in range(2):
        q = _rope_wide(proj(block * 512, 512), cos, sin, first)
        for i in range(8):
            head, half = divmod(block * 8 + i, 2)
            slot = ((head // rep) * 2 + half) * rep + head % rep
            cq_o[slot] = (q[:, i * HEAD_DIM:(i + 1) * HEAD_DIM] * EXP_SCALE).astype(BF16)
    k = _rope_wide(proj(_C_Q_W, _C_KV_W), cos, sin, first)
    _store_kv_pairs(k, ck_o, interleaved)
    _split_heads_t(k, ckt_o)
    v = proj(_C_Q_W + _C_KV_W, _C_KV_W)
    _store_kv_pairs(v, cv_o, interleaved)
    cv16_o[...] = v.astype(BF16)


def _in_proj_c(x2d, g, w_bf16, cos, sin, interleaved=False):
    m, d = x2d.shape
    tm = min(ROW_TILE, m)
    n_pos_blocks = cos.shape[0] // tm
    row = lambda i: (i, 0)
    hrow = lambda i: (0, i, 0)
    hcol = lambda i: (0, 0, i)
    pos = lambda i: (i % n_pos_blocks, 0)
    const = lambda i: (0, 0)
    width = _C_Q_W + 2 * _C_KV_W
    sds = jax.ShapeDtypeStruct
    if interleaved:
        kv = (sds((m * C_KV_HEADS, LANES), F32), pl.BlockSpec((tm * C_KV_HEADS, LANES), row))
    else:
        kv = (sds((m, _C_KV_W), F32), pl.BlockSpec((tm, _C_KV_W), row))
    outs = (
        (sds((2 * C_HEADS, m, HEAD_DIM), BF16), pl.BlockSpec((2 * C_HEADS, tm, HEAD_DIM), hrow)),
        kv, (sds((2 * C_KV_HEADS, HEAD_DIM, m), BF16), pl.BlockSpec((2 * C_KV_HEADS, HEAD_DIM, tm), hcol)),
        kv, (sds((m, _C_KV_W), BF16), pl.BlockSpec((tm, _C_KV_W), row)),
    )
    return pl.pallas_call(
        functools.partial(_in_c_body, interleaved),
        grid=(m // tm,),
        in_specs=[pl.BlockSpec((tm, d), row), pl.BlockSpec((1, d), const), pl.BlockSpec((d, width), const),
                  pl.BlockSpec((tm, LANES), pos), pl.BlockSpec((tm, LANES), pos)],
        out_specs=tuple(o[1] for o in outs),
        out_shape=tuple(o[0] for o in outs),
        compiler_params=_cparams("parallel"),
        name="in_proj_c",
    )(x2d, g.reshape(1, d), w_bf16, cos, sin)


def _post_body(o_ref, x_ref, wout_ref, gpost_ref, gpre_ref, wup_ref, wdown_ref, gfpost_ref, y_ref):
    mix = _dot(o_ref[...], wout_ref[...])
    x1 = x_ref[...] + _rms(mix, gpost_ref[...], NORM_EPS)
    hn = _rms(x1, gpre_ref[...], NORM_EPS).astype(BF16)
    d_ff = wup_ref.shape[1]
    step = 1024
    f = jnp.zeros(x1.shape, F32)
    for c in range(d_ff // step):
        u = jnp.maximum(_dot(hn, wup_ref[:, c * step:(c + 1) * step]), 0.0)
        f = f + _dot((u * u).astype(BF16), wdown_ref[c * step:(c + 1) * step, :])
    y_ref[...] = x1 + _rms(f, gfpost_ref[...], NORM_EPS)


def _post_block(o2d, x2d, w_out, g_post, g_pre, w_up, w_down, g_fpost):
    m, d = x2d.shape
    tm = min(MLP_ROW_TILE if m % MLP_ROW_TILE == 0 else ROW_TILE, m)
    assert m % tm == 0
    d_ff = w_up.shape[1]
    row = lambda i: (i, 0)
    const = lambda i: (0, 0)
    resident = lambda shape: pl.BlockSpec(shape, const, pipeline_mode=pl.Buffered(1))
    return pl.pallas_call(
        _post_body,
        grid=(m // tm,),
        in_specs=[pl.BlockSpec((tm, o2d.shape[1]), row), pl.BlockSpec((tm, d), row),
                  resident(w_out.shape), pl.BlockSpec((1, d), const), pl.BlockSpec((1, d), const),
                  resident((d, d_ff)), resident((d_ff, d)), pl.BlockSpec((1, d), const)],
        out_specs=pl.BlockSpec((tm, d), row),
        out_shape=jax.ShapeDtypeStruct((m, d), F32),
        compiler_params=_cparams("parallel"),
        name="post_block",
    )(o2d, x2d, w_out, g_post.reshape(1, d), g_pre.reshape(1, d), w_up, w_down, g_fpost.reshape(1, d))


def _order_key(s):
    s = jnp.where(s == 0.0, 0.0, s)
    b = pltpu.bitcast(s, I32)
    return b ^ ((b >> 31) & I32(0x7FFFFFFF))


def _select_bias(sc_ref, key_ref, nch, n_sel, rows, width):
    def fill_keys(j, carry):
        s = sc_ref[j]
        key_ref[j] = _order_key(s)
        return jnp.maximum(carry[0], s), jnp.minimum(carry[1], s)

    lane_max, lane_min = lax.fori_loop(0, nch, fill_keys, (jnp.full((rows, width), -jnp.inf, F32),
                                                           jnp.full((rows, width), jnp.inf, F32)))

    def count(pred):
        def body(j, c):
            hit = jnp.where(pred(key_ref[j]), 1.0, 0.0)
            return c + sum(hit[:, i * LANES:(i + 1) * LANES] for i in range(width // LANES))

        c = lax.fori_loop(0, nch, body, jnp.zeros((rows, LANES), F32))
        return jnp.sum(c, axis=1, keepdims=True)

    def count_ge(cand):
        cand_b = jnp.broadcast_to(cand, (rows, width))
        return count(lambda k: k >= cand_b)

    k_sel = float(n_sel)
    low_score = jnp.min(lane_max if n_sel <= width else lane_min, axis=1, keepdims=True)
    lo = _order_key(low_score)
    hi = _order_key(jnp.max(lane_max, axis=1, keepdims=True))
    lo_b = jnp.broadcast_to(lo, (rows, width))
    hi = jnp.where(count(lambda k: k > lo_b) < k_sel, lo, hi)

    def open_rows(lo, hi):
        return jnp.max(jnp.where(lo < hi, 1.0, 0.0))

    def bisect(carry):
        lo, hi, _ = carry
        mid = (lo | hi) - ((lo ^ hi) >> 1)
        n_ge = count_ge(mid)
        lo = jnp.where(n_ge >= k_sel, mid, lo)
        hi = jnp.where(n_ge > k_sel, hi, jnp.where(n_ge == k_sel, mid, mid - 1))
        return lo, hi, open_rows(lo, hi)

    thr, _, _ = lax.while_loop(lambda carry: carry[2] > 0.5, bisect, (lo, hi, open_rows(lo, hi)))
    thr_b = jnp.broadcast_to(thr, (rows, width))
    need_b = jnp.broadcast_to(k_sel - count(lambda k: k > thr_b), (rows, width))
    r = lax.broadcasted_iota(I32, (width, width), 0)
    c = lax.broadcasted_iota(I32, (width, width), 1)
    upper = jnp.where(r < c, 1.0, 0.0).astype(BF16)
    ones = jnp.ones((width, width), BF16)

    def tie_body(j, run):
        k = key_ref[j]
        eq = jnp.where(k == thr_b, 1.0, 0.0)
        eq16 = eq.astype(BF16)
        rank = run + _dot(eq16, upper)
        bias = jnp.where(k > thr_b, 0.0, jnp.where(eq * jnp.where(rank < need_b, 1.0, 0.0) > 0.5, 0.0, MASKED))
        sc_ref[j] = jnp.where(sc_ref[j] == -jnp.inf, MASKED, bias)
        return run + _dot(eq16, ones)

    lax.fori_loop(0, nch, tie_body, jnp.zeros((rows, width), F32))


def _weighted_values(p16, v16, v_transposed):
    return _dot_nt(p16, v16) if v_transposed else _dot(p16, v16)


def _softmax_step(s, values, m, l, acc, v_transposed=False):
    m_new = jnp.maximum(m, jnp.max(s, axis=1, keepdims=True))
    alpha = jnp.exp2(m - m_new)
    p = jnp.exp2(s - m_new)
    l = alpha * l + jnp.sum(p, axis=1, keepdims=True)
    p16 = p.astype(BF16)
    if not isinstance(values, (list, tuple)):
        pv = _weighted_values(p16, values, v_transposed)
    else:
        pv = sum(_weighted_values(p16[:, i * LANES:(i + 1) * LANES], v, v_transposed) for i, v in enumerate(values))
    return m_new, l, alpha * acc + pv


def _stick_matrices():
    r = lax.broadcasted_iota(I32, (2 * LANES, 2 * LANES), 0) & (LANES - 1)
    c = lax.broadcasted_iota(I32, (2 * LANES, 2 * LANES), 1)
    return jnp.where((c >= LANES) | (r > c), 1.0, 0.0).astype(BF16)


def _softmax_ones_step(s, v_ext16, m, la):
    tiles = s.shape[1] // LANES
    m_new = jnp.maximum(m, jnp.max(s.astype(BF16), axis=1, keepdims=True).astype(F32))
    alpha = jnp.exp2(m - m_new)
    p16 = jnp.exp2(s - jnp.concatenate([m_new] * tiles, axis=1)).astype(BF16)
    return m_new, jnp.concatenate([alpha] * (la.shape[1] // LANES), axis=1) * la + _dot(p16, v_ext16)


def _stick_weights(zs, valid, run, mats):
    rows = zs[0].shape[0]
    softplus = [jnp.maximum(z, 0.0) + jnp.log2(1.0 + jnp.exp2(-jnp.abs(z))) for z in zs]
    split = []
    for sp in softplus:
        log_stay = -sp if valid is None else jnp.where(valid, -sp, 0.0)
        hi = log_stay.astype(BF16)
        lo = (log_stay - hi.astype(F32)).astype(BF16)
        split.append(jnp.concatenate([hi, lo], axis=1))
    sums = _dot(split[0] if len(split) == 1 else jnp.concatenate(split, axis=0), mats)
    ws = []
    for n, (z, sp) in enumerate(zip(zs, softplus)):
        part = sums[n * rows:(n + 1) * rows]
        w = jnp.exp2((z - sp) + part[:, :LANES] + run)
        if valid is not None:
            w = jnp.where(valid, w, 0.0)
        ws.append(w.astype(BF16))
        run = run + part[:, LANES:]
    return run, ws


def _chunk_count(qi, qb=QB):
    return lax.shift_right_logical((qi + 1) * qb - 1, KC.bit_length() - 1) + 1


def _chunk_lanes(j):
    return pl.ds(pl.multiple_of(j * KC, KC), KC)


def _att0_prompt_body(n_sel, aq_ref, iq_ref, iw_ref, bq_ref, akt_ref, av_ref, ikt_ref, bkt_ref, bv_ref,
                      o_ref, sc_ref, key_ref, m_ref, la_ref, run_ref, acc_ref):
    qi = pl.program_id(1)
    n_kc = _chunk_count(qi)
    row = lax.broadcasted_iota(I32, (QB, KC), 0)
    col = lax.broadcasted_iota(I32, (QB, KC), 1)
    qpos = qi * QB + row

    def score_chunk(j, c):
        kt = ikt_ref[:, _chunk_lanes(j)]
        acc = jnp.zeros((QB, KC), F32)
        for h in range(IDX_HEADS):
            acc = acc + jnp.maximum(_dot(iq_ref[h], kt), 0.0) * iw_ref[:, h:h + 1]
        sc_ref[j] = jnp.where(j * KC + col <= qpos, acc, -jnp.inf)
        return c

    lax.fori_loop(0, n_kc, score_chunk, 0)
    _select_bias(sc_ref, key_ref, n_kc, n_sel, QB, KC)

    m_ref[...] = jnp.full_like(m_ref, MASKED)
    la_ref[...] = jnp.zeros_like(la_ref)
    rep = A_HEADS // A_KV_HEADS
    ones = jnp.ones((KC, LANES), BF16)

    def dsa_chunk(j, c):
        bias = jnp.concatenate([sc_ref[j]] * rep, axis=0)
        v_ext = jnp.concatenate([av_ref[_chunk_lanes(j), :], ones], axis=1)
        for g in range(A_KV_HEADS):
            q = aq_ref[rep * g:rep * (g + 1)].reshape(rep * QB, HEAD_DIM)
            s = _dot(q, akt_ref[g, :, _chunk_lanes(j)]) + bias
            m_ref[g], la_ref[g] = _softmax_ones_step(s, v_ext, m_ref[g], la_ref[g])
        return c

    lax.fori_loop(0, n_kc, dsa_chunk, 0)
    outs = []
    for h in range(A_HEADS):
        g, r = h // rep, h % rep
        la = la_ref[g][r * QB:(r + 1) * QB]
        outs.append(la[:, g * HEAD_DIM:(g + 1) * HEAD_DIM] / la[:, LANES:LANES + HEAD_DIM])
    o_ref[:, :A_HEADS * HEAD_DIM] = jnp.concatenate(outs, axis=1).astype(o_ref.dtype)

    run_ref[...] = jnp.zeros_like(run_ref)
    acc_ref[...] = jnp.zeros_like(acc_ref)
    mats = _stick_matrices()
    rows_all = B_HEADS * QB
    qpos_c = qi * QB + (lax.broadcasted_iota(I32, (rows_all, LANES), 0) & (QB - 1))
    col_c = lax.broadcasted_iota(I32, (rows_all, LANES), 1)

    def sb_keys(starts, on_diagonal):
        keys = [pl.ds(pl.multiple_of(start, LANES), LANES) for start in starts]
        zs = [jnp.concatenate([_dot(bq_ref[h], bkt_ref[h, :, k]) for h in range(B_HEADS)], axis=0) for k in keys]
        valid = (starts[0] + col_c < qpos_c) if on_diagonal else None
        run_ref[...], ws = _stick_weights(zs, valid, run_ref[...], mats)
        for pair in range(B_HEADS // 2):
            rows = slice(2 * pair * QB, 2 * (pair + 1) * QB)
            acc_ref[pair] += sum(_dot(w16[rows], bv_ref[k, pair * LANES:(pair + 1) * LANES]) for w16, k in zip(ws, keys))

    n_full = qi * (QB // KC)
    for sub in reversed(range(QB // LANES)):
        sb_keys([qi * QB + sub * LANES], True)

    def sb_live():
        return jnp.max(run_ref[...]) > STICK_UNDERFLOW_LOG2

    def sb_chunk(carry):
        i = carry[0]
        j = n_full - 1 - i
        sb_keys([j * KC + sub * LANES for sub in reversed(range(KC // LANES))], False)
        return i + 1, sb_live()

    lax.while_loop(lambda carry: (carry[0] < n_full) & carry[1], sb_chunk, (I32(0), sb_live()))
    outs = [acc_ref[h // 2][(h % 2) * QB:(h % 2 + 1) * QB, (h % 2) * HEAD_DIM:(h % 2 + 1) * HEAD_DIM]
            for h in range(B_HEADS)]
    o_ref[:, A_HEADS * HEAD_DIM:] = jnp.concatenate(outs, axis=1).astype(o_ref.dtype)


def _att0_prompt(aq, iq, iw, bq, akt, av16, ikt, bkt, bv16, bsz, t):
    nq = t // QB
    n_sel = min(TOPK_MAX, t // 4)
    qh = lambda b, q: (0, b * nq + q, 0)
    qrow = lambda b, q: (b * nq + q, 0)
    kt = lambda b, q: (0, 0, b)
    krow = lambda b, q: (b, 0)
    width = (A_HEADS + B_HEADS) * HEAD_DIM
    rows_a = (A_HEADS // A_KV_HEADS) * QB
    return pl.pallas_call(
        functools.partial(_att0_prompt_body, n_sel),
        grid=(bsz, nq),
        in_specs=[pl.BlockSpec((A_HEADS, QB, HEAD_DIM), qh), pl.BlockSpec((IDX_HEADS, QB, IDX_DIM), qh),
                  pl.BlockSpec((QB, IDX_HEADS), qrow), pl.BlockSpec((B_HEADS, QB, HEAD_DIM), qh),
                  pl.BlockSpec((A_KV_HEADS, HEAD_DIM, t), kt), pl.BlockSpec((t, _A_KV_W), krow),
                  pl.BlockSpec((IDX_DIM, t), lambda b, q: (0, b)),
                  pl.BlockSpec((B_HEADS, HEAD_DIM, t), kt), pl.BlockSpec((t, _B_W), krow)],
        out_specs=pl.BlockSpec((QB, width), qrow),
        out_shape=jax.ShapeDtypeStruct((bsz * t, width), BF16),
        scratch_shapes=[pltpu.VMEM((t // KC, QB, KC), F32), pltpu.VMEM((t // KC, QB, KC), I32),
                        pltpu.VMEM((A_KV_HEADS, rows_a, LANES), F32), pltpu.VMEM((A_KV_HEADS, rows_a, 2 * LANES), F32),
                        pltpu.VMEM((B_HEADS * QB, LANES), F32), pltpu.VMEM((B_HEADS // 2, 2 * QB, LANES), F32)],
        compiler_params=_cparams("parallel", "arbitrary"),
        name="att0_prompt",
    )(aq, iq, iw, bq, akt, av16, ikt, bkt, bv16)


def _diff_lambda(lq1_ref, lk1_ref, lq2_ref, lk2_ref, lambda_init):
    s1 = jnp.sum(lq1_ref[...] * lk1_ref[...], axis=1, keepdims=True)
    s2 = jnp.sum(lq2_ref[...] * lk2_ref[...], axis=1, keepdims=True)
    return jnp.exp(s1) - jnp.exp(s2) + lambda_init


def _diff_finish(a1, l1, a2, l2, lam, gsub, lambda_init):
    o = a1 / l1 - lam * (a2 / l2)
    return _rms(o, gsub, SUBLN_EPS) * (1.0 - lambda_init)


def _att1_prompt_body(lambda_init, cq_ref, ckt_ref, cv_ref, lq1_ref, lk1_ref, lq2_ref, lk2_ref, gsub_ref,
                      o_ref, m_ref, la_ref):
    qi = pl.program_id(1)
    qb = o_ref.shape[0]
    rep = C_HEADS // C_KV_HEADS
    rows = rep * qb
    n_kc = _chunk_count(qi, qb)
    row = lax.broadcasted_iota(I32, (rows, KC), 0)
    col = lax.broadcasted_iota(I32, (rows, KC), 1)
    qpos = qi * qb + (row & (qb - 1))
    m_ref[...] = jnp.full_like(m_ref, MASKED)
    la_ref[...] = jnp.zeros_like(la_ref)
    ones = jnp.ones((KC, LANES), BF16)

    def chunk(j, on_diagonal):
        bias = jnp.where(j * KC + col <= qpos, 0.0, MASKED) if on_diagonal else None
        for g in range(C_KV_HEADS):
            v_ext = jnp.concatenate([cv_ref[_chunk_lanes(j), g * LANES:(g + 1) * LANES], ones], axis=1)
            for half in range(2):
                u = 2 * g + half
                q = cq_ref[rep * u:rep * (u + 1)].reshape(rows, HEAD_DIM)
                s = _dot(q, ckt_ref[u, :, _chunk_lanes(j)])
                if on_diagonal:
                    s = s + bias
                m_ref[u], la_ref[u] = _softmax_ones_step(s, v_ext, m_ref[u], la_ref[u])

    def full_chunk(j, c):
        chunk(j, False)
        return c

    lax.fori_loop(0, n_kc - 1, full_chunk, 0)
    chunk(n_kc - 1, True)

    lam = _diff_lambda(lq1_ref, lk1_ref, lq2_ref, lk2_ref, lambda_init)
    gsub = gsub_ref[...]
    for h in range(C_HEADS):
        g, r = h // rep, h % rep
        la1 = la_ref[2 * g][r * qb:(r + 1) * qb]
        la2 = la_ref[2 * g + 1][r * qb:(r + 1) * qb]
        o = _diff_finish(la1[:, :LANES], la1[:, LANES:], la2[:, :LANES], la2[:, LANES:], lam, gsub, lambda_init)
        o_ref[:, h * LANES:(h + 1) * LANES] = o.astype(o_ref.dtype)


def _att1_prompt(cq, ckt, cv16, lq1, lk1, lq2, lk2, gsub, lambda_init, bsz, t):
    qb = QB_DIFF
    nq = t // qb
    qh = lambda b, q: (0, b * nq + q, 0)
    qrow = lambda b, q: (b * nq + q, 0)
    const = lambda b, q: (0, 0)
    vec = lambda a: a.reshape(1, -1)
    rows = (C_HEADS // C_KV_HEADS) * qb
    return pl.pallas_call(
        functools.partial(_att1_prompt_body, lambda_init),
        grid=(bsz, nq),
        in_specs=[pl.BlockSpec((2 * C_HEADS, qb, HEAD_DIM), qh),
                  pl.BlockSpec((2 * C_KV_HEADS, HEAD_DIM, t), lambda b, q: (0, 0, b)),
                  pl.BlockSpec((t, _C_KV_W), lambda b, q: (b, 0)),
                  pl.BlockSpec((1, HEAD_DIM), const), pl.BlockSpec((1, HEAD_DIM), const),
                  pl.BlockSpec((1, HEAD_DIM), const), pl.BlockSpec((1, HEAD_DIM), const),
                  pl.BlockSpec((1, LANES), const)],
        out_specs=pl.BlockSpec((qb, _C_Q_W), qrow),
        out_shape=jax.ShapeDtypeStruct((bsz * t, _C_Q_W), BF16),
        scratch_shapes=[pltpu.VMEM((2 * C_KV_HEADS, rows, LANES), F32),
                        pltpu.VMEM((2 * C_KV_HEADS, rows, 2 * LANES), F32)],
        compiler_params=_cparams("parallel", "arbitrary"),
        name="att1_prompt",
    )(cq, ckt, cv16, vec(lq1), vec(lk1), vec(lq2), vec(lk2), vec(gsub))


def _page_specs(block, n_groups, group, newest_first):
    zeros = (0,) * (len(block) - 2)

    def index(i):
        if newest_first:
            return lambda b, s, pt: (0, pt[b, jnp.minimum(n_groups - s, n_groups - 1) * group + i]) + zeros
        return lambda b, s, pt: (0, pt[b, jnp.minimum(s, n_groups - 1) * group + i]) + zeros

    return [pl.BlockSpec(block, index(i)) for i in range(group)]


def _keys_on_lanes(cache):
    nd = cache.ndim
    return jnp.transpose(cache, (0, 1) + tuple(range(3, nd)) + (2,))


def _att0_sample_walk_body(dec, group, pt_ref, iq_ref, iw_ref, bq_ref, ikn_ref, bkn_ref, bvn_ref, *refs):
    del pt_ref
    ikc_refs, bkc_refs, bvc_refs = refs[:group], refs[group:2 * group], refs[2 * group:3 * group]
    sc_o, ob_o, run_ref, acc_ref = refs[3 * group:]
    s = pl.program_id(1)
    rows = B_HEADS * dec
    row = lax.broadcasted_iota(I32, (rows, LANES), 0)
    col = lax.broadcasted_iota(I32, (rows, LANES), 1)
    t_of_row = row & (dec - 1)
    mats = _stick_matrices()

    def scores(kt16):
        lg = jnp.maximum(_dot(iq_ref[...], kt16), 0.0) * iw_ref[...]
        out = lg[0:dec]
        for h in range(1, IDX_HEADS):
            out = out + lg[h * dec:(h + 1) * dec]
        return out

    def stick(kts, vts, valid):
        run_ref[...], ws = _stick_weights([_dot(bq_ref[...], kt) for kt in kts], valid, run_ref[...], mats)
        acc_ref[...] += sum(_dot_nt(w16, vt) for w16, vt in zip(ws, vts))

    @pl.when(s == 0)
    def _():
        run_ref[...] = jnp.zeros_like(run_ref)
        acc_ref[...] = jnp.zeros_like(acc_ref)
        causal = (col[:dec] < dec) & (col[:dec] <= row[:dec])
        sc_o[0] = jnp.where(causal, scores(ikn_ref[...]), -jnp.inf)
        for i in range(1, group):
            sc_o[i] = jnp.full((dec, LANES), -jnp.inf, F32)
        stick([bkn_ref[...]], [bvn_ref[...]], (col < dec) & (col < t_of_row))

    @pl.when(s > 0)
    def _():
        for i in range(group):
            sc_o[i] = scores(ikc_refs[i][...].astype(BF16))

        @pl.when(jnp.max(run_ref[...]) > STICK_UNDERFLOW_LOG2)
        def _():
            stick([bkc_refs[i][...].reshape(_B_W, LANES).astype(BF16) for i in reversed(range(group))],
                  [bvc_refs[i][...].reshape(_B_W, LANES).astype(BF16) for i in reversed(range(group))], None)

    @pl.when(s == pl.num_programs(1) - 1)
    def _():
        acc = acc_ref[...]
        ob_o[...] = jnp.concatenate([acc[h * dec:(h + 1) * dec, h * HEAD_DIM:(h + 1) * HEAD_DIM]
                                     for h in range(B_HEADS)], axis=1).astype(ob_o.dtype)


def _att0_sample_walk(page_table, iq_st, iw_st, bq_bd, ikt_new, bkt_new, bvt_new, cache_kidx, cache_bk, cache_bv):
    db, n_pages = page_table.shape
    group = math.gcd(PAGE_GROUP, n_pages)
    n_groups = n_pages // group
    dec = iq_st.shape[1] // IDX_HEADS
    rows = B_HEADS * dec
    per_b = lambda b, s, pt: (b, 0, 0)
    chunks_of_step = lambda b, s, pt: (jnp.where(s == 0, n_groups, n_groups - s), b, 0, 0)
    b_page = (None, None, B_HEADS, HEAD_DIM, LANES)
    grid_spec = pltpu.PrefetchScalarGridSpec(
        num_scalar_prefetch=1,
        grid=(db, n_groups + 1),
        in_specs=[pl.BlockSpec((None, rows, IDX_DIM), per_b), pl.BlockSpec((None, rows, 1), per_b),
                  pl.BlockSpec((None, rows, _B_W), per_b),
                  pl.BlockSpec((None, IDX_DIM, LANES), per_b), pl.BlockSpec((None, _B_W, LANES), per_b),
                  pl.BlockSpec((None, _B_W, LANES), per_b)]
        + _page_specs((None, None, IDX_DIM, LANES), n_groups, group, True)
        + _page_specs(b_page, n_groups, group, True) + _page_specs(b_page, n_groups, group, True),
        out_specs=[pl.BlockSpec((group, None, dec, LANES), chunks_of_step), pl.BlockSpec((None, dec, _B_W), per_b)],
        scratch_shapes=[pltpu.VMEM((rows, LANES), F32), pltpu.VMEM((rows, _B_W), F32)],
    )
    return pl.pallas_call(
        functools.partial(_att0_sample_walk_body, dec, group),
        grid_spec=grid_spec,
        out_shape=(jax.ShapeDtypeStruct(((n_groups + 1) * group, db, dec, LANES), F32),
                   jax.ShapeDtypeStruct((db, dec, _B_W), BF16)),
        compiler_params=_cparams("parallel", "arbitrary"),
        name="att0_sample_walk",
    )(page_table, iq_st, iw_st, bq_bd, ikt_new, bkt_new, bvt_new,
      *([cache_kidx] * group + [cache_bk] * group + [cache_bv] * group))


def _select_sample_body(n_sel, nch, sc_ref, bias_o, key_ref):
    rows = sc_ref.shape[1]
    bias_o[...] = sc_ref[...]
    _select_bias(bias_o, key_ref, nch, n_sel, rows, LANES)


def _select_sample(scores, nch, n_sel):
    n_alloc, m, _ = scores.shape
    rows = min(m, 64)
    spec = pl.BlockSpec((n_alloc, rows, LANES), lambda i: (0, i, 0))
    return pl.pallas_call(
        functools.partial(_select_sample_body, n_sel, nch),
        grid=(m // rows,),
        in_specs=[spec],
        out_specs=spec,
        out_shape=jax.ShapeDtypeStruct(scores.shape, F32),
        scratch_shapes=[pltpu.VMEM((n_alloc, rows, LANES), I32)],
        compiler_params=_cparams("parallel"),
        name="select_sample",
    )(scores)


def _att0_sample_dsa_body(n_groups, dec, group, pt_ref, aq_ref, bias_ref, akn_ref, avn_ref, *refs):
    del pt_ref
    akc_refs, avc_refs = refs[:group], refs[group:2 * group]
    oa_o, m_ref, l_ref, acc_ref = refs[2 * group:]
    s = pl.program_id(1)

    @pl.when(s == 0)
    def _():
        m_ref[...] = jnp.full_like(m_ref, MASKED)
        l_ref[...] = jnp.zeros_like(l_ref)
        acc_ref[...] = jnp.zeros_like(acc_ref)

    def step(kts, vts):
        n = len(kts)
        bias = jnp.concatenate([jnp.concatenate([bias_ref[i]] * A_HEADS, axis=0) for i in range(n)], axis=1)
        sc = jnp.concatenate([_dot(aq_ref[...], kt) for kt in kts], axis=1) + bias
        m_ref[...], l_ref[...], acc_ref[...] = _softmax_step(sc, vts, m_ref[...], l_ref[...], acc_ref[...],
                                                             v_transposed=True)

    @pl.when(s < n_groups)
    def _():
        step([r[...].reshape(_A_KV_W, LANES).astype(BF16) for r in akc_refs],
             [r[...].reshape(_A_KV_W, LANES).astype(BF16) for r in avc_refs])

    @pl.when(s == n_groups)
    def _():
        step([akn_ref[...]], [avn_ref[...]])
        o = acc_ref[...] / l_ref[...]
        rep = A_HEADS // A_KV_HEADS
        oa_o[...] = jnp.concatenate(
            [o[h * dec:(h + 1) * dec, (h // rep) * HEAD_DIM:(h // rep + 1) * HEAD_DIM] for h in range(A_HEADS)],
            axis=1).astype(oa_o.dtype)


def _att0_sample_dsa(page_table, aq_bd, bias, akt_new, avt_new, cache_akt, cache_avt):
    db, n_pages = page_table.shape
    group = math.gcd(PAGE_GROUP, n_pages)
    n_groups = n_pages // group
    rows = aq_bd.shape[1]
    dec = rows // A_HEADS
    per_b = lambda b, s, pt: (b, 0, 0)
    a_page = (None, None, A_KV_HEADS, HEAD_DIM, LANES)
    grid_spec = pltpu.PrefetchScalarGridSpec(
        num_scalar_prefetch=1,
        grid=(db, n_groups + 1),
        in_specs=[pl.BlockSpec((None, rows, _A_KV_W), per_b),
                  pl.BlockSpec((group, None, dec, LANES), lambda b, s, pt: (s, b, 0, 0)),
                  pl.BlockSpec((None, _A_KV_W, LANES), per_b), pl.BlockSpec((None, _A_KV_W, LANES), per_b)]
        + _page_specs(a_page, n_groups, group, False) + _page_specs(a_page, n_groups, group, False),
        out_specs=pl.BlockSpec((None, dec, A_HEADS * HEAD_DIM), per_b),
        scratch_shapes=[pltpu.VMEM((rows, 1), F32), pltpu.VMEM((rows, 1), F32), pltpu.VMEM((rows, _A_KV_W), F32)],
    )
    return pl.pallas_call(
        functools.partial(_att0_sample_dsa_body, n_groups, dec, group),
        grid_spec=grid_spec,
        out_shape=jax.ShapeDtypeStruct((db, dec, A_HEADS * HEAD_DIM), BF16),
        compiler_params=_cparams("parallel", "arbitrary"),
        name="att0_sample_dsa",
    )(page_table, aq_bd, bias, akt_new, avt_new, *([cache_akt] * group + [cache_avt] * group))


def _att1_sample_body(n_groups, dec, group, lambda_init, pt_ref, cq_ref, ckn_ref, cvn_ref, *refs):
    del pt_ref
    ckc_refs, cvc_refs = refs[:group], refs[group:2 * group]
    lq1_ref, lk1_ref, lq2_ref, lk2_ref, gsub_ref, o_o, m_ref, l_ref, acc_ref = refs[2 * group:]
    s = pl.program_id(1)
    rows_g = 2 * (C_HEADS // C_KV_HEADS) * dec

    @pl.when(s == 0)
    def _():
        m_ref[...] = jnp.full_like(m_ref, MASKED)
        l_ref[...] = jnp.zeros_like(l_ref)
        acc_ref[...] = jnp.zeros_like(acc_ref)

    def step(keys_of, values_of, n_pages, bias):
        sc = jnp.concatenate(
            [jnp.concatenate([_dot_nt(cq_ref[g * rows_g:(g + 1) * rows_g, :], keys_of(g, i)) for i in range(n_pages)],
                             axis=1) for g in range(C_KV_HEADS)], axis=0)
        if bias is not None:
            sc = sc + bias
        m = m_ref[...]
        m_new = jnp.maximum(m, jnp.max(sc, axis=1, keepdims=True))
        alpha = jnp.exp2(m - m_new)
        p = jnp.exp2(sc - m_new)
        p16 = p.astype(BF16)
        pv = jnp.concatenate(
            [sum(_dot(p16[g * rows_g:(g + 1) * rows_g, i * LANES:(i + 1) * LANES], values_of(g, i))
                 for i in range(n_pages)) for g in range(C_KV_HEADS)], axis=0)
        m_ref[...] = m_new
        l_ref[...] = alpha * l_ref[...] + jnp.sum(p, axis=1, keepdims=True)
        acc_ref[...] = alpha * acc_ref[...] + pv

    @pl.when(s < n_groups)
    def _():
        head_rows = [pl.ds(g, LANES, stride=C_KV_HEADS) for g in range(C_KV_HEADS)]
        step(lambda g, i: ckc_refs[i][head_rows[g], :].astype(BF16),
             lambda g, i: cvc_refs[i][head_rows[g], :].astype(BF16), group, None)

    @pl.when(s == n_groups)
    def _():
        rows = C_KV_HEADS * rows_g
        row = lax.broadcasted_iota(I32, (rows, LANES), 0)
        col = lax.broadcasted_iota(I32, (rows, LANES), 1)
        bias = jnp.where(col <= (row & (dec - 1)), 0.0, MASKED)
        step(lambda g, i: ckn_ref[:, g * LANES:(g + 1) * LANES],
             lambda g, i: cvn_ref[:, g * LANES:(g + 1) * LANES], 1, bias)
        lam = _diff_lambda(lq1_ref, lk1_ref, lq2_ref, lk2_ref, lambda_init)
        acc = acc_ref[...]
        l = l_ref[...]
        outs = []
        rep = C_HEADS // C_KV_HEADS
        for h in range(C_HEADS):
            g, r = h // rep, h % rep
            r1 = ((2 * g) * rep + r) * dec
            r2 = ((2 * g + 1) * rep + r) * dec
            outs.append(_diff_finish(acc[r1:r1 + dec], l[r1:r1 + dec], acc[r2:r2 + dec], l[r2:r2 + dec],
                                     lam, gsub_ref[...], lambda_init))
        o_o[...] = jnp.concatenate(outs, axis=1).astype(o_o.dtype)


def _att1_sample(page_table, cq_bd, ck_new, cv_new, cache_ck, cache_cv, lq1, lk1, lq2, lk2, gsub, lambda_init):
    db, n_pages = page_table.shape
    group = math.gcd(PAGE_GROUP, n_pages)
    n_groups = n_pages // group
    rows = cq_bd.shape[1]
    dec = rows // (2 * C_HEADS)
    per_b = lambda b, s, pt: (b, 0, 0)
    const = lambda b, s, pt: (0, 0)
    vec = lambda a: a.reshape(1, -1)
    c_page = (None, None, LANES * C_KV_HEADS, LANES)
    grid_spec = pltpu.PrefetchScalarGridSpec(
        num_scalar_prefetch=1,
        grid=(db, n_groups + 1),
        in_specs=[pl.BlockSpec((None, rows, LANES), per_b),
                  pl.BlockSpec((None, LANES, _C_KV_W), per_b), pl.BlockSpec((None, LANES, _C_KV_W), per_b)]
        + _page_specs(c_page, n_groups, group, False) + _page_specs(c_page, n_groups, group, False)
        + [pl.BlockSpec((1, HEAD_DIM), const), pl.BlockSpec((1, HEAD_DIM), const),
           pl.BlockSpec((1, HEAD_DIM), const), pl.BlockSpec((1, HEAD_DIM), const), pl.BlockSpec((1, LANES), const)],
        out_specs=pl.BlockSpec((None, dec, _C_Q_W), per_b),
        scratch_shapes=[pltpu.VMEM((rows, 1), F32), pltpu.VMEM((rows, 1), F32), pltpu.VMEM((rows, LANES), F32)],
    )
    return pl.pallas_call(
        functools.partial(_att1_sample_body, n_groups, dec, group, lambda_init),
        grid_spec=grid_spec,
        out_shape=jax.ShapeDtypeStruct((db, dec, _C_Q_W), BF16),
        compiler_params=_cparams("parallel", "arbitrary"),
        name="att1_sample",
    )(page_table, cq_bd, ck_new, cv_new, *([cache_ck] * group + [cache_cv] * group),
      vec(lq1), vec(lk1), vec(lq2), vec(lk2), vec(gsub))


def _rope_tables(pos):
    half = HEAD_DIM // 2
    inv_freq = ROPE_THETA ** (-jnp.arange(half, dtype=F32) / half)
    ang = pos.astype(F32)[:, None] * inv_freq[None, :]
    cos, sin = jnp.cos(ang), jnp.sin(ang)
    return jnp.concatenate([cos] * 4, axis=1), jnp.concatenate([-sin, sin, -sin, sin], axis=1)


def _block_diag_rows(q_hm, db, dec, slot_of_head, n_slots):
    heads, _, w = q_hm.shape
    q = q_hm.reshape(heads, db, dec, w).transpose(1, 0, 2, 3)
    onehot = (jnp.asarray(slot_of_head)[:, None] == jnp.arange(n_slots)[None, :]).astype(q.dtype)
    return (q[:, :, :, None, :] * onehot[None, :, None, :, None]).reshape(db, heads * dec, n_slots * w)


def kernel(x_prompt, x_sample, cache_a_k, cache_a_v, cache_a_kidx, cache_b_k, cache_b_v, cache_c_k, cache_c_v,
           page_table, norm_mix_pre, norm_mix_post, norm_ffn_pre, norm_ffn_post, w_in_ab, w_out_ab, w_in_c, w_out_c,
           c_lambda_q1, c_lambda_k1, c_lambda_q2, c_lambda_k2, c_subln, w_up, w_down):
    bsz, t, d = x_prompt.shape
    db, dec, _ = x_sample.shape
    n_phys, page = cache_a_k.shape[1], cache_a_k.shape[2]
    n_pages = page_table.shape[1]
    past = n_pages * page
    assert page == LANES and t % KC == 0 and dec & (dec - 1) == 0 and dec <= 8
    mp, ms = bsz * t, db * dec

    cos_p, sin_p = _rope_tables(jnp.arange(t, dtype=jnp.int32))
    cos_s, sin_s = _rope_tables(jnp.arange(past, past + dec, dtype=jnp.int32))
    cos_s, sin_s = jnp.tile(cos_s, (db, 1)), jnp.tile(sin_s, (db, 1))

    xp = x_prompt.reshape(mp, d)
    xs = x_sample.reshape(ms, d)
    w_ab = _prep_w_ab(w_in_ab[0])
    w_out_ab16 = w_out_ab[0].astype(BF16)
    w_c16 = w_in_c[0].astype(BF16)
    w_out_c16 = w_out_c[0].astype(BF16)
    w_up16 = w_up.astype(BF16)
    w_down16 = w_down.astype(BF16)
    lambda_init = 0.8 - 0.6 * math.exp(-0.3 * 1)

    (aq_p, iq_p, bq_p, ak_p, akt_p, av_p, av16_p, ik_p, ikt_p, iw_p, bk_p, bkt_p, bv_p, bv16_p) = _in_proj_ab(
        xp, norm_mix_pre[0], w_ab, cos_p, sin_p, batch_tokens=t)
    o_p = _att0_prompt(aq_p, iq_p, iw_p, bq_p, akt_p, av16_p, ikt_p, bkt_p, bv16_p, bsz, t)
    xp = _post_block(o_p, xp, w_out_ab16, norm_mix_post[0], norm_ffn_pre[0], w_up16[0], w_down16[0], norm_ffn_post[0])

    (aq_s, iq_s, bq_s, ak_s, _, av_s, _, ik_s, _, iw_s, bk_s, _, bv_s, _) = _in_proj_ab(
        xs, norm_mix_pre[0], w_ab, cos_s, sin_s)
    iq_st = iq_s.reshape(IDX_HEADS, db, dec, IDX_DIM).transpose(1, 0, 2, 3).reshape(db, IDX_HEADS * dec, IDX_DIM)
    iw_st = iw_s.reshape(db, dec, IDX_HEADS).transpose(0, 2, 1).reshape(db, IDX_HEADS * dec, 1)
    bq_bd = _block_diag_rows(bq_s, db, dec, list(range(B_HEADS)), B_HEADS)
    aq_bd = _block_diag_rows(aq_s, db, dec, [h // (A_HEADS // A_KV_HEADS) for h in range(A_HEADS)], A_KV_HEADS)
    new16 = lambda a: jnp.pad(a.reshape(db, dec, -1).astype(BF16), ((0, 0), (0, LANES - dec), (0, 0)))
    new16t = lambda a: jnp.pad(a.reshape(db, dec, -1).astype(BF16).transpose(0, 2, 1), ((0, 0), (0, 0), (0, LANES - dec)))
    scores, ob_s = _att0_sample_walk(
        page_table, iq_st, iw_st, bq_bd, new16t(ik_s), new16t(bk_s), new16t(bv_s),
        _keys_on_lanes(cache_a_kidx), _keys_on_lanes(cache_b_k), _keys_on_lanes(cache_b_v))
    n_sel_s = min(TOPK_MAX, (past + dec) // 4)
    n_alloc = scores.shape[0]
    bias = _select_sample(scores.reshape(n_alloc, ms, LANES), n_pages + 1, n_sel_s).reshape(n_alloc, db, dec, LANES)
    oa_s = _att0_sample_dsa(page_table, aq_bd, bias, new16t(ak_s), new16t(av_s),
                            _keys_on_lanes(cache_a_k), _keys_on_lanes(cache_a_v))
    o_s = jnp.concatenate([oa_s, ob_s], axis=-1).reshape(ms, -1)
    xs = _post_block(o_s, xs, w_out_ab16, norm_mix_post[0], norm_ffn_pre[0], w_up16[0], w_down16[0], norm_ffn_post[0])

    cq_p, ck_p, ckt_p, cv_p, cv16_p = _in_proj_c(xp, norm_mix_pre[1], w_c16, cos_p, sin_p, interleaved=True)
    o_p = _att1_prompt(cq_p, ckt_p, cv16_p, c_lambda_q1[0], c_lambda_k1[0], c_lambda_q2[0], c_lambda_k2[0], c_subln[0],
                       lambda_init, bsz, t)
    yp = _post_block(o_p, xp, w_out_c16, norm_mix_post[1], norm_ffn_pre[1], w_up16[1], w_down16[1], norm_ffn_post[1])

    cq_s, ck_s, _, cv_s, _ = _in_proj_c(xs, norm_mix_pre[1], w_c16, cos_s, sin_s)
    rep = C_HEADS // C_KV_HEADS
    cq_bd = _block_diag_rows(cq_s, db, dec, [(slot // rep) % 2 for slot in range(2 * C_HEADS)], 2)
    o_s = _att1_sample(page_table, cq_bd, new16(ck_s), new16(cv_s),
                       cache_c_k.reshape(-1, n_phys, page * C_KV_HEADS, LANES),
                       cache_c_v.reshape(-1, n_phys, page * C_KV_HEADS, LANES),
                       c_lambda_q1[0], c_lambda_k1[0], c_lambda_q2[0], c_lambda_k2[0], c_subln[0], lambda_init)
    ys = _post_block(o_s.reshape(ms, -1), xs, w_out_c16, norm_mix_post[1], norm_ffn_pre[1], w_up16[1], w_down16[1],
                     norm_ffn_post[1])

    def rows(a, b_, t_, heads, width):
        shape = (1, b_, t_, heads, width) if heads else (1, b_, t_, width)
        return a.reshape(shape)

    per_head = lambda a: jnp.transpose(a, (0, 3, 1, 2))[None]
    return (yp.reshape(bsz, t, d), ys.reshape(db, dec, d),
            per_head(ak_p), per_head(av_p), jnp.transpose(ik_p, (0, 2, 1))[None], per_head(bk_p), per_head(bv_p),
            rows(ck_p, bsz, t, C_KV_HEADS, 2 * HEAD_DIM), rows(cv_p, bsz, t, C_KV_HEADS, 2 * HEAD_DIM),
            rows(ak_s, db, dec, A_KV_HEADS, HEAD_DIM), rows(av_s, db, dec, A_KV_HEADS, HEAD_DIM),
            rows(ik_s, db, dec, 0, IDX_DIM),
            rows(bk_s, db, dec, B_HEADS, HEAD_DIM), rows(bv_s, db, dec, B_HEADS, HEAD_DIM),
            rows(ck_s, db, dec, C_KV_HEADS, 2 * HEAD_DIM), rows(cv_s, db, dec, C_KV_HEADS, 2 * HEAD_DIM))
```
